```python
import jax, jax.numpy as jnp
from jax import lax
import numpy as np

D_MODEL = 1024
BATCH = 16
SEQ = 2048
DEPTH = 1

GRID_W = 64
HEAD_DIM = 64
RWKV_HEADS = 8
RWKV_WIDTH = RWKV_HEADS * HEAD_DIM
DECAY_RANK = 64
ICLR_RANK = 64
GATE_RANK = 128
NA_HEADS = 8
NA_WIDTH = NA_HEADS * HEAD_DIM
NA_MAX_WIN_ROWS = 8
NA_WIN_COLS = 16
N_GROUPS = 4
EXPERTS_PER_GROUP = 8
N_EXPERTS = N_GROUPS * EXPERTS_PER_GROUP
D_EXPERT = 256
TOP_K_IN_GROUP = 2
RMS_EPS = 1e-6
GN_EPS = 64e-5
NEG_INF = -1e30
RWKV_COLS = 3 * RWKV_WIDTH + 2 * DECAY_RANK + 2 * ICLR_RANK + GATE_RANK
NA_COLS = 3 * NA_WIDTH
GATE_COLS = 2 * D_MODEL
IN_COLS = RWKV_COLS + NA_COLS + GATE_COLS
RWKV_SPLITS = (RWKV_WIDTH, 2 * RWKV_WIDTH, 3 * RWKV_WIDTH,
               3 * RWKV_WIDTH + DECAY_RANK, 3 * RWKV_WIDTH + 2 * DECAY_RANK,
               3 * RWKV_WIDTH + 2 * DECAY_RANK + ICLR_RANK,
               3 * RWKV_WIDTH + 2 * DECAY_RANK + 2 * ICLR_RANK)
IN_SPLITS = (RWKV_COLS, RWKV_COLS + NA_WIDTH, RWKV_COLS + 2 * NA_WIDTH, RWKV_COLS + 3 * NA_WIDTH)

kernel_name = 'hybrid_rwkv7_natten2d_hiermoe_block'


def rmsnorm(x, g):
    x32 = x.astype(jnp.float32)
    y = x32 * lax.rsqrt(jnp.mean(x32 * x32, axis=-1, keepdims=True) + RMS_EPS)
    return (y * g).astype(x.dtype)


def centred_token_shift(p, mu_prev, mu_next):
    prev = jnp.pad(p, ((0, 0), (1, 0), (0, 0)))[:, :-1]
    nxt = jnp.pad(p, ((0, 0), (0, 1), (0, 0)))[:, 1:]
    return p + mu_prev * (prev - p) + mu_next * (nxt - p)


def rwkv_log_decay(lora, w0, up):
    w_raw = (w0 + jnp.tanh(lora) @ up).astype(jnp.float32)
    return -jnp.exp(-jax.nn.softplus(-w_raw) - 0.5)


def rwkv_step(S, inp):
    r_t, w_t, k_t, v_t, kk_t, a_t = inp
    sa = jnp.einsum('dbhij,dbhj->dbhi', S, -kk_t)
    S = (S * w_t[..., None, :] + sa[..., :, None] * (kk_t * a_t)[..., None, :]
         + v_t[..., :, None] * k_t[..., None, :])
    y = jnp.einsum('dbhij,dbhj->dbhi', S, r_t)
    return S, y


def rwkv7_bidirectional(p, mu_prev, mu_next, w0_f, w_up_f, w0_b, w_up_b, a0_f, a_up_f,
                        a0_b, a_up_b, g_up, k_k, k_a, r_k, lnx_g, lnx_b):
    B, T, _ = p.shape
    dt = p.dtype
    p = centred_token_shift(p, mu_prev, mu_next)
    r, k, v, wl_f, wl_b, al_f, al_b, gl = jnp.split(p, RWKV_SPLITS, axis=-1)
    lw_f = rwkv_log_decay(wl_f, w0_f, w_up_f)
    lw_b = rwkv_log_decay(wl_b, w0_b, w_up_b)
    a_f = jax.nn.sigmoid(a0_f + al_f @ a_up_f)
    a_b = jax.nn.sigmoid(a0_b + al_b @ a_up_b)
    g = jax.nn.sigmoid(gl) @ g_up

    def heads(t):
        return t.reshape(B, T, RWKV_HEADS, HEAD_DIM).astype(jnp.float32)

    r_h, v_h = heads(r), heads(v)
    kk = heads(k * k_k)
    kk = kk * lax.rsqrt(jnp.sum(kk * kk, axis=-1, keepdims=True) + 1e-12)
    k_f = heads(k * (1.0 + (a_f - 1.0) * k_a))
    k_b = heads(k * (1.0 + (a_b - 1.0) * k_a))
    a_fh, a_bh = heads(a_f), heads(a_b)
    w_fh, w_bh = jnp.exp(heads(lw_f)), jnp.exp(heads(lw_b))

    def both(f, b):
        return jnp.stack([f, jnp.flip(b, axis=1)], axis=0).transpose(2, 0, 1, 3, 4)

    xs = (both(r_h, r_h), both(w_fh, w_bh), both(k_f, k_b), both(v_h, v_h),
          both(kk, kk), both(a_fh, a_bh))
    S0 = jnp.zeros((2, B, RWKV_HEADS, HEAD_DIM, HEAD_DIM), jnp.float32)
    _, ys = lax.scan(rwkv_step, S0, xs)
    y_f = ys[:, 0].transpose(1, 0, 2, 3)
    y_b = jnp.flip(ys[:, 1].transpose(1, 0, 2, 3), axis=1)
    y = y_f + y_b
    mean = jnp.mean(y, axis=-1, keepdims=True)
    var = jnp.mean(jnp.square(y - mean), axis=-1, keepdims=True)
    y = ((y - mean) * lax.rsqrt(var + GN_EPS)).reshape(B, T, RWKV_WIDTH) * lnx_g + lnx_b
    bonus = jnp.sum(r_h * (k_f + k_b) * r_k, axis=-1, keepdims=True) * v_h
    y = (y + bonus.reshape(B, T, RWKV_WIDTH)) * g
    return y.astype(dt)


def neighbourhood_attention_2d(q, k, v, q_gain, k_gain, rpb):
    B, T, _ = q.shape
    dt = q.dtype
    rows = T // GRID_W
    kh = min(NA_MAX_WIN_ROWS, rows)

    def qk_norm(t, g):
        t32 = t.reshape(B, T, NA_HEADS, HEAD_DIM).astype(jnp.float32)
        return t32 * lax.rsqrt(jnp.mean(t32 * t32, axis=-1, keepdims=True) + RMS_EPS) * g

    def grid(t):
        return t.reshape(B, rows, GRID_W, NA_HEADS, HEAD_DIM).transpose(0, 3, 1, 2, 4)

    qg = grid(qk_norm(q, q_gain) * (HEAD_DIM ** -0.5))
    kg = grid(qk_norm(k, k_gain))
    vg = grid(v.reshape(B, T, NA_HEADS, HEAD_DIM).astype(jnp.float32))

    cols = np.arange(GRID_W)
    col_start = np.clip(cols - NA_WIN_COLS // 2, 0, GRID_W - NA_WIN_COLS)
    col_mask = ((cols[None, :] >= col_start[:, None])
                & (cols[None, :] < col_start[:, None] + NA_WIN_COLS))
    col_off = np.clip(cols[None, :] - cols[:, None] + NA_WIN_COLS - 1, 0, 2 * NA_WIN_COLS - 2)

    def row_block(i):
        start = jnp.clip(i - kh // 2, 0, rows - kh)
        k_blk = lax.dynamic_slice_in_dim(kg, start, kh, axis=2)
        v_blk = lax.dynamic_slice_in_dim(vg, start, kh, axis=2)
        q_blk = lax.dynamic_index_in_dim(qg, i, axis=2, keepdims=False)
        s = jnp.einsum('bhqd,bhrkd->bhqrk', q_blk, k_blk)
        row_off = start + jnp.arange(kh) - i + NA_MAX_WIN_ROWS - 1
        bias = rpb.astype(jnp.float32)[:, row_off][:, :, col_off]
        s = s + bias.transpose(0, 2, 1, 3)[None]
        s = jnp.where(col_mask[:, None, :], s, NEG_INF)
        prob = jax.nn.softmax(s, axis=(-2, -1))
        return jnp.einsum('bhqrk,bhrkd->bhqd', prob, v_blk)

    out = lax.map(row_block, jnp.arange(rows))
    out = out.transpose(1, 0, 3, 2, 4).reshape(B, T, NA_WIDTH)
    return out.astype(dt)


def hierarchical_moe(h, w_rg, b_rg, w_re, b_re, w_gate_e, w_up_e, w_down_e):
    B, T, D = h.shape
    ht = h.reshape(B * T, D)
    n_tok = ht.shape[0]
    g_logits = (ht @ w_rg + b_rg).astype(jnp.float32)
    g_sel = jnp.argmax(g_logits, axis=-1)
    p_group = jnp.take_along_axis(jax.nn.softmax(g_logits, axis=-1), g_sel[:, None], axis=-1)
    e_logits = (ht @ w_re + b_re).astype(jnp.float32).reshape(n_tok, N_GROUPS, EXPERTS_PER_GROUP)
    e_in_group = jnp.take_along_axis(e_logits, g_sel[:, None, None], axis=1)[:, 0]
    top_val, top_idx = lax.top_k(e_in_group, TOP_K_IN_GROUP)
    weights = p_group * jax.nn.softmax(top_val, axis=-1)
    expert_id = g_sel[:, None] * EXPERTS_PER_GROUP + top_idx
    combine = jnp.sum(jax.nn.one_hot(expert_id, N_EXPERTS, dtype=jnp.float32) * weights[..., None], axis=1)
    combine = combine.astype(h.dtype)
    y = jnp.zeros_like(ht)
    for e in range(N_EXPERTS):
        he = jax.nn.silu(ht @ w_gate_e[e]) * (ht @ w_up_e[e])
        y = y + combine[:, e:e + 1] * (he @ w_down_e[e])
    return y.reshape(B, T, D)


def setup_inputs(seed: int = 0) -> dict:
    key = jax.random.key(seed)
    ks = iter(jax.random.split(key, 40))
    L = DEPTH

    def nrm(shape, scale):
        return jax.random.normal(next(ks), shape, jnp.float32) * scale

    def unif(shape, lo, hi):
        return jax.random.uniform(next(ks), shape, jnp.float32, lo, hi)

    return {
        'x': nrm((BATCH, SEQ, D_MODEL), 1.0),
        'norm1_g': 1.0 + nrm((L, D_MODEL), 0.02),
        'w_in': nrm((L, D_MODEL, IN_COLS), D_MODEL ** -0.5),
        'mu_prev': unif((L, RWKV_COLS), 0.0, 0.5),
        'mu_next': unif((L, RWKV_COLS), 0.0, 0.5),
        'w0_f': unif((L, RWKV_WIDTH), -6.0, 1.0),
        'w_up_f': nrm((L, DECAY_RANK, RWKV_WIDTH), DECAY_RANK ** -0.5),
        'w0_b': unif((L, RWKV_WIDTH), -6.0, 1.0),
        'w_up_b': nrm((L, DECAY_RANK, RWKV_WIDTH), DECAY_RANK ** -0.5),
        'a0_f': nrm((L, RWKV_WIDTH), 0.5),
        'a_up_f': nrm((L, ICLR_RANK, RWKV_WIDTH), ICLR_RANK ** -0.5),
        'a0_b': nrm((L, RWKV_WIDTH), 0.5),
        'a_up_b': nrm((L, ICLR_RANK, RWKV_WIDTH), ICLR_RANK ** -0.5),
        'g_up': nrm((L, GATE_RANK, RWKV_WIDTH), GATE_RANK ** -0.5),
        'k_k': 0.85 + nrm((L, RWKV_WIDTH), 0.05),
        'k_a': 1.0 + nrm((L, RWKV_WIDTH), 0.05),
        'r_k': nrm((L, RWKV_HEADS, HEAD_DIM), 0.1),
        'lnx_g': 1.0 + nrm((L, RWKV_WIDTH), 0.02),
        'lnx_b': nrm((L, RWKV_WIDTH), 0.01),
        'q_gain': 1.0 + nrm((L, HEAD_DIM), 0.02),
        'k_gain': 1.0 + nrm((L, HEAD_DIM), 0.02),
        'rpb': nrm((L, NA_HEADS, 2 * NA_MAX_WIN_ROWS - 1, 2 * NA_WIN_COLS - 1), 0.1),
        'b_gate': nrm((L, GATE_COLS), 0.01),
        'w_o_rwkv': nrm((L, RWKV_WIDTH, D_MODEL), RWKV_WIDTH ** -0.5),
        'w_o_na': nrm((L, NA_WIDTH, D_MODEL), NA_WIDTH ** -0.5),
        'w_out': nrm((L, D_MODEL, D_MODEL), D_MODEL ** -0.5),
        'norm2_g': 1.0 + nrm((L, D_MODEL), 0.02),
        'w_router_group': nrm((L, D_MODEL, N_GROUPS), D_MODEL ** -0.5),
        'b_router_group': nrm((L, N_GROUPS), 0.01),
        'w_router_expert': nrm((L, D_MODEL, N_EXPERTS), D_MODEL ** -0.5),
        'b_router_expert': nrm((L, N_EXPERTS), 0.01),
        'w_gate_e': nrm((L, N_EXPERTS, D_MODEL, D_EXPERT), D_MODEL ** -0.5),
        'w_up_e': nrm((L, N_EXPERTS, D_MODEL, D_EXPERT), D_MODEL ** -0.5),
        'w_down_e': nrm((L, N_EXPERTS, D_EXPERT, D_MODEL), D_EXPERT ** -0.5),
    }


def reference(x, norm1_g, w_in, mu_prev, mu_next, w0_f, w_up_f, w0_b, w_up_b, a0_f, a_up_f,
              a0_b, a_up_b, g_up, k_k, k_a, r_k, lnx_g, lnx_b, q_gain, k_gain, rpb, b_gate,
              w_o_rwkv, w_o_na, w_out, norm2_g, w_router_group, b_router_group,
              w_router_expert, b_router_expert, w_gate_e, w_up_e, w_down_e):
    for l in range(DEPTH):
        h = rmsnorm(x, norm1_g[l])
        p = h @ w_in[l]
        p_rwkv, q, k, v, gate_logits = jnp.split(p, IN_SPLITS, axis=-1)
        y_a = rwkv7_bidirectional(p_rwkv, mu_prev[l], mu_next[l], w0_f[l], w_up_f[l], w0_b[l],
                                  w_up_b[l], a0_f[l], a_up_f[l], a0_b[l], a_up_b[l], g_up[l],
                                  k_k[l], k_a[l], r_k[l], lnx_g[l], lnx_b[l]) @ w_o_rwkv[l]
        y_b = neighbourhood_attention_2d(q, k, v, q_gain[l], k_gain[l], rpb[l]) @ w_o_na[l]
        g_a, g_b = jnp.split(jax.nn.sigmoid(gate_logits + b_gate[l]), 2, axis=-1)
        x = x + (g_a * y_a + g_b * y_b) @ w_out[l]
        x = x + hierarchical_moe(rmsnorm(x, norm2_g[l]), w_router_group[l], b_router_group[l],
                                 w_router_expert[l], b_router_expert[l], w_gate_e[l],
                                 w_up_e[l], w_down_e[l])
    return x
```

```python
import functools

import numpy as np
import jax
import jax.numpy as jnp
from jax import lax
from jax.experimental import pallas as pl
from jax.experimental.pallas import tpu as pltpu

D_MODEL = 1024
HEAD_DIM = 64
RWKV_HEADS = 8
RWKV_WIDTH = RWKV_HEADS * HEAD_DIM
DECAY_RANK = 64
ICLR_RANK = 64
GATE_RANK = 128
NA_HEADS = 8
NA_WIDTH = NA_HEADS * HEAD_DIM
GRID_W = 64
NA_MAX_WIN_ROWS = 8
NA_WIN_COLS = 16
N_GROUPS = 4
EXPERTS_PER_GROUP = 8
N_EXPERTS = N_GROUPS * EXPERTS_PER_GROUP
D_EXPERT = 256
RMS_EPS = 1e-6
GN_EPS = 64e-5
NEG_INF = -1e30
RWKV_COLS = 3 * RWKV_WIDTH + 2 * DECAY_RANK + 2 * ICLR_RANK + GATE_RANK
NA_COLS = 3 * NA_WIDTH
GATE_COLS = 2 * D_MODEL
IN_COLS = RWKV_COLS + NA_COLS + GATE_COLS

LANES = 128
BF16_SUBLANES = 16
VMEM_LIMIT_BYTES = 56 * 1024 * 1024

CHUNK = 64
PAIR = LANES // HEAD_DIM
N_PAIRS = RWKV_HEADS // PAIR
EXP_M05 = float(np.exp(-0.5))
ROUTER_LANES = LANES

F32 = jnp.float32
BF16 = jnp.bfloat16


def _dot(a, b):
    return jnp.dot(a, b, preferred_element_type=F32)


def _dot_nt(a, b):
    return lax.dot_general(a, b, (((1,), (1,)), ((), ())), preferred_element_type=F32)


def _dot_tn(a, b):
    return lax.dot_general(a, b, (((0,), (0,)), ((), ())), preferred_element_type=F32)


def _split2(z):
    hi = z.astype(BF16)
    lo = (z - hi.astype(F32)).astype(BF16)
    return hi, lo


def _seg_dot(z, bd):
    hi, lo = _split2(z)
    return _dot(hi, bd) + _dot(lo, bd)


def _block_diag_heads(width, value):
    idx = np.arange(width) // HEAD_DIM
    return jnp.asarray((idx[:, None] == idx[None, :]).astype(np.float32) * value, dtype=BF16)


def _inproj_kernel(x_ref, g1_ref, w_ref, qg_ref, kg_ref, bd_ref,
                   prw_ref, q_ref, k_ref, v_ref, gate_ref):
    x = x_ref[...]
    ms = jnp.mean(x * x, axis=-1, keepdims=True)
    h = (x * lax.rsqrt(ms + RMS_EPS) * g1_ref[...]).astype(BF16)

    def proj(lo, hi):
        return _dot(h, w_ref[:, lo:hi])

    col = 0
    while col < RWKV_COLS:
        nxt = min(col + 512, RWKV_COLS)
        prw_ref[:, col:nxt] = proj(col, nxt).astype(BF16)
        col = nxt

    def head_rms(t, gain):
        msq = _dot((t * t).astype(BF16), bd_ref[...])
        return t * lax.rsqrt(msq + RMS_EPS) * gain

    c0 = RWKV_COLS
    q_ref[...] = head_rms(proj(c0, c0 + NA_WIDTH), qg_ref[...]).astype(BF16)
    k_ref[...] = head_rms(proj(c0 + NA_WIDTH, c0 + 2 * NA_WIDTH), kg_ref[...]).astype(BF16)
    v_ref[...] = proj(c0 + 2 * NA_WIDTH, c0 + 3 * NA_WIDTH).astype(BF16)
    c1 = c0 + NA_COLS
    for j in range(GATE_COLS // 512):
        gate_ref[:, j * 512:(j + 1) * 512] = proj(c1 + j * 512, c1 + (j + 1) * 512).astype(BF16)


def _in_proj(xf, g1, w_in_b, q_gain, k_gain, tm):
    n = xf.shape[0]
    qg = (jnp.tile(q_gain, NA_HEADS) * (HEAD_DIM ** -0.5)).reshape(1, NA_WIDTH)
    kg = jnp.tile(k_gain, NA_HEADS).reshape(1, NA_WIDTH)
    bd = _block_diag_heads(NA_WIDTH, 1.0 / HEAD_DIM)
    const = lambda i: (0, 0)
    row = lambda i: (i, 0)
    return pl.pallas_call(
        _inproj_kernel,
        grid=(n // tm,),
        in_specs=[
            pl.BlockSpec((tm, D_MODEL), row),
            pl.BlockSpec((1, D_MODEL), const),
            pl.BlockSpec((D_MODEL, IN_COLS), const),
            pl.BlockSpec((1, NA_WIDTH), const),
            pl.BlockSpec((1, NA_WIDTH), const),
            pl.BlockSpec((NA_WIDTH, NA_WIDTH), const),
        ],
        out_specs=[
            pl.BlockSpec((tm, RWKV_COLS), row),
            pl.BlockSpec((tm, NA_WIDTH), row),
            pl.BlockSpec((tm, NA_WIDTH), row),
            pl.BlockSpec((tm, NA_WIDTH), row),
            pl.BlockSpec((tm, GATE_COLS), row),
        ],
        out_shape=[
            jax.ShapeDtypeStruct((n, RWKV_COLS), BF16),
            jax.ShapeDtypeStruct((n, NA_WIDTH), BF16),
            jax.ShapeDtypeStruct((n, NA_WIDTH), BF16),
            jax.ShapeDtypeStruct((n, NA_WIDTH), BF16),
            jax.ShapeDtypeStruct((n, GATE_COLS), BF16),
        ],
        compiler_params=pltpu.CompilerParams(
            dimension_semantics=("parallel",), vmem_limit_bytes=VMEM_LIMIT_BYTES),
        name="in_proj",
    )(xf, g1.reshape(1, D_MODEL), w_in_b, qg, kg, bd)


def _expand(z, m0):
    return jnp.concatenate([jnp.where(m0, z, 0.0), jnp.where(m0, 0.0, z)], axis=0).astype(BF16)


def _unit_tri_inverse(nmat, t_idx, s_idx, eye, m0, reverse):
    def level_mask(b):
        same = (t_idx // (2 * b)) == (s_idx // (2 * b))
        t_hi = (t_idx % (2 * b)) >= b
        s_hi = (s_idx % (2 * b)) >= b
        if reverse:
            return same & jnp.logical_not(t_hi) & s_hi
        return same & t_hi & jnp.logical_not(s_hi)

    nb = nmat.astype(BF16)
    d = eye + jnp.where(level_mask(1), nmat, 0.0)
    b = 2
    while b < CHUNK:
        e = _dot(nb, _expand(d, m0))
        f = _dot(d.astype(BF16), _expand(e, m0))
        d = d + jnp.where(level_mask(b), f, 0.0)
        b *= 2
    return d


def _rwkv_chunk(ps, s_ref, d, prm, with_gate):
    reverse = d == 1
    w = RWKV_WIDTH
    r = ps[:, 0:w]
    k = ps[:, w:2 * w]
    v = ps[:, 2 * w:3 * w]
    c_dec = 3 * w
    c_icl = c_dec + 2 * DECAY_RANK
    c_gate = c_icl + 2 * ICLR_RANK
    bd1 = prm["bd1"][...]

    w_raw = prm["w0"][d:d + 1, :] + _dot(jnp.tanh(ps[:, c_dec:c_icl]).astype(BF16), prm["wup"][d])
    lw = -EXP_M05 * jax.nn.sigmoid(w_raw)
    a = jax.nn.sigmoid(prm["a0"][d:d + 1, :] + _dot(ps[:, c_icl:c_gate].astype(BF16), prm["aup"][d]))
    kk = k * prm["k_k"][...]
    kk = kk * lax.rsqrt(_seg_dot(kk * kk, bd1) + 1e-12)
    kd = k * (1.0 + (a - 1.0) * prm["k_a"][...])
    bonus = _seg_dot(r * kd * prm["r_k"][...], bd1) * v

    rowc = lax.broadcasted_iota(jnp.int32, (CHUNK, CHUNK), 0)
    colc = lax.broadcasted_iota(jnp.int32, (CHUNK, CHUNK), 1)
    tri = jnp.where((colc >= rowc) if reverse else (colc <= rowc), 1.0, 0.0).astype(BF16)
    lw_hi = lw.astype(BF16)
    rem = lw - lw_hi.astype(F32)
    lw_mid = rem.astype(BF16)
    lw_lo = (rem - lw_mid.astype(F32)).astype(BF16)
    cum = _dot(tri, lw_hi) + _dot(tri, lw_mid) + _dot(tri, lw_lo)
    cum_x = cum - lw
    tot = jnp.sum(lw, axis=0, keepdims=True)
    e_rest = jnp.exp(tot - cum)
    e_neg = jnp.exp(-cum)
    beta = kk * a
    rt = r * jnp.exp(cum)
    at = -kk * jnp.exp(cum_x)
    kh = kd * e_neg
    bh = beta * e_neg
    kw = kd * e_rest
    bw = beta * e_rest
    wc = jnp.exp(tot)

    lane = lax.broadcasted_iota(jnp.int32, (1, LANES), 1)
    m0 = lane < HEAD_DIM
    t_idx = lax.broadcasted_iota(jnp.int32, (CHUNK, LANES), 0)
    s_idx = lax.broadcasted_iota(jnp.int32, (CHUNK, LANES), 1) % CHUNK
    strict = (s_idx > t_idx) if reverse else (s_idx < t_idx)
    incl = (s_idx >= t_idx) if reverse else (s_idx <= t_idx)
    eye = jnp.where(s_idx == t_idx, 1.0, 0.0)
    bd_r = lax.broadcasted_iota(jnp.int32, (LANES, LANES), 0) // HEAD_DIM
    bd_c = lax.broadcasted_iota(jnp.int32, (LANES, LANES), 1) // HEAD_DIM
    bdmask = bd_r == bd_c

    ys = []
    for pr in range(N_PAIRS):
        sl = slice(pr * LANES, (pr + 1) * LANES)
        rt_p, at_p, v_p = rt[:, sl], at[:, sl], v[:, sl]
        x = jnp.concatenate([at_p, rt_p], axis=0).astype(BF16)
        ab = _dot_nt(x, _expand(bh[:, sl], m0))
        ak = _dot_nt(x, _expand(kh[:, sl], m0))
        nmat = jnp.where(strict, ab[:CHUNK], 0.0)
        a_rb = jnp.where(incl, ab[CHUNK:], 0.0)
        a_ak = jnp.where(strict, ak[:CHUNK], 0.0)
        a_rk = jnp.where(incl, ak[CHUNK:], 0.0)
        tinv = _unit_tri_inverse(nmat, t_idx, s_idx, eye, m0, reverse)
        av = _dot(jnp.concatenate([a_ak, a_rk], axis=0).astype(BF16), _expand(v_p, m0))
        tu = _dot(tinv.astype(BF16),
                  jnp.concatenate([_expand(at_p, m0), _expand(av[:CHUNK], m0)], axis=1))
        ut, ul = tu[:, :LANES], tu[:, LANES:]
        qy = _dot(a_rb.astype(BF16), jnp.concatenate([_expand(ut, m0), _expand(ul, m0)], axis=1))
        q = rt_p + qy[:, :LANES]
        yl = qy[:, LANES:] + av[CHUNK:]
        s0 = s_ref[d, pr]
        s0b = s0.astype(BF16)
        ys.append(_dot_nt(q.astype(BF16), s0b) + yl)
        bw_b = bw[:, sl].astype(BF16)
        tn = _dot_tn(tu.astype(BF16), bw_b)
        vk = _dot_tn(v_p.astype(BF16), kw[:, sl].astype(BF16))
        mt = jnp.where(bdmask, tn[:LANES], 0.0)
        bt = jnp.where(bdmask, tn[LANES:] + vk, 0.0)
        s_ref[d, pr] = s0 * wc[:, sl] + _dot(s0b, mt.astype(BF16)) + bt
    y = jnp.concatenate(ys, axis=1)
    gate = None
    if with_gate:
        gate = _dot(jax.nn.sigmoid(ps[:, c_gate:c_gate + GATE_RANK]).astype(BF16), prm["gup"][...])
    return y, bonus, gate


def _rwkv_kernel(pf_ref, pfp_ref, pfn_ref, pb_ref, pbp_ref, pbn_ref,
                 mup_ref, mun_ref, w0_ref, wup_ref, a0_ref, aup_ref, gup_ref,
                 kk_ref, ka_ref, rk_ref, bd1_ref,
                 yf_ref, bonf_ref, g_ref, yb_ref, bonb_ref,
                 buf_ref, s_ref, *, nc):
    i = pl.program_id(1)

    @pl.when(i == 0)
    def _():
        s_ref[...] = jnp.zeros_like(s_ref)

    prm = dict(w0=w0_ref, wup=wup_ref, a0=a0_ref, aup=aup_ref, gup=gup_ref,
               k_k=kk_ref, k_a=ka_ref, r_k=rk_ref, bd1=bd1_ref)
    pad = BF16_SUBLANES

    def shifted(main_ref, prev_ref, next_ref, c):
        has_prev = jnp.where(c > 0, 1.0, 0.0)
        has_next = jnp.where(c < nc - 1, 1.0, 0.0)
        buf_ref[0:pad, :] = prev_ref[...].astype(F32) * has_prev
        buf_ref[pad:pad + CHUNK, :] = main_ref[...].astype(F32)
        buf_ref[pad + CHUNK:, :] = next_ref[...].astype(F32) * has_next
        p = buf_ref[pad:pad + CHUNK, :]
        p_prev = buf_ref[pad - 1:pad - 1 + CHUNK, :]
        p_next = buf_ref[pad + 1:pad + 1 + CHUNK, :]
        return p + mup_ref[...] * (p_prev - p) + mun_ref[...] * (p_next - p)

    ps = shifted(pf_ref, pfp_ref, pfn_ref, i)
    y, bonus, gate = _rwkv_chunk(ps, s_ref, 0, prm, True)
    yf_ref[...] = y
    bonf_ref[...] = bonus
    g_ref[...] = gate

    ps = shifted(pb_ref, pbp_ref, pbn_ref, nc - 1 - i)
    y, bonus, _ = _rwkv_chunk(ps, s_ref, 1, prm, False)
    yb_ref[...] = y
    bonb_ref[...] = bonus


def _rwkv(prw, batch, seq, mu_prev, mu_next, w0_f, w_up_f, w0_b, w_up_b, a0_f, a_up_f, a0_b, a_up_b,
          g_up, k_k, k_a, r_k):
    n = batch * seq
    nc = seq // CHUNK
    sub = CHUNK // BF16_SUBLANES
    nsub = n // BF16_SUBLANES
    w = RWKV_WIDTH
    zd = jnp.zeros((DECAY_RANK, w), F32)
    zi = jnp.zeros((ICLR_RANK, w), F32)
    wup = jnp.stack([jnp.concatenate([w_up_f, zd], 0), jnp.concatenate([zd, w_up_b], 0)]).astype(BF16)
    aup = jnp.stack([jnp.concatenate([a_up_f, zi], 0), jnp.concatenate([zi, a_up_b], 0)]).astype(BF16)
    w0 = jnp.stack([w0_f, w0_b])
    a0 = jnp.stack([a0_f, a0_b])
    bd1 = _block_diag_heads(w, 1.0)

    def fwd_c(b, i):
        return i

    def bwd_c(b, i):
        return nc - 1 - i

    def main(cf):
        return lambda b, i: (b * nc + cf(b, i), 0)

    def prev(cf):
        return lambda b, i: (jnp.maximum((b * nc + cf(b, i)) * sub - 1, 0), 0)

    def nxt(cf):
        return lambda b, i: (jnp.minimum((b * nc + cf(b, i)) * sub + sub, nsub - 1), 0)

    const2 = lambda b, i: (0, 0)
    const3 = lambda b, i: (0, 0, 0)
    pspec = lambda f: pl.BlockSpec((CHUNK, RWKV_COLS), f)
    nspec = lambda f: pl.BlockSpec((BF16_SUBLANES, RWKV_COLS), f)
    ospec = lambda f: pl.BlockSpec((CHUNK, w), f)
    vec = pl.BlockSpec((1, w), const2)
    out_sds = jax.ShapeDtypeStruct((n, w), F32)
    return pl.pallas_call(
        functools.partial(_rwkv_kernel, nc=nc),
        grid=(batch, nc),
        in_specs=[
            pspec(main(fwd_c)), nspec(prev(fwd_c)), nspec(nxt(fwd_c)),
            pspec(main(bwd_c)), nspec(prev(bwd_c)), nspec(nxt(bwd_c)),
            pl.BlockSpec((1, RWKV_COLS), const2), pl.BlockSpec((1, RWKV_COLS), const2),
            pl.BlockSpec((2, w), const2), pl.BlockSpec((2, 2 * DECAY_RANK, w), const3),
            pl.BlockSpec((2, w), const2), pl.BlockSpec((2, 2 * ICLR_RANK, w), const3),
            pl.BlockSpec((GATE_RANK, w), const2),
            vec, vec, vec,
            pl.BlockSpec((w, w), const2),
        ],
        out_specs=[ospec(main(fwd_c)), ospec(main(fwd_c)), ospec(main(fwd_c)),
                   ospec(main(bwd_c)), ospec(main(bwd_c))],
        out_shape=[out_sds] * 5,
        scratch_shapes=[
            pltpu.VMEM((CHUNK + 2 * BF16_SUBLANES, RWKV_COLS), F32),
            pltpu.VMEM((2, N_PAIRS, LANES, LANES), F32),
        ],
        compiler_params=pltpu.CompilerParams(
            dimension_semantics=("arbitrary", "arbitrary"), vmem_limit_bytes=VMEM_LIMIT_BYTES),
        name="rwkv7",
    )(prw, prw, prw, prw, prw, prw,
      mu_prev.reshape(1, RWKV_COLS), mu_next.reshape(1, RWKV_COLS), w0, wup, a0, aup,
      g_up.astype(BF16), k_k.reshape(1, w), k_a.reshape(1, w), r_k.reshape(1, w), bd1)


def _na_kernel(q_ref, k_ref, v_ref, bias_ref, o_ref, *, rows, kh):
    i = pl.program_id(1)
    start = jnp.clip(i - kh // 2, 0, rows - kh)
    off = pl.multiple_of(start * GRID_W, GRID_W)
    q = q_ref[...]
    kb = k_ref[pl.ds(off, kh * GRID_W), :]
    vb = v_ref[pl.ds(off, kh * GRID_W), :]
    lane = lax.broadcasted_iota(jnp.int32, (1, LANES), 1)
    m0 = lane < HEAD_DIM
    zero = jnp.zeros((), BF16)
    outs = []
    for pr in range(NA_HEADS // PAIR):
        sl = slice(pr * LANES, (pr + 1) * LANES)
        qp = q[:, sl]
        qs = jnp.concatenate([jnp.where(m0, qp, zero), jnp.where(m0, zero, qp)], axis=0)
        s = _dot_nt(qs, kb[:, sl]) + bias_ref[0, pr]
        m = jnp.max(s, axis=-1, keepdims=True)
        e = jnp.exp(s - m)
        l = jnp.sum(e, axis=-1, keepdims=True)
        pv = _dot(e.astype(BF16), vb[:, sl]) / l
        outs.append(jnp.where(m0, pv[:GRID_W], pv[GRID_W:]))
    o_ref[...] = jnp.concatenate(outs, axis=1).astype(BF16)


def _na_bias_table(rpb, kh):
    cols = np.arange(GRID_W)
    col_start = np.clip(cols - NA_WIN_COLS // 2, 0, GRID_W - NA_WIN_COLS)
    col_mask = (cols[None, :] >= col_start[:, None]) & (cols[None, :] < col_start[:, None] + NA_WIN_COLS)
    col_off = np.clip(cols[None, :] - cols[:, None] + NA_WIN_COLS - 1, 0, 2 * NA_WIN_COLS - 2)
    delta = np.arange(kh)
    row_off = np.arange(kh)[None, :] - delta[:, None] + NA_MAX_WIN_ROWS - 1
    t = rpb.astype(F32)[:, row_off]
    t = t[:, :, :, col_off]
    t = jnp.where(col_mask[None, None, None], t, NEG_INF)
    t = t.transpose(1, 0, 3, 2, 4).reshape(kh, NA_HEADS // PAIR, PAIR * GRID_W, kh * GRID_W)
    return t


def _natten(q, k, v, rpb, batch, seq):
    n = batch * seq
    rows = seq // GRID_W
    kh = min(NA_MAX_WIN_ROWS, rows)
    bias = _na_bias_table(rpb, kh)

    def delta(b, i):
        return i - jnp.clip(i - kh // 2, 0, rows - kh)

    return pl.pallas_call(
        functools.partial(_na_kernel, rows=rows, kh=kh),
        grid=(batch, rows),
        in_specs=[
            pl.BlockSpec((GRID_W, NA_WIDTH), lambda b, i: (b * rows + i, 0)),
            pl.BlockSpec((seq, NA_WIDTH), lambda b, i: (b, 0)),
            pl.BlockSpec((seq, NA_WIDTH), lambda b, i: (b, 0)),
            pl.BlockSpec((1, NA_HEADS // PAIR, PAIR * GRID_W, kh * GRID_W),
                         lambda b, i: (delta(b, i), 0, 0, 0)),
        ],
        out_specs=pl.BlockSpec((GRID_W, NA_WIDTH), lambda b, i: (b * rows + i, 0)),
        out_shape=jax.ShapeDtypeStruct((n, NA_WIDTH), BF16),
        compiler_params=pltpu.CompilerParams(
            dimension_semantics=("parallel", "arbitrary"), vmem_limit_bytes=VMEM_LIMIT_BYTES),
        name="natten2d",
    )(q, k, v, bias)


def _route(logits):
    lane = lax.broadcasted_iota(jnp.int32, (1, ROUTER_LANES), 1)
    big = jnp.int32(ROUTER_LANES)
    ninf = -jnp.inf
    gmask = lane < N_GROUPS
    gl = jnp.where(gmask, logits, ninf)
    gmax = jnp.max(gl, axis=-1, keepdims=True)
    g_sel = jnp.min(jnp.where(gl == gmax, lane, big), axis=-1, keepdims=True)
    p_group = 1.0 / jnp.sum(jnp.where(gmask, jnp.exp(logits - gmax), 0.0), axis=-1, keepdims=True)
    e_idx = lane - N_GROUPS
    emask = (e_idx >= 0) & (e_idx < N_EXPERTS) & ((e_idx // EXPERTS_PER_GROUP) == g_sel)
    ev = jnp.where(emask, logits, ninf)
    m1 = jnp.max(ev, axis=-1, keepdims=True)
    i1 = jnp.min(jnp.where(ev == m1, lane, big), axis=-1, keepdims=True)
    ev2 = jnp.where(lane == i1, ninf, ev)
    m2 = jnp.max(ev2, axis=-1, keepdims=True)
    i2 = jnp.min(jnp.where(ev2 == m2, lane, big), axis=-1, keepdims=True)
    t = jnp.exp(m2 - m1)
    w1 = p_group / (1.0 + t)
    w2 = p_group * t / (1.0 + t)
    return jnp.where(lane == i1, w1, 0.0) + jnp.where(lane == i2, w2, 0.0)


def _out_kernel(x_ref, yf_ref, yb_ref, bonf_ref, bonb_ref, g_ref, yna_ref, gate_ref,
                lng_ref, lnb_ref, bgate_ref, worw_ref, wona_ref, wout_ref, n2g_ref, wr_ref, br_ref, bd_ref,
                x1_ref, h2_ref, comb_ref):
    bd = bd_ref[...]
    y = yf_ref[...] + yb_ref[...]
    yc = y - _seg_dot(y, bd)
    var = _seg_dot(yc * yc, bd)
    yn = yc * lax.rsqrt(var + GN_EPS) * lng_ref[...] + lnb_ref[...]
    ya = (yn + bonf_ref[...] + bonb_ref[...]) * g_ref[...]
    ya_o = _dot(ya.astype(BF16), worw_ref[...])
    yb_o = _dot(yna_ref[...], wona_ref[...])
    gl = gate_ref[...].astype(F32) + bgate_ref[...]
    mix = jax.nn.sigmoid(gl[:, :D_MODEL]) * ya_o + jax.nn.sigmoid(gl[:, D_MODEL:]) * yb_o
    x1 = x_ref[...] + _dot(mix.astype(BF16), wout_ref[...])
    x1_ref[...] = x1
    ms = jnp.mean(x1 * x1, axis=-1, keepdims=True)
    h2 = x1 * lax.rsqrt(ms + RMS_EPS) * n2g_ref[...]
    h2_ref[...] = h2.astype(BF16)
    logits = jnp.dot(h2, wr_ref[...], precision=lax.Precision.HIGHEST,
                     preferred_element_type=F32) + br_ref[...]
    comb_ref[...] = _route(logits)


def _out_proj(xf, yf, yb, bonf, bonb, g, yna, gate, lnx_g, lnx_b, b_gate, w_o_rwkv, w_o_na, w_out,
              norm2_g, w_rg, b_rg, w_re, b_re, tm):
    n = xf.shape[0]
    w = RWKV_WIDTH
    padc = ROUTER_LANES - N_GROUPS - N_EXPERTS
    wr = jnp.concatenate([w_rg, w_re, jnp.zeros((D_MODEL, padc), F32)], axis=1)
    br = jnp.concatenate([b_rg, b_re, jnp.zeros((padc,), F32)]).reshape(1, ROUTER_LANES)
    bd = _block_diag_heads(w, 1.0 / HEAD_DIM)
    const = lambda i: (0, 0)
    row = lambda i: (i, 0)
    rs = lambda c: pl.BlockSpec((tm, c), row)
    cs = lambda r, c: pl.BlockSpec((r, c), const)
    return pl.pallas_call(
        _out_kernel,
        grid=(n // tm,),
        in_specs=[rs(D_MODEL), rs(w), rs(w), rs(w), rs(w), rs(w), rs(NA_WIDTH), rs(GATE_COLS),
                  cs(1, w), cs(1, w), cs(1, GATE_COLS), cs(w, D_MODEL), cs(NA_WIDTH, D_MODEL),
                  cs(D_MODEL, D_MODEL), cs(1, D_MODEL), cs(D_MODEL, ROUTER_LANES), cs(1, ROUTER_LANES),
                  cs(w, w)],
        out_specs=[rs(D_MODEL), rs(D_MODEL), rs(ROUTER_LANES)],
        out_shape=[jax.ShapeDtypeStruct((n, D_MODEL), F32),
                   jax.ShapeDtypeStruct((n, D_MODEL), BF16),
                   jax.ShapeDtypeStruct((n, ROUTER_LANES), F32)],
        compiler_params=pltpu.CompilerParams(
            dimension_semantics=("parallel",), vmem_limit_bytes=VMEM_LIMIT_BYTES),
        name="out_proj",
    )(xf, yf, yb, bonf, bonb, g, yna, gate,
      lnx_g.reshape(1, w), lnx_b.reshape(1, w), b_gate.reshape(1, GATE_COLS),
      w_o_rwkv.astype(BF16), w_o_na.astype(BF16), w_out.astype(BF16),
      norm2_g.reshape(1, D_MODEL), wr, br, bd)


def _moe_kernel(h_ref, x1_ref, comb_ref, wgu_ref, wd_ref, o_ref, acc_ref):
    e = pl.program_id(1)

    @pl.when(e == 0)
    def _():
        acc_ref[...] = jnp.zeros_like(acc_ref)

    gu = _dot(h_ref[...], wgu_ref[0])
    gate, up = gu[:, :D_EXPERT], gu[:, D_EXPERT:]
    lane = lax.broadcasted_iota(jnp.int32, (1, ROUTER_LANES), 1)
    c = jnp.sum(jnp.where(lane == e + N_GROUPS, comb_ref[...], 0.0), axis=-1, keepdims=True)
    he = gate * jax.nn.sigmoid(gate) * up * c
    acc_ref[...] += _dot(he.astype(BF16), wd_ref[0])

    @pl.when(e == N_EXPERTS - 1)
    def _():
        o_ref[...] = x1_ref[...] + acc_ref[...]


def _moe(h2, x1, comb, w_gate_e, w_up_e, w_down_e, tm):
    n = h2.shape[0]
    wgu = jnp.concatenate([w_gate_e, w_up_e], axis=-1).astype(BF16)
    wd = w_down_e.astype(BF16)
    row = lambda i, e: (i, 0)
    return pl.pallas_call(
        _moe_kernel,
        grid=(n // tm, N_EXPERTS),
        in_specs=[
            pl.BlockSpec((tm, D_MODEL), row),
            pl.BlockSpec((tm, D_MODEL), row),
            pl.BlockSpec((tm, ROUTER_LANES), row),
            pl.BlockSpec((1, D_MODEL, 2 * D_EXPERT), lambda i, e: (e, 0, 0)),
            pl.BlockSpec((1, D_EXPERT, D_MODEL), lambda i, e: (e, 0, 0)),
        ],
        out_specs=pl.BlockSpec((tm, D_MODEL), row),
        out_shape=jax.ShapeDtypeStruct((n, D_MODEL), F32),
        scratch_shapes=[pltpu.VMEM((tm, D_MODEL), F32)],
        compiler_params=pltpu.CompilerParams(
            dimension_semantics=("parallel", "arbitrary"), vmem_limit_bytes=VMEM_LIMIT_BYTES),
        name="moe",
    )(h2, x1, comb, wgu, wd)


def _row_tile(n, want):
    t = min(want, n)
    while n % t:
        t //= 2
    return t


def kernel(x, norm1_g, w_in, mu_prev, mu_next, w0_f, w_up_f, w0_b, w_up_b, a0_f, a_up_f, a0_b, a_up_b, g_up, k_k, k_a, r_k, lnx_g, lnx_b, q_gain, k_gain, rpb, b_gate, w_o_rwkv, w_o_na, w_out, norm2_g, w_router_group, b_router_group, w_router_expert, b_router_expert, w_gate_e, w_up_e, w_down_e):
    batch, seq, d = x.shape
    assert d == D_MODEL and seq % CHUNK == 0 and seq % GRID_W == 0
    n = batch * seq
    for l in range(norm1_g.shape[0]):
        xf = x.reshape(n, d)
        prw, q, k, v, gate = _in_proj(xf, norm1_g[l], w_in[l].astype(BF16), q_gain[l], k_gain[l],
                                      _row_tile(n, 512))
        yf, bonf, g, yb, bonb = _rwkv(prw, batch, seq, mu_prev[l], mu_next[l], w0_f[l], w_up_f[l],
                                      w0_b[l], w_up_b[l], a0_f[l], a_up_f[l], a0_b[l], a_up_b[l],
                                      g_up[l], k_k[l], k_a[l], r_k[l].reshape(-1))
        yna = _natten(q, k, v, rpb[l], batch, seq)
        x1, h2, comb = _out_proj(xf, yf, yb, bonf, bonb, g, yna, gate, lnx_g[l], lnx_b[l], b_gate[l],
                                 w_o_rwkv[l], w_o_na[l], w_out[l], norm2_g[l],
                                 w_router_group[l], b_router_group[l],
                                 w_router_expert[l], b_router_expert[l], _row_tile(n, 256))
        out = _moe(h2, x1, comb, w_gate_e[l], w_up_e[l], w_down_e[l], _row_tile(n, 1024))
        x = out.reshape(batch, seq, d)
    return x
```

```python
import functools

import numpy as np
import jax
import jax.numpy as jnp
from jax import lax
from jax.experimental import pallas as pl
from jax.experimental.pallas import tpu as pltpu

D_MODEL = 1024
HEAD_DIM = 64
RWKV_HEADS = 8
RWKV_WIDTH = RWKV_HEADS * HEAD_DIM
DECAY_RANK = 64
ICLR_RANK = 64
GATE_RANK = 128
NA_HEADS = 8
NA_WIDTH = NA_HEADS * HEAD_DIM
GRID_W = 64
NA_MAX_WIN_ROWS = 8
NA_WIN_COLS = 16
N_GROUPS = 4
EXPERTS_PER_GROUP = 8
N_EXPERTS = N_GROUPS * EXPERTS_PER_GROUP
D_EXPERT = 256
RMS_EPS = 1e-6
GN_EPS = 64e-5
NEG_INF = -1e30
RWKV_COLS = 3 * RWKV_WIDTH + 2 * DECAY_RANK + 2 * ICLR_RANK + GATE_RANK
NA_COLS = 3 * NA_WIDTH
GATE_COLS = 2 * D_MODEL
IN_COLS = RWKV_COLS + NA_COLS + GATE_COLS

LANES = 128
BF16_SUBLANES = 16
VMEM_LIMIT_BYTES = 56 * 1024 * 1024

CHUNK = 64
PAIR = LANES // HEAD_DIM
N_PAIRS = RWKV_HEADS // PAIR
EXP_M05 = float(np.exp(-0.5))
ROUTER_LANES = LANES

F32 = jnp.float32
BF16 = jnp.bfloat16


def _dot(a, b):
    return jnp.dot(a, b, preferred_element_type=F32)


def _dot_nt(a, b):
    return lax.dot_general(a, b, (((1,), (1,)), ((), ())), preferred_element_type=F32)


def _dot_tn(a, b):
    return lax.dot_general(a, b, (((0,), (0,)), ((), ())), preferred_element_type=F32)


def _split2(z):
    hi = z.astype(BF16)
    lo = (z - hi.astype(F32)).astype(BF16)
    return hi, lo


def _seg_dot(z, bd):
    hi, lo = _split2(z)
    return _dot(hi, bd) + _dot(lo, bd)


def _block_diag_heads(width, value):
    idx = np.arange(width) // HEAD_DIM
    return jnp.asarray((idx[:, None] == idx[None, :]).astype(np.float32) * value, dtype=BF16)


def _inproj_kernel(x_ref, g1_ref, w_ref, qg_ref, kg_ref, bd_ref,
                   prw_ref, q_ref, k_ref, v_ref, gate_ref):
    x = x_ref[...]
    ms = jnp.mean(x * x, axis=-1, keepdims=True)
    h = (x * lax.rsqrt(ms + RMS_EPS) * g1_ref[...]).astype(BF16)

    def proj(lo, hi):
        return _dot(h, w_ref[:, lo:hi])

    col = 0
    while col < RWKV_COLS:
        nxt = min(col + 512, RWKV_COLS)
        prw_ref[:, col:nxt] = proj(col, nxt).astype(BF16)
        col = nxt

    def head_rms(t, gain):
        msq = _dot((t * t).astype(BF16), bd_ref[...])
        return t * lax.rsqrt(msq + RMS_EPS) * gain

    c0 = RWKV_COLS
    q_ref[...] = head_rms(proj(c0, c0 + NA_WIDTH), qg_ref[...]).astype(BF16)
    k_ref[...] = head_rms(proj(c0 + NA_WIDTH, c0 + 2 * NA_WIDTH), kg_ref[...]).astype(BF16)
    v_ref[...] = proj(c0 + 2 * NA_WIDTH, c0 + 3 * NA_WIDTH).astype(BF16)
    c1 = c0 + NA_COLS
    for j in range(GATE_COLS // 512):
        gate_ref[:, j * 512:(j + 1) * 512] = proj(c1 + j * 512, c1 + (j + 1) * 512).astype(BF16)


def _in_proj(xf, g1, w_in_b, q_gain, k_gain, tm):
    n = xf.shape[0]
    qg = (jnp.tile(q_gain, NA_HEADS) * (HEAD_DIM ** -0.5)).reshape(1, NA_WIDTH)
    kg = jnp.tile(k_gain, NA_HEADS).reshape(1, NA_WIDTH)
    bd = _block_diag_heads(NA_WIDTH, 1.0 / HEAD_DIM)
    const = lambda i: (0, 0)
    row = lambda i: (i, 0)
    return pl.pallas_call(
        _inproj_kernel,
        grid=(n // tm,),
        in_specs=[
            pl.BlockSpec((tm, D_MODEL), row),
            pl.BlockSpec((1, D_MODEL), const),
            pl.BlockSpec((D_MODEL, IN_COLS), const),
            pl.BlockSpec((1, NA_WIDTH), const),
            pl.BlockSpec((1, NA_WIDTH), const),
            pl.BlockSpec((NA_WIDTH, NA_WIDTH), const),
        ],
        out_specs=[
            pl.BlockSpec((tm, RWKV_COLS), row),
            pl.BlockSpec((tm, NA_WIDTH), row),
            pl.BlockSpec((tm, NA_WIDTH), row),
            pl.BlockSpec((tm, NA_WIDTH), row),
            pl.BlockSpec((tm, GATE_COLS), row),
        ],
        out_shape=[
            jax.ShapeDtypeStruct((n, RWKV_COLS), BF16),
            jax.ShapeDtypeStruct((n, NA_WIDTH), BF16),
            jax.ShapeDtypeStruct((n, NA_WIDTH), BF16),
            jax.ShapeDtypeStruct((n, NA_WIDTH), BF16),
            jax.ShapeDtypeStruct((n, GATE_COLS), BF16),
        ],
        compiler_params=pltpu.CompilerParams(
            dimension_semantics=("parallel",), vmem_limit_bytes=VMEM_LIMIT_BYTES),
        name="in_proj",
    )(xf, g1.reshape(1, D_MODEL), w_in_b, qg, kg, bd)


def _expand(z, m0):
    return jnp.concatenate([jnp.where(m0, z, 0.0), jnp.where(m0, 0.0, z)], axis=0).astype(BF16)


def _level_mask(t_idx, s_idx, b, reverse):
    same = (t_idx // (2 * b)) == (s_idx // (2 * b))
    t_hi = (t_idx % (2 * b)) >= b
    s_hi = (s_idx % (2 * b)) >= b
    if reverse:
        return same & jnp.logical_not(t_hi) & s_hi
    return same & t_hi & jnp.logical_not(s_hi)


def _rwkv_prep(ps, d, prm, with_gate):
    reverse = d == 1
    w = RWKV_WIDTH
    r = ps[:, 0:w]
    k = ps[:, w:2 * w]
    v = ps[:, 2 * w:3 * w]
    c_dec = 3 * w
    c_icl = c_dec + 2 * DECAY_RANK
    c_gate = c_icl + 2 * ICLR_RANK
    bd1 = prm["bd1"][...]

    w_raw = prm["w0"][d:d + 1, :] + _dot(jnp.tanh(ps[:, c_dec:c_icl]).astype(BF16), prm["wup"][d])
    lw = -EXP_M05 * jax.nn.sigmoid(w_raw)
    a = jax.nn.sigmoid(prm["a0"][d:d + 1, :] + _dot(ps[:, c_icl:c_gate].astype(BF16), prm["aup"][d]))
    kk = k * prm["k_k"][...]
    kk = kk * lax.rsqrt(_seg_dot(kk * kk, bd1) + 1e-12)
    kd = k * (1.0 + (a - 1.0) * prm["k_a"][...])
    bonus = _seg_dot(r * kd * prm["r_k"][...], bd1) * v

    rowc = lax.broadcasted_iota(jnp.int32, (CHUNK, CHUNK), 0)
    colc = lax.broadcasted_iota(jnp.int32, (CHUNK, CHUNK), 1)
    tri = jnp.where((colc >= rowc) if reverse else (colc <= rowc), 1.0, 0.0).astype(BF16)
    lw_hi = lw.astype(BF16)
    rem = lw - lw_hi.astype(F32)
    lw_mid = rem.astype(BF16)
    lw_lo = (rem - lw_mid.astype(F32)).astype(BF16)
    cum = _dot(tri, lw_hi) + _dot(tri, lw_mid) + _dot(tri, lw_lo)
    tot = jnp.sum(lw, axis=0, keepdims=True)
    e_rest = jnp.exp(tot - cum)
    e_neg = jnp.exp(-cum)
    beta = kk * a
    out = dict(rt=r * jnp.exp(cum), at=-kk * jnp.exp(cum - lw), kh=kd * e_neg, bh=beta * e_neg,
               kw=kd * e_rest, bw=beta * e_rest, v=v, wc=jnp.exp(tot), bonus=bonus)
    if with_gate:
        out["gate"] = _dot(jax.nn.sigmoid(ps[:, c_gate:c_gate + GATE_RANK]).astype(BF16), prm["gup"][...])
    return out


def _rwkv_chains(preps, s_ref):
    lane = lax.broadcasted_iota(jnp.int32, (1, LANES), 1)
    m0 = lane < HEAD_DIM
    t_idx = lax.broadcasted_iota(jnp.int32, (CHUNK, LANES), 0)
    s_idx = lax.broadcasted_iota(jnp.int32, (CHUNK, LANES), 1) % CHUNK
    eye = jnp.where(s_idx == t_idx, 1.0, 0.0)
    strict = [s_idx < t_idx, s_idx > t_idx]
    incl = [s_idx <= t_idx, s_idx >= t_idx]
    bd_r = lax.broadcasted_iota(jnp.int32, (LANES, LANES), 0) // HEAD_DIM
    bd_c = lax.broadcasted_iota(jnp.int32, (LANES, LANES), 1) // HEAD_DIM
    bdmask = bd_r == bd_c

    chains = [(d, pr) for pr in range(N_PAIRS) for d in range(len(preps))]

    def part(c, name):
        d, pr = c
        return preps[d][name][:, pr * LANES:(pr + 1) * LANES]

    ab, ak = {}, {}
    for c in chains:
        x = jnp.concatenate([part(c, "at"), part(c, "rt")], axis=0).astype(BF16)
        ab[c] = _dot_nt(x, _expand(part(c, "bh"), m0))
        ak[c] = _dot_nt(x, _expand(part(c, "kh"), m0))
    nmat, nb, a_rb, av, tinv = {}, {}, {}, {}, {}
    for c in chains:
        d = c[0]
        nmat[c] = jnp.where(strict[d], ab[c][:CHUNK], 0.0)
        nb[c] = nmat[c].astype(BF16)
        a_rb[c] = jnp.where(incl[d], ab[c][CHUNK:], 0.0).astype(BF16)
        a_k = jnp.concatenate([jnp.where(strict[d], ak[c][:CHUNK], 0.0),
                               jnp.where(incl[d], ak[c][CHUNK:], 0.0)], axis=0).astype(BF16)
        av[c] = _dot(a_k, _expand(part(c, "v"), m0))
        tinv[c] = eye + jnp.where(_level_mask(t_idx, s_idx, 1, d == 1), nmat[c], 0.0)
    b = 2
    while b < CHUNK:
        e = {c: _dot(nb[c], _expand(tinv[c], m0)) for c in chains}
        for c in chains:
            f = _dot(tinv[c].astype(BF16), _expand(e[c], m0))
            tinv[c] = tinv[c] + jnp.where(_level_mask(t_idx, s_idx, b, c[0] == 1), f, 0.0)
        b *= 2
    tu = {c: _dot(tinv[c].astype(BF16),
                  jnp.concatenate([_expand(part(c, "at"), m0), _expand(av[c][:CHUNK], m0)], axis=1))
          for c in chains}
    qy = {c: _dot(a_rb[c], jnp.concatenate([_expand(tu[c][:, :LANES], m0), _expand(tu[c][:, LANES:], m0)],
                                           axis=1))
          for c in chains}
    ys = {}
    for c in chains:
        d, pr = c
        q = part(c, "rt") + qy[c][:, :LANES]
        yl = qy[c][:, LANES:] + av[c][CHUNK:]
        s0 = s_ref[d, pr]
        s0b = s0.astype(BF16)
        ys[c] = _dot_nt(q.astype(BF16), s0b) + yl
        tn = _dot_tn(tu[c].astype(BF16), part(c, "bw").astype(BF16))
        vk = _dot_tn(part(c, "v").astype(BF16), part(c, "kw").astype(BF16))
        mt = jnp.where(bdmask, tn[:LANES], 0.0)
        bt = jnp.where(bdmask, tn[LANES:] + vk, 0.0)
        s_ref[d, pr] = s0 * part(c, "wc") + _dot(s0b, mt.astype(BF16)) + bt
    return [jnp.concatenate([ys[(d, pr)] for pr in range(N_PAIRS)], axis=1) for d in range(len(preps))]


def _rwkv_kernel(pf_ref, pfp_ref, pfn_ref, pb_ref, pbp_ref, pbn_ref,
                 mup_ref, mun_ref, w0_ref, wup_ref, a0_ref, aup_ref, gup_ref,
                 kk_ref, ka_ref, rk_ref, bd1_ref,
                 yf_ref, bonf_ref, g_ref, yb_ref, bonb_ref,
                 buf_ref, s_ref, *, nc):
    i = pl.program_id(1)

    @pl.when(i == 0)
    def _():
        s_ref[...] = jnp.zeros_like(s_ref)

    prm = dict(w0=w0_ref, wup=wup_ref, a0=a0_ref, aup=aup_ref, gup=gup_ref,
               k_k=kk_ref, k_a=ka_ref, r_k=rk_ref, bd1=bd1_ref)
    pad = BF16_SUBLANES

    def shifted(d, main_ref, prev_ref, next_ref, c):
        has_prev = jnp.where(c > 0, 1.0, 0.0)
        has_next = jnp.where(c < nc - 1, 1.0, 0.0)
        buf_ref[d, 0:pad, :] = prev_ref[...].astype(F32) * has_prev
        buf_ref[d, pad:pad + CHUNK, :] = main_ref[...].astype(F32)
        buf_ref[d, pad + CHUNK:, :] = next_ref[...].astype(F32) * has_next
        p = buf_ref[d, pad:pad + CHUNK, :]
        p_prev = buf_ref[d, pad - 1:pad - 1 + CHUNK, :]
        p_next = buf_ref[d, pad + 1:pad + 1 + CHUNK, :]
        return p + mup_ref[...] * (p_prev - p) + mun_ref[...] * (p_next - p)

    preps = [_rwkv_prep(shifted(0, pf_ref, pfp_ref, pfn_ref, i), 0, prm, True),
             _rwkv_prep(shifted(1, pb_ref, pbp_ref, pbn_ref, nc - 1 - i), 1, prm, False)]
    y_f, y_b = _rwkv_chains(preps, s_ref)
    yf_ref[...] = y_f
    bonf_ref[...] = preps[0]["bonus"]
    g_ref[...] = preps[0]["gate"]
    yb_ref[...] = y_b
    bonb_ref[...] = preps[1]["bonus"]


def _rwkv(prw, batch, seq, mu_prev, mu_next, w0_f, w_up_f, w0_b, w_up_b, a0_f, a_up_f, a0_b, a_up_b,
          g_up, k_k, k_a, r_k):
    n = batch * seq
    nc = seq // CHUNK
    sub = CHUNK // BF16_SUBLANES
    nsub = n // BF16_SUBLANES
    w = RWKV_WIDTH
    zd = jnp.zeros((DECAY_RANK, w), F32)
    zi = jnp.zeros((ICLR_RANK, w), F32)
    wup = jnp.stack([jnp.concatenate([w_up_f, zd], 0), jnp.concatenate([zd, w_up_b], 0)]).astype(BF16)
    aup = jnp.stack([jnp.concatenate([a_up_f, zi], 0), jnp.concatenate([zi, a_up_b], 0)]).astype(BF16)
    w0 = jnp.stack([w0_f, w0_b])
    a0 = jnp.stack([a0_f, a0_b])
    bd1 = _block_diag_heads(w, 1.0)

    def fwd_c(b, i):
        return i

    def bwd_c(b, i):
        return nc - 1 - i

    def main(cf):
        return lambda b, i: (b * nc + cf(b, i), 0)

    def prev(cf):
        return lambda b, i: (jnp.maximum((b * nc + cf(b, i)) * sub - 1, 0), 0)

    def nxt(cf):
        return lambda b, i: (jnp.minimum((b * nc + cf(b, i)) * sub + sub, nsub - 1), 0)

    const2 = lambda b, i: (0, 0)
    const3 = lambda b, i: (0, 0, 0)
    pspec = lambda f: pl.BlockSpec((CHUNK, RWKV_COLS), f)
    nspec = lambda f: pl.BlockSpec((BF16_SUBLANES, RWKV_COLS), f)
    ospec = lambda f: pl.BlockSpec((CHUNK, w), f)
    vec = pl.BlockSpec((1, w), const2)
    out_sds = jax.ShapeDtypeStruct((n, w), F32)
    return pl.pallas_call(
        functools.partial(_rwkv_kernel, nc=nc),
        grid=(batch, nc),
        in_specs=[
            pspec(main(fwd_c)), nspec(prev(fwd_c)), nspec(nxt(fwd_c)),
            pspec(main(bwd_c)), nspec(prev(bwd_c)), nspec(nxt(bwd_c)),
            pl.BlockSpec((1, RWKV_COLS), const2), pl.BlockSpec((1, RWKV_COLS), const2),
            pl.BlockSpec((2, w), const2), pl.BlockSpec((2, 2 * DECAY_RANK, w), const3),
            pl.BlockSpec((2, w), const2), pl.BlockSpec((2, 2 * ICLR_RANK, w), const3),
            pl.BlockSpec((GATE_RANK, w), const2),
            vec, vec, vec,
            pl.BlockSpec((w, w), const2),
        ],
        out_specs=[ospec(main(fwd_c)), ospec(main(fwd_c)), ospec(main(fwd_c)),
                   ospec(main(bwd_c)), ospec(main(bwd_c))],
        out_shape=[out_sds] * 5,
        scratch_shapes=[
            pltpu.VMEM((2, CHUNK + 2 * BF16_SUBLANES, RWKV_COLS), F32),
            pltpu.VMEM((2, N_PAIRS, LANES, LANES), F32),
        ],
        compiler_params=pltpu.CompilerParams(
            dimension_semantics=("arbitrary", "arbitrary"), vmem_limit_bytes=VMEM_LIMIT_BYTES),
        name="rwkv7",
    )(prw, prw, prw, prw, prw, prw,
      mu_prev.reshape(1, RWKV_COLS), mu_next.reshape(1, RWKV_COLS), w0, wup, a0, aup,
      g_up.astype(BF16), k_k.reshape(1, w), k_a.reshape(1, w), r_k.reshape(1, w), bd1)


def _na_kernel(q_ref, k_ref, v_ref, bias_ref, o_ref, *, rows, kh):
    i = pl.program_id(1)
    start = jnp.clip(i - kh // 2, 0, rows - kh)
    off = pl.multiple_of(start * GRID_W, GRID_W)
    q = q_ref[...]
    kb = k_ref[pl.ds(off, kh * GRID_W), :]
    vb = v_ref[pl.ds(off, kh * GRID_W), :]
    lane = lax.broadcasted_iota(jnp.int32, (1, LANES), 1)
    m0 = lane < HEAD_DIM
    zero = jnp.zeros((), BF16)
    outs = []
    for pr in range(NA_HEADS // PAIR):
        sl = slice(pr * LANES, (pr + 1) * LANES)
        qp = q[:, sl]
        qs = jnp.concatenate([jnp.where(m0, qp, zero), jnp.where(m0, zero, qp)], axis=0)
        s = _dot_nt(qs, kb[:, sl]) + bias_ref[0, pr]
        m = jnp.max(s, axis=-1, keepdims=True)
        e = jnp.exp(s - m)
        l = jnp.sum(e, axis=-1, keepdims=True)
        pv = _dot(e.astype(BF16), vb[:, sl]) / l
        outs.append(jnp.where(m0, pv[:GRID_W], pv[GRID_W:]))
    o_ref[...] = jnp.concatenate(outs, axis=1).astype(BF16)


def _na_bias_table(rpb, kh):
    cols = np.arange(GRID_W)
    col_start = np.clip(cols - NA_WIN_COLS // 2, 0, GRID_W - NA_WIN_COLS)
    col_mask = (cols[None, :] >= col_start[:, None]) & (cols[None, :] < col_start[:, None] + NA_WIN_COLS)
    col_off = np.clip(cols[None, :] - cols[:, None] + NA_WIN_COLS - 1, 0, 2 * NA_WIN_COLS - 2)
    delta = np.arange(kh)
    row_off = np.arange(kh)[None, :] - delta[:, None] + NA_MAX_WIN_ROWS - 1
    t = rpb.astype(F32)[:, row_off]
    t = t[:, :, :, col_off]
    t = jnp.where(col_mask[None, None, None], t, NEG_INF)
    t = t.transpose(1, 0, 3, 2, 4).reshape(kh, NA_HEADS // PAIR, PAIR * GRID_W, kh * GRID_W)
    return t


def _natten(q, k, v, rpb, batch, seq):
    n = batch * seq
    rows = seq // GRID_W
    kh = min(NA_MAX_WIN_ROWS, rows)
    bias = _na_bias_table(rpb, kh)

    def delta(b, i):
        return i - jnp.clip(i - kh // 2, 0, rows - kh)

    return pl.pallas_call(
        functools.partial(_na_kernel, rows=rows, kh=kh),
        grid=(batch, rows),
        in_specs=[
            pl.BlockSpec((GRID_W, NA_WIDTH), lambda b, i: (b * rows + i, 0)),
            pl.BlockSpec((seq, NA_WIDTH), lambda b, i: (b, 0)),
            pl.BlockSpec((seq, NA_WIDTH), lambda b, i: (b, 0)),
            pl.BlockSpec((1, NA_HEADS // PAIR, PAIR * GRID_W, kh * GRID_W),
                         lambda b, i: (delta(b, i), 0, 0, 0)),
        ],
        out_specs=pl.BlockSpec((GRID_W, NA_WIDTH), lambda b, i: (b * rows + i, 0)),
        out_shape=jax.ShapeDtypeStruct((n, NA_WIDTH), BF16),
        compiler_params=pltpu.CompilerParams(
            dimension_semantics=("parallel", "arbitrary"), vmem_limit_bytes=VMEM_LIMIT_BYTES),
        name="natten2d",
    )(q, k, v, bias)


def _route(logits):
    lane = lax.broadcasted_iota(jnp.int32, (1, ROUTER_LANES), 1)
    big = jnp.int32(ROUTER_LANES)
    ninf = -jnp.inf
    gmask = lane < N_GROUPS
    gl = jnp.where(gmask, logits, ninf)
    gmax = jnp.max(gl, axis=-1, keepdims=True)
    g_sel = jnp.min(jnp.where(gl == gmax, lane, big), axis=-1, keepdims=True)
    p_group = 1.0 / jnp.sum(jnp.where(gmask, jnp.exp(logits - gmax), 0.0), axis=-1, keepdims=True)
    e_idx = lane - N_GROUPS
    emask = (e_idx >= 0) & (e_idx < N_EXPERTS) & ((e_idx // EXPERTS_PER_GROUP) == g_sel)
    ev = jnp.where(emask, logits, ninf)
    m1 = jnp.max(ev, axis=-1, keepdims=True)
    i1 = jnp.min(jnp.where(ev == m1, lane, big), axis=-1, keepdims=True)
    ev2 = jnp.where(lane == i1, ninf, ev)
    m2 = jnp.max(ev2, axis=-1, keepdims=True)
    i2 = jnp.min(jnp.where(ev2 == m2, lane, big), axis=-1, keepdims=True)
    t = jnp.exp(m2 - m1)
    w1 = p_group / (1.0 + t)
    w2 = p_group * t / (1.0 + t)
    return jnp.where(lane == i1, w1, 0.0) + jnp.where(lane == i2, w2, 0.0)


def _out_kernel(x_ref, yf_ref, yb_ref, bonf_ref, bonb_ref, g_ref, yna_ref, gate_ref,
                lng_ref, lnb_ref, bgate_ref, worw_ref, wona_ref, wout_ref, n2g_ref, wr_ref, br_ref, bd_ref,
                x1_ref, h2_ref, comb_ref):
    bd = bd_ref[...]
    y = yf_ref[...] + yb_ref[...]
    yc = y - _seg_dot(y, bd)
    var = _seg_dot(yc * yc, bd)
    yn = yc * lax.rsqrt(var + GN_EPS) * lng_ref[...] + lnb_ref[...]
    ya = (yn + bonf_ref[...] + bonb_ref[...]) * g_ref[...]
    ya_o = _dot(ya.astype(BF16), worw_ref[...])
    yb_o = _dot(yna_ref[...], wona_ref[...])
    gl = gate_ref[...].astype(F32) + bgate_ref[...]
    mix = jax.nn.sigmoid(gl[:, :D_MODEL]) * ya_o + jax.nn.sigmoid(gl[:, D_MODEL:]) * yb_o
    x1 = x_ref[...] + _dot(mix.astype(BF16), wout_ref[...])
    x1_ref[...] = x1
    ms = jnp.mean(x1 * x1, axis=-1, keepdims=True)
    h2 = x1 * lax.rsqrt(ms + RMS_EPS) * n2g_ref[...]
    h2_ref[...] = h2.astype(BF16)
    logits = jnp.dot(h2, wr_ref[...], precision=lax.Precision.HIGHEST,
                     preferred_element_type=F32) + br_ref[...]
    comb_ref[...] = _route(logits)


def _out_proj(xf, yf, yb, bonf, bonb, g, yna, gate, lnx_g, lnx_b, b_gate, w_o_rwkv, w_o_na, w_out,
              norm2_g, w_rg, b_rg, w_re, b_re, tm):
    n = xf.shape[0]
    w = RWKV_WIDTH
    padc = ROUTER_LANES - N_GROUPS - N_EXPERTS
    wr = jnp.concatenate([w_rg, w_re, jnp.zeros((D_MODEL, padc), F32)], axis=1)
    br = jnp.concatenate([b_rg, b_re, jnp.zeros((padc,), F32)]).reshape(1, ROUTER_LANES)
    bd = _block_diag_heads(w, 1.0 / HEAD_DIM)
    const = lambda i: (0, 0)
    row = lambda i: (i, 0)
    rs = lambda c: pl.BlockSpec((tm, c), row)
    cs = lambda r, c: pl.BlockSpec((r, c), const)
    return pl.pallas_call(
        _out_kernel,
        grid=(n // tm,),
        in_specs=[rs(D_MODEL), rs(w), rs(w), rs(w), rs(w), rs(w), rs(NA_WIDTH), rs(GATE_COLS),
                  cs(1, w), cs(1, w), cs(1, GATE_COLS), cs(w, D_MODEL), cs(NA_WIDTH, D_MODEL),
                  cs(D_MODEL, D_MODEL), cs(1, D_MODEL), cs(D_MODEL, ROUTER_LANES), cs(1, ROUTER_LANES),
                  cs(w, w)],
        out_specs=[rs(D_MODEL), rs(D_MODEL), rs(ROUTER_LANES)],
        out_shape=[jax.ShapeDtypeStruct((n, D_MODEL), F32),
                   jax.ShapeDtypeStruct((n, D_MODEL), BF16),
                   jax.ShapeDtypeStruct((n, ROUTER_LANES), F32)],
        compiler_params=pltpu.CompilerParams(
            dimension_semantics=("parallel",), vmem_limit_bytes=VMEM_LIMIT_BYTES),
        name="out_proj",
    )(xf, yf, yb, bonf, bonb, g, yna, gate,
      lnx_g.reshape(1, w), lnx_b.reshape(1, w), b_gate.reshape(1, GATE_COLS),
      w_o_rwkv.astype(BF16), w_o_na.astype(BF16), w_out.astype(BF16),
      norm2_g.reshape(1, D_MODEL), wr, br, bd)


def _moe_kernel(h_ref, x1_ref, comb_ref, wgu_ref, wd_ref, o_ref, acc_ref):
    e = pl.program_id(1)

    @pl.when(e == 0)
    def _():
        acc_ref[...] = jnp.zeros_like(acc_ref)

    gu = _dot(h_ref[...], wgu_ref[0])
    gate, up = gu[:, :D_EXPERT], gu[:, D_EXPERT:]
    lane = lax.broadcasted_iota(jnp.int32, (1, ROUTER_LANES), 1)
    c = jnp.sum(jnp.where(lane == e + N_GROUPS, comb_ref[...], 0.0), axis=-1, keepdims=True)
    he = gate * jax.nn.sigmoid(gate) * up * c
    acc_ref[...] += _dot(he.astype(BF16), wd_ref[0])

    @pl.when(e == N_EXPERTS - 1)
    def _():
        o_ref[...] = x1_ref[...] + acc_ref[...]


def _moe(h2, x1, comb, w_gate_e, w_up_e, w_down_e, tm):
    n = h2.shape[0]
    wgu = jnp.concatenate([w_gate_e, w_up_e], axis=-1).astype(BF16)
    wd = w_down_e.astype(BF16)
    row = lambda i, e: (i, 0)
    return pl.pallas_call(
        _moe_kernel,
        grid=(n // tm, N_EXPERTS),
        in_specs=[
            pl.BlockSpec((tm, D_MODEL), row),
            pl.BlockSpec((tm, D_MODEL), row),
            pl.BlockSpec((tm, ROUTER_LANES), row),
            pl.BlockSpec((1, D_MODEL, 2 * D_EXPERT), lambda i, e: (e, 0, 0)),
            pl.BlockSpec((1, D_EXPERT, D_MODEL), lambda i, e: (e, 0, 0)),
        ],
        out_specs=pl.BlockSpec((tm, D_MODEL), row),
        out_shape=jax.ShapeDtypeStruct((n, D_MODEL), F32),
        scratch_shapes=[pltpu.VMEM((tm, D_MODEL), F32)],
        compiler_params=pltpu.CompilerParams(
            dimension_semantics=("parallel", "arbitrary"), vmem_limit_bytes=VMEM_LIMIT_BYTES),
        name="moe",
    )(h2, x1, comb, wgu, wd)


def _row_tile(n, want):
    t = min(want, n)
    while n % t:
        t //= 2
    return t


def kernel(x, norm1_g, w_in, mu_prev, mu_next, w0_f, w_up_f, w0_b, w_up_b, a0_f, a_up_f, a0_b, a_up_b, g_up, k_k, k_a, r_k, lnx_g, lnx_b, q_gain, k_gain, rpb, b_gate, w_o_rwkv, w_o_na, w_out, norm2_g, w_router_group, b_router_group, w_router_expert, b_router_expert, w_gate_e, w_up_e, w_down_e):
    batch, seq, d = x.shape
    assert d == D_MODEL and seq % CHUNK == 0 and seq % GRID_W == 0
    n = batch * seq
    for l in range(norm1_g.shape[0]):
        xf = x.reshape(n, d)
        prw, q, k, v, gate = _in_proj(xf, norm1_g[l], w_in[l].astype(BF16), q_gain[l], k_gain[l],
                                      _row_tile(n, 512))
        yf, bonf, g, yb, bonb = _rwkv(prw, batch, seq, mu_prev[l], mu_next[l], w0_f[l], w_up_f[l],
                                      w0_b[l], w_up_b[l], a0_f[l], a_up_f[l], a0_b[l], a_up_b[l],
                                      g_up[l], k_k[l], k_a[l], r_k[l].reshape(-1))
        yna = _natten(q, k, v, rpb[l], batch, seq)
        x1, h2, comb = _out_proj(xf, yf, yb, bonf, bonb, g, yna, gate, lnx_g[l], lnx_b[l], b_gate[l],
                                 w_o_rwkv[l], w_o_na[l], w_out[l], norm2_g[l],
                                 w_router_group[l], b_router_group[l],
                                 w_router_expert[l], b_router_expert[l], _row_tile(n, 256))
        out = _moe(h2, x1, comb, w_gate_e[l], w_up_e[l], w_down_e[l], _row_tile(n, 1024))
        x = out.reshape(batch, seq, d)
    return x
```

```python
import functools

import numpy as np
import jax
import jax.numpy as jnp
from jax import lax
from jax.experimental import pallas as pl
from jax.experimental.pallas import tpu as pltpu

D_MODEL = 1024
HEAD_DIM = 64
RWKV_HEADS = 8
RWKV_WIDTH = RWKV_HEADS * HEAD_DIM
DECAY_RANK = 64
ICLR_RANK = 64
GATE_RANK = 128
NA_HEADS = 8
NA_WIDTH = NA_HEADS * HEAD_DIM
GRID_W = 64
NA_MAX_WIN_ROWS = 8
NA_WIN_COLS = 16
N_GROUPS = 4
EXPERTS_PER_GROUP = 8
N_EXPERTS = N_GROUPS * EXPERTS_PER_GROUP
D_EXPERT = 256
RMS_EPS = 1e-6
GN_EPS = 64e-5
NEG_INF = -1e30
RWKV_COLS = 3 * RWKV_WIDTH + 2 * DECAY_RANK + 2 * ICLR_RANK + GATE_RANK
NA_COLS = 3 * NA_WIDTH
GATE_COLS = 2 * D_MODEL
IN_COLS = RWKV_COLS + NA_COLS + GATE_COLS

LANES = 128
BF16_SUBLANES = 16
VMEM_LIMIT_BYTES = 56 * 1024 * 1024

CHUNK = 64
RWKV_CHUNKS_PER_STEP = 4
PAIR = LANES // HEAD_DIM
N_PAIRS = RWKV_HEADS // PAIR
EXP_M05 = float(np.exp(-0.5))
NA_ROWS_PER_STEP = 4
SUBLANES = 8
ROUTER_ROWS = SUBLANES + N_EXPERTS
ROUTE_REC = SUBLANES
MOE_TM = 512
MOE_RB = 512
SEG = BF16_SUBLANES

F32 = jnp.float32
BF16 = jnp.bfloat16


def _dot(a, b):
    return jnp.dot(a, b, preferred_element_type=F32)


def _dot_nt(a, b):
    return lax.dot_general(a, b, (((1,), (1,)), ((), ())), preferred_element_type=F32)


def _dot_tn(a, b):
    return lax.dot_general(a, b, (((0,), (0,)), ((), ())), preferred_element_type=F32)


def _split2(z):
    hi = z.astype(BF16)
    lo = (z - hi.astype(F32)).astype(BF16)
    return hi, lo


def _seg_dot(z, bd):
    hi, lo = _split2(z)
    return _dot(hi, bd) + _dot(lo, bd)


def _block_diag_heads(width, value):
    idx = np.arange(width) // HEAD_DIM
    return jnp.asarray((idx[:, None] == idx[None, :]).astype(np.float32) * value, dtype=BF16)


def _inproj_kernel(x_ref, g1_ref, w_ref, qg_ref, kg_ref, bd_ref,
                   prw_ref, q_ref, k_ref, v_ref, gate_ref):
    x = x_ref[...]
    ms = jnp.mean(x * x, axis=-1, keepdims=True)
    h = (x * lax.rsqrt(ms + RMS_EPS) * g1_ref[...]).astype(BF16)

    def proj(lo, hi):
        return _dot(h, w_ref[:, lo:hi])

    col = 0
    while col < RWKV_COLS:
        nxt = min(col + 512, RWKV_COLS)
        prw_ref[:, col:nxt] = proj(col, nxt).astype(BF16)
        col = nxt

    def head_rms(t, gain):
        msq = _dot((t * t).astype(BF16), bd_ref[...])
        return t * lax.rsqrt(msq + RMS_EPS) * gain

    c0 = RWKV_COLS
    q_ref[...] = head_rms(proj(c0, c0 + NA_WIDTH), qg_ref[...]).astype(BF16)
    k_ref[...] = head_rms(proj(c0 + NA_WIDTH, c0 + 2 * NA_WIDTH), kg_ref[...]).astype(BF16)
    v_ref[...] = proj(c0 + 2 * NA_WIDTH, c0 + 3 * NA_WIDTH).astype(BF16)
    c1 = c0 + NA_COLS
    for j in range(GATE_COLS // 512):
        gate_ref[:, j * 512:(j + 1) * 512] = proj(c1 + j * 512, c1 + (j + 1) * 512).astype(BF16)


def _in_proj(xf, g1, w_in_b, q_gain, k_gain, tm):
    n = xf.shape[0]
    qg = (jnp.tile(q_gain, NA_HEADS) * (HEAD_DIM ** -0.5)).reshape(1, NA_WIDTH)
    kg = jnp.tile(k_gain, NA_HEADS).reshape(1, NA_WIDTH)
    bd = _block_diag_heads(NA_WIDTH, 1.0 / HEAD_DIM)
    const = lambda i: (0, 0)
    row = lambda i: (i, 0)
    return pl.pallas_call(
        _inproj_kernel,
        grid=(n // tm,),
        in_specs=[
            pl.BlockSpec((tm, D_MODEL), row),
            pl.BlockSpec((1, D_MODEL), const),
            pl.BlockSpec((D_MODEL, IN_COLS), const),
            pl.BlockSpec((1, NA_WIDTH), const),
            pl.BlockSpec((1, NA_WIDTH), const),
            pl.BlockSpec((NA_WIDTH, NA_WIDTH), const),
        ],
        out_specs=[
            pl.BlockSpec((tm, RWKV_COLS), row),
            pl.BlockSpec((tm, NA_WIDTH), row),
            pl.BlockSpec((tm, NA_WIDTH), row),
            pl.BlockSpec((tm, NA_WIDTH), row),
            pl.BlockSpec((tm, GATE_COLS), row),
        ],
        out_shape=[
            jax.ShapeDtypeStruct((n, RWKV_COLS), BF16),
            jax.ShapeDtypeStruct((n, NA_WIDTH), BF16),
            jax.ShapeDtypeStruct((n, NA_WIDTH), BF16),
            jax.ShapeDtypeStruct((n, NA_WIDTH), BF16),
            jax.ShapeDtypeStruct((n, GATE_COLS), BF16),
        ],
        compiler_params=pltpu.CompilerParams(
            dimension_semantics=("parallel",), vmem_limit_bytes=VMEM_LIMIT_BYTES),
        name="in_proj",
    )(xf, g1.reshape(1, D_MODEL), w_in_b, qg, kg, bd)


def _expand(z, m0):
    return jnp.concatenate([jnp.where(m0, z, 0.0), jnp.where(m0, 0.0, z)], axis=0).astype(BF16)


def _level_mask(t_idx, s_idx, b, reverse):
    same = (t_idx // (2 * b)) == (s_idx // (2 * b))
    t_hi = (t_idx % (2 * b)) >= b
    s_hi = (s_idx % (2 * b)) >= b
    if reverse:
        return same & jnp.logical_not(t_hi) & s_hi
    return same & t_hi & jnp.logical_not(s_hi)


def _rwkv_prep(ps, d, prm, with_gate):
    reverse = d == 1
    w = RWKV_WIDTH
    r = ps[:, 0:w]
    k = ps[:, w:2 * w]
    v = ps[:, 2 * w:3 * w]
    c_dec = 3 * w
    c_icl = c_dec + 2 * DECAY_RANK
    c_gate = c_icl + 2 * ICLR_RANK
    bd1 = prm["bd1"][...]

    w_raw = prm["w0"][d:d + 1, :] + _dot(jnp.tanh(ps[:, c_dec:c_icl]).astype(BF16), prm["wup"][d])
    lw = -EXP_M05 * jax.nn.sigmoid(w_raw)
    a = jax.nn.sigmoid(prm["a0"][d:d + 1, :] + _dot(ps[:, c_icl:c_gate].astype(BF16), prm["aup"][d]))
    kk = k * prm["k_k"][...]
    kk = kk * lax.rsqrt(_seg_dot(kk * kk, bd1) + 1e-12)
    kd = k * (1.0 + (a - 1.0) * prm["k_a"][...])
    bonus = _seg_dot(r * kd * prm["r_k"][...], bd1) * v

    nrow = ps.shape[0]
    rowc = lax.broadcasted_iota(jnp.int32, (nrow, nrow), 0)
    colc = lax.broadcasted_iota(jnp.int32, (nrow, nrow), 1)
    in_chunk = (rowc // CHUNK) == (colc // CHUNK)
    tri = jnp.where(in_chunk & ((colc >= rowc) if reverse else (colc <= rowc)), 1.0, 0.0).astype(BF16)
    lw_hi = lw.astype(BF16)
    rem = lw - lw_hi.astype(F32)
    lw_mid = rem.astype(BF16)
    lw_lo = (rem - lw_mid.astype(F32)).astype(BF16)
    cum = _dot(tri, lw_hi) + _dot(tri, lw_mid) + _dot(tri, lw_lo)
    tots = [jnp.sum(lw[c * CHUNK:(c + 1) * CHUNK], axis=0, keepdims=True) for c in range(nrow // CHUNK)]
    tot = jnp.concatenate([jnp.broadcast_to(t, (CHUNK, w)) for t in tots], axis=0)
    e_rest = jnp.exp(tot - cum)
    e_neg = jnp.exp(-cum)
    beta = kk * a
    out = dict(rt=r * jnp.exp(cum), at=-kk * jnp.exp(cum - lw), kh=kd * e_neg, bh=beta * e_neg,
               kw=kd * e_rest, bw=beta * e_rest, v=v, wc=[jnp.exp(t) for t in tots], bonus=bonus)
    if with_gate:
        out["gate"] = _dot(jax.nn.sigmoid(ps[:, c_gate:c_gate + GATE_RANK]).astype(BF16), prm["gup"][...])
    return out


def _rwkv_chains(preps, s_ref):
    lane = lax.broadcasted_iota(jnp.int32, (1, LANES), 1)
    m0 = lane < HEAD_DIM
    t_idx = lax.broadcasted_iota(jnp.int32, (CHUNK, LANES), 0)
    s_idx = lax.broadcasted_iota(jnp.int32, (CHUNK, LANES), 1) % CHUNK
    eye = jnp.where(s_idx == t_idx, 1.0, 0.0)
    strict = [s_idx < t_idx, s_idx > t_idx]
    incl = [s_idx <= t_idx, s_idx >= t_idx]
    bd_r = lax.broadcasted_iota(jnp.int32, (LANES, LANES), 0) // HEAD_DIM
    bd_c = lax.broadcasted_iota(jnp.int32, (LANES, LANES), 1) // HEAD_DIM
    bdmask = bd_r == bd_c

    ndir = len(preps)
    nck = preps[0]["v"].shape[0] // CHUNK
    chains = [(d, ck, pr) for ck in range(nck) for pr in range(N_PAIRS) for d in range(ndir)]

    def part(c, name):
        d, ck, pr = c
        return preps[d][name][ck * CHUNK:(ck + 1) * CHUNK, pr * LANES:(pr + 1) * LANES]

    ab, ak = {}, {}
    for c in chains:
        x = jnp.concatenate([part(c, "at"), part(c, "rt")], axis=0).astype(BF16)
        ab[c] = _dot_nt(x, _expand(part(c, "bh"), m0))
        ak[c] = _dot_nt(x, _expand(part(c, "kh"), m0))
    nmat, nb, a_rb, av, tinv = {}, {}, {}, {}, {}
    for c in chains:
        d = c[0]
        nmat[c] = jnp.where(strict[d], ab[c][:CHUNK], 0.0)
        nb[c] = nmat[c].astype(BF16)
        a_rb[c] = jnp.where(incl[d], ab[c][CHUNK:], 0.0).astype(BF16)
        a_k = jnp.concatenate([jnp.where(strict[d], ak[c][:CHUNK], 0.0),
                               jnp.where(incl[d], ak[c][CHUNK:], 0.0)], axis=0).astype(BF16)
        av[c] = _dot(a_k, _expand(part(c, "v"), m0))
        tinv[c] = eye + jnp.where(_level_mask(t_idx, s_idx, 1, d == 1), nmat[c], 0.0)
    b = 2
    while b < CHUNK:
        e = {c: _dot(nb[c], _expand(tinv[c], m0)) for c in chains}
        for c in chains:
            f = _dot(tinv[c].astype(BF16), _expand(e[c], m0))
            tinv[c] = tinv[c] + jnp.where(_level_mask(t_idx, s_idx, b, c[0] == 1), f, 0.0)
        b *= 2
    tu = {c: _dot(tinv[c].astype(BF16),
                  jnp.concatenate([_expand(part(c, "at"), m0), _expand(av[c][:CHUNK], m0)], axis=1))
          for c in chains}
    qy = {c: _dot(a_rb[c], jnp.concatenate([_expand(tu[c][:, :LANES], m0), _expand(tu[c][:, LANES:], m0)],
                                           axis=1))
          for c in chains}
    mt, bt = {}, {}
    for c in chains:
        tn = _dot_tn(tu[c].astype(BF16), part(c, "bw").astype(BF16))
        vk = _dot_tn(part(c, "v").astype(BF16), part(c, "kw").astype(BF16))
        mt[c] = jnp.where(bdmask, tn[:LANES], 0.0).astype(BF16)
        bt[c] = jnp.where(bdmask, tn[LANES:] + vk, 0.0)
    state = {(d, pr): s_ref[d, pr] for d in range(ndir) for pr in range(N_PAIRS)}
    ys = {}
    for step in range(nck):
        for pr in range(N_PAIRS):
            for d in range(ndir):
                c = (d, step if d == 0 else nck - 1 - step, pr)
                s0 = state[(d, pr)]
                s0b = s0.astype(BF16)
                q = part(c, "rt") + qy[c][:, :LANES]
                ys[c] = _dot_nt(q.astype(BF16), s0b) + qy[c][:, LANES:] + av[c][CHUNK:]
                wc = preps[d]["wc"][c[1]][:, pr * LANES:(pr + 1) * LANES]
                state[(d, pr)] = s0 * wc + _dot(s0b, mt[c]) + bt[c]
    for (d, pr), s_new in state.items():
        s_ref[d, pr] = s_new
    return [jnp.concatenate([jnp.concatenate([ys[(d, ck, pr)] for pr in range(N_PAIRS)], axis=1)
                             for ck in range(nck)], axis=0) for d in range(ndir)]


def _rwkv_kernel(pf_ref, pfp_ref, pfn_ref, pb_ref, pbp_ref, pbn_ref,
                 mup_ref, mun_ref, w0_ref, wup_ref, a0_ref, aup_ref, gup_ref,
                 kk_ref, ka_ref, rk_ref, bd1_ref,
                 yf_ref, bonf_ref, g_ref, yb_ref, bonb_ref,
                 buf_ref, s_ref, *, nc):
    i = pl.program_id(1)

    @pl.when(i == 0)
    def _():
        s_ref[...] = jnp.zeros_like(s_ref)

    prm = dict(w0=w0_ref, wup=wup_ref, a0=a0_ref, aup=aup_ref, gup=gup_ref,
               k_k=kk_ref, k_a=ka_ref, r_k=rk_ref, bd1=bd1_ref)
    pad = BF16_SUBLANES
    nrow = pf_ref.shape[0]

    def shifted(d, main_ref, prev_ref, next_ref, c):
        has_prev = jnp.where(c > 0, 1.0, 0.0)
        has_next = jnp.where(c < nc - 1, 1.0, 0.0)
        buf_ref[d, 0:pad, :] = prev_ref[...].astype(F32) * has_prev
        buf_ref[d, pad:pad + nrow, :] = main_ref[...].astype(F32)
        buf_ref[d, pad + nrow:, :] = next_ref[...].astype(F32) * has_next
        p = buf_ref[d, pad:pad + nrow, :]
        p_prev = buf_ref[d, pad - 1:pad - 1 + nrow, :]
        p_next = buf_ref[d, pad + 1:pad + 1 + nrow, :]
        return p + mup_ref[...] * (p_prev - p) + mun_ref[...] * (p_next - p)

    preps = [_rwkv_prep(shifted(0, pf_ref, pfp_ref, pfn_ref, i), 0, prm, True),
             _rwkv_prep(shifted(1, pb_ref, pbp_ref, pbn_ref, nc - 1 - i), 1, prm, False)]
    y_f, y_b = _rwkv_chains(preps, s_ref)
    yf_ref[...] = y_f
    bonf_ref[...] = preps[0]["bonus"]
    g_ref[...] = preps[0]["gate"]
    yb_ref[...] = y_b
    bonb_ref[...] = preps[1]["bonus"]


def _rwkv(prw, batch, seq, mu_prev, mu_next, w0_f, w_up_f, w0_b, w_up_b, a0_f, a_up_f, a0_b, a_up_b,
          g_up, k_k, k_a, r_k):
    n = batch * seq
    cps = RWKV_CHUNKS_PER_STEP if seq % (CHUNK * RWKV_CHUNKS_PER_STEP) == 0 else 1
    nrow = cps * CHUNK
    nc = seq // nrow
    sub = nrow // BF16_SUBLANES
    nsub = n // BF16_SUBLANES
    w = RWKV_WIDTH
    zd = jnp.zeros((DECAY_RANK, w), F32)
    zi = jnp.zeros((ICLR_RANK, w), F32)
    wup = jnp.stack([jnp.concatenate([w_up_f, zd], 0), jnp.concatenate([zd, w_up_b], 0)]).astype(BF16)
    aup = jnp.stack([jnp.concatenate([a_up_f, zi], 0), jnp.concatenate([zi, a_up_b], 0)]).astype(BF16)
    w0 = jnp.stack([w0_f, w0_b])
    a0 = jnp.stack([a0_f, a0_b])
    bd1 = _block_diag_heads(w, 1.0)

    def fwd_c(b, i):
        return i

    def bwd_c(b, i):
        return nc - 1 - i

    def main(cf):
        return lambda b, i: (b * nc + cf(b, i), 0)

    def prev(cf):
        return lambda b, i: (jnp.maximum((b * nc + cf(b, i)) * sub - 1, 0), 0)

    def nxt(cf):
        return lambda b, i: (jnp.minimum((b * nc + cf(b, i)) * sub + sub, nsub - 1), 0)

    const2 = lambda b, i: (0, 0)
    const3 = lambda b, i: (0, 0, 0)
    pspec = lambda f: pl.BlockSpec((nrow, RWKV_COLS), f)
    nspec = lambda f: pl.BlockSpec((BF16_SUBLANES, RWKV_COLS), f)
    ospec = lambda f: pl.BlockSpec((nrow, w), f)
    vec = pl.BlockSpec((1, w), const2)
    out_sds = jax.ShapeDtypeStruct((n, w), F32)
    return pl.pallas_call(
        functools.partial(_rwkv_kernel, nc=nc),
        grid=(batch, nc),
        in_specs=[
            pspec(main(fwd_c)), nspec(prev(fwd_c)), nspec(nxt(fwd_c)),
            pspec(main(bwd_c)), nspec(prev(bwd_c)), nspec(nxt(bwd_c)),
            pl.BlockSpec((1, RWKV_COLS), const2), pl.BlockSpec((1, RWKV_COLS), const2),
            pl.BlockSpec((2, w), const2), pl.BlockSpec((2, 2 * DECAY_RANK, w), const3),
            pl.BlockSpec((2, w), const2), pl.BlockSpec((2, 2 * ICLR_RANK, w), const3),
            pl.BlockSpec((GATE_RANK, w), const2),
            vec, vec, vec,
            pl.BlockSpec((w, w), const2),
        ],
        out_specs=[ospec(main(fwd_c)), ospec(main(fwd_c)), ospec(main(fwd_c)),
                   ospec(main(bwd_c)), ospec(main(bwd_c))],
        out_shape=[out_sds] * 5,
        scratch_shapes=[
            pltpu.VMEM((2, nrow + 2 * BF16_SUBLANES, RWKV_COLS), F32),
            pltpu.VMEM((2, N_PAIRS, LANES, LANES), F32),
        ],
        compiler_params=pltpu.CompilerParams(
            dimension_semantics=("arbitrary", "arbitrary"), vmem_limit_bytes=VMEM_LIMIT_BYTES),
        name="rwkv7",
    )(prw, prw, prw, prw, prw, prw,
      mu_prev.reshape(1, RWKV_COLS), mu_next.reshape(1, RWKV_COLS), w0, wup, a0, aup,
      g_up.astype(BF16), k_k.reshape(1, w), k_a.reshape(1, w), r_k.reshape(1, w), bd1)


def _na_kernel(q_ref, k_ref, v_ref, bias_ref, o_ref, *, rows, kh, rblk):
    j = pl.program_id(1)
    lane = lax.broadcasted_iota(jnp.int32, (1, LANES), 1)
    m0 = lane < HEAD_DIM
    zero = jnp.zeros((), BF16)
    npairs = NA_HEADS // PAIR
    chains = [(r, pr) for r in range(rblk) for pr in range(npairs)]
    starts, deltas = [], []
    for r in range(rblk):
        i = j * rblk + r
        start = jnp.clip(i - kh // 2, 0, rows - kh)
        starts.append(pl.multiple_of(start * GRID_W, GRID_W))
        deltas.append(i - start)
    s = {}
    for (r, pr) in chains:
        sl = slice(pr * LANES, (pr + 1) * LANES)
        qp = q_ref[r * GRID_W:(r + 1) * GRID_W, sl]
        qs = jnp.concatenate([jnp.where(m0, qp, zero), jnp.where(m0, zero, qp)], axis=0)
        kb = k_ref[pl.ds(starts[r], kh * GRID_W), sl]
        s[(r, pr)] = _dot_nt(qs, kb) + bias_ref[deltas[r], pr]
    e, l = {}, {}
    for c in chains:
        m = jnp.max(s[c], axis=-1, keepdims=True)
        p = jnp.exp(s[c] - m)
        l[c] = jnp.sum(p, axis=-1, keepdims=True)
        e[c] = p.astype(BF16)
    for r in range(rblk):
        outs = []
        for pr in range(npairs):
            sl = slice(pr * LANES, (pr + 1) * LANES)
            vb = v_ref[pl.ds(starts[r], kh * GRID_W), sl]
            pv = _dot(e[(r, pr)], vb) / l[(r, pr)]
            outs.append(jnp.where(m0, pv[:GRID_W], pv[GRID_W:]))
        o_ref[r * GRID_W:(r + 1) * GRID_W, :] = jnp.concatenate(outs, axis=1).astype(BF16)


def _na_bias_table(rpb, kh):
    cols = np.arange(GRID_W)
    col_start = np.clip(cols - NA_WIN_COLS // 2, 0, GRID_W - NA_WIN_COLS)
    col_mask = (cols[None, :] >= col_start[:, None]) & (cols[None, :] < col_start[:, None] + NA_WIN_COLS)
    col_off = np.clip(cols[None, :] - cols[:, None] + NA_WIN_COLS - 1, 0, 2 * NA_WIN_COLS - 2)
    delta = np.arange(kh)
    row_off = np.arange(kh)[None, :] - delta[:, None] + NA_MAX_WIN_ROWS - 1
    t = rpb.astype(F32)[:, row_off]
    t = t[:, :, :, col_off]
    t = jnp.where(col_mask[None, None, None], t, NEG_INF)
    t = t.transpose(1, 0, 3, 2, 4).reshape(kh, NA_HEADS // PAIR, PAIR * GRID_W, kh * GRID_W)
    return t


def _natten(q, k, v, rpb, batch, seq):
    n = batch * seq
    rows = seq // GRID_W
    kh = min(NA_MAX_WIN_ROWS, rows)
    bias = _na_bias_table(rpb, kh)
    rblk = NA_ROWS_PER_STEP if rows % NA_ROWS_PER_STEP == 0 else 1
    nblk = rows // rblk

    return pl.pallas_call(
        functools.partial(_na_kernel, rows=rows, kh=kh, rblk=rblk),
        grid=(batch, nblk),
        in_specs=[
            pl.BlockSpec((rblk * GRID_W, NA_WIDTH), lambda b, i: (b * nblk + i, 0)),
            pl.BlockSpec((seq, NA_WIDTH), lambda b, i: (b, 0)),
            pl.BlockSpec((seq, NA_WIDTH), lambda b, i: (b, 0)),
            pl.BlockSpec((kh, NA_HEADS // PAIR, PAIR * GRID_W, kh * GRID_W), lambda b, i: (0, 0, 0, 0),
                         pipeline_mode=pl.Buffered(1)),
        ],
        out_specs=pl.BlockSpec((rblk * GRID_W, NA_WIDTH), lambda b, i: (b * nblk + i, 0)),
        out_shape=jax.ShapeDtypeStruct((n, NA_WIDTH), BF16),
        compiler_params=pltpu.CompilerParams(
            dimension_semantics=("parallel", "arbitrary"), vmem_limit_bytes=VMEM_LIMIT_BYTES),
        name="natten2d",
    )(q, k, v, bias)


def _route_t(logits_t):
    tm = logits_t.shape[1]
    sub = lax.broadcasted_iota(jnp.int32, (SUBLANES, tm), 0)
    ninf = -jnp.inf
    gvalid = sub < N_GROUPS
    gl = jnp.where(gvalid, logits_t[0:SUBLANES], ninf)
    gmax = jnp.max(gl, axis=0, keepdims=True)
    g_sel = jnp.min(jnp.where(gl == gmax, sub, SUBLANES), axis=0, keepdims=True)
    p_group = 1.0 / jnp.sum(jnp.where(gvalid, jnp.exp(gl - gmax), 0.0), axis=0, keepdims=True)
    ev = logits_t[SUBLANES:2 * SUBLANES]
    for g in range(1, N_GROUPS):
        lo = SUBLANES + g * EXPERTS_PER_GROUP
        ev = jnp.where(g_sel == g, logits_t[lo:lo + EXPERTS_PER_GROUP], ev)
    m1 = jnp.max(ev, axis=0, keepdims=True)
    i1 = jnp.min(jnp.where(ev == m1, sub, SUBLANES), axis=0, keepdims=True)
    ev2 = jnp.where(sub == i1, ninf, ev)
    m2 = jnp.max(ev2, axis=0, keepdims=True)
    i2 = jnp.min(jnp.where(ev2 == m2, sub, SUBLANES), axis=0, keepdims=True)
    t = jnp.exp(m2 - m1)
    w1 = p_group / (1.0 + t)
    w2 = p_group * t / (1.0 + t)
    e1 = g_sel * EXPERTS_PER_GROUP + i1
    e2 = g_sel * EXPERTS_PER_GROUP + i2
    rec = jnp.where(sub == 0, e1.astype(F32),
                    jnp.where(sub == 1, e2.astype(F32),
                              jnp.where(sub == 2, w1, jnp.where(sub == 3, w2, 0.0))))
    return rec, e1, e2


def _out_kernel(x_ref, yf_ref, yb_ref, bonf_ref, bonb_ref, g_ref, yna_ref, gate_ref,
                lng_ref, lnb_ref, bgate_ref, worw_ref, wona_ref, wout_ref, n2g_ref, wrh_ref, wrl_ref, br_ref,
                bd_ref, x1_ref, h2_ref, rt_ref, cnt_ref):
    bd = bd_ref[...]
    y = yf_ref[...] + yb_ref[...]
    yc = y - _seg_dot(y, bd)
    var = _seg_dot(yc * yc, bd)
    yn = yc * lax.rsqrt(var + GN_EPS) * lng_ref[...] + lnb_ref[...]
    ya = (yn + bonf_ref[...] + bonb_ref[...]) * g_ref[...]
    ya_o = _dot(ya.astype(BF16), worw_ref[...])
    yb_o = _dot(yna_ref[...], wona_ref[...])
    gl = gate_ref[...].astype(F32) + bgate_ref[...]
    mix = jax.nn.sigmoid(gl[:, :D_MODEL]) * ya_o + jax.nn.sigmoid(gl[:, D_MODEL:]) * yb_o
    x1 = x_ref[...] + _dot(mix.astype(BF16), wout_ref[...])
    x1_ref[...] = x1
    ms = jnp.mean(x1 * x1, axis=-1, keepdims=True)
    h2 = x1 * lax.rsqrt(ms + RMS_EPS) * n2g_ref[...]
    h2_hi, h2_lo = _split2(h2)
    h2_ref[...] = h2_hi
    logits_t = (_dot_nt(wrh_ref[...], h2_hi) + _dot_nt(wrh_ref[...], h2_lo)
                + _dot_nt(wrl_ref[...], h2_hi) + br_ref[...])
    rec, e1, e2 = _route_t(logits_t)
    rt_ref[...] = rec
    rows_e = lax.broadcasted_iota(jnp.int32, (N_EXPERTS, rec.shape[1]), 0)
    chosen = jnp.where((rows_e == e1) | (rows_e == e2), 1.0, 0.0)
    cnt_ref[0] = jnp.broadcast_to(jnp.sum(chosen, axis=1, keepdims=True), (N_EXPERTS, LANES))


def _out_proj(xf, yf, yb, bonf, bonb, g, yna, gate, lnx_g, lnx_b, b_gate, w_o_rwkv, w_o_na, w_out,
              norm2_g, w_rg, b_rg, w_re, b_re, tm):
    n = xf.shape[0]
    w = RWKV_WIDTH
    assert EXPERTS_PER_GROUP == SUBLANES and N_GROUPS <= SUBLANES
    gpad = SUBLANES - N_GROUPS
    wr_t = jnp.concatenate([w_rg.T, jnp.zeros((gpad, D_MODEL), F32), w_re.T], axis=0)
    wr_hi = wr_t.astype(BF16)
    wr_lo = (wr_t - wr_hi.astype(F32)).astype(BF16)
    br = jnp.concatenate([b_rg, jnp.zeros((gpad,), F32), b_re]).reshape(ROUTER_ROWS, 1)
    bd = _block_diag_heads(w, 1.0 / HEAD_DIM)
    const = lambda i: (0, 0)
    row = lambda i: (i, 0)
    rs = lambda c: pl.BlockSpec((tm, c), row)
    cs = lambda r, c: pl.BlockSpec((r, c), const)
    return pl.pallas_call(
        _out_kernel,
        grid=(n // tm,),
        in_specs=[rs(D_MODEL), rs(w), rs(w), rs(w), rs(w), rs(w), rs(NA_WIDTH), rs(GATE_COLS),
                  cs(1, w), cs(1, w), cs(1, GATE_COLS), cs(w, D_MODEL), cs(NA_WIDTH, D_MODEL),
                  cs(D_MODEL, D_MODEL), cs(1, D_MODEL), cs(ROUTER_ROWS, D_MODEL), cs(ROUTER_ROWS, D_MODEL),
                  cs(ROUTER_ROWS, 1), cs(w, w)],
        out_specs=[rs(D_MODEL), rs(D_MODEL), pl.BlockSpec((ROUTE_REC, tm), lambda i: (0, i)),
                   pl.BlockSpec((1, N_EXPERTS, LANES), lambda i: (i, 0, 0))],
        out_shape=[jax.ShapeDtypeStruct((n, D_MODEL), F32),
                   jax.ShapeDtypeStruct((n, D_MODEL), BF16),
                   jax.ShapeDtypeStruct((ROUTE_REC, n), F32),
                   jax.ShapeDtypeStruct((n // tm, N_EXPERTS, LANES), F32)],
        compiler_params=pltpu.CompilerParams(
            dimension_semantics=("parallel",), vmem_limit_bytes=VMEM_LIMIT_BYTES),
        name="out_proj",
    )(xf, yf, yb, bonf, bonb, g, yna, gate,
      lnx_g.reshape(1, w), lnx_b.reshape(1, w), b_gate.reshape(1, GATE_COLS),
      w_o_rwkv.astype(BF16), w_o_na.astype(BF16), w_out.astype(BF16),
      norm2_g.reshape(1, D_MODEL), wr_hi, wr_lo, br, bd)


def _slot_rows(tm):
    rows = 2 * tm + N_EXPERTS * (SEG - 1)
    return -(-rows // LANES) * LANES


def _moe_plan(cnt, rb, nb_max):
    pc = ((cnt + SEG - 1) // SEG) * SEG
    loc = jnp.cumsum(pc, axis=1) - pc
    within = jnp.cumsum(pc, axis=0) - pc
    region = ((jnp.sum(pc, axis=0) + rb - 1) // rb) * rb
    gbase = (jnp.cumsum(region) - region)[None, :] + within
    blk_end = jnp.cumsum(region // rb)
    nused = blk_end[-1:]
    blk = jnp.minimum(jnp.arange(nb_max, dtype=jnp.int32), nused - 1)
    blk_expert = jnp.minimum(jnp.searchsorted(blk_end, blk, side="right"), N_EXPERTS - 1)
    i32 = lambda a: a.reshape(-1).astype(jnp.int32)
    return loc, i32(loc), i32(gbase), i32(pc // SEG), i32(blk_expert), i32(nused)


def _segment_copies(src_ref, dst_ref, src_s, dst_s, n16_s, tile, sem):
    def seg(e, total):
        k = tile * N_EXPERTS + e
        s0, d0, nblk = src_s[k], dst_s[k], n16_s[k]

        def blk(b, carry):
            pltpu.make_async_copy(src_ref.at[pl.ds(pl.multiple_of(s0 + b * SEG, SEG), SEG)],
                                  dst_ref.at[pl.ds(pl.multiple_of(d0 + b * SEG, SEG), SEG)], sem).start()
            return carry

        lax.fori_loop(0, nblk, blk, 0)
        return total + nblk

    return lax.fori_loop(0, N_EXPERTS, seg, 0)


def _wait_copies(src_ref, dst_ref, count, sem):
    def one(b, carry):
        pltpu.make_async_copy(src_ref.at[pl.ds(0, SEG)], dst_ref.at[pl.ds(0, SEG)], sem).wait()
        return carry

    lax.fori_loop(0, count, one, 0)


def _dispatch_kernel(loc_s, gbase_s, n16_s, h_ref, rt_ref, locc_ref, hs_init_ref, hs_ref, sorted_ref, sem):
    del hs_init_ref
    tile = pl.program_id(0)
    tm = h_ref.shape[0]
    rt = rt_ref[...]
    e1 = rt[0:1].astype(jnp.int32)
    e2 = rt[1:2].astype(jnp.int32)
    rows_e = lax.broadcasted_iota(jnp.int32, (N_EXPERTS, tm), 0)
    ind1 = rows_e == e1
    ind2 = rows_e == e2
    ind = jnp.where(ind1 | ind2, 1.0, 0.0).astype(BF16)
    earlier = jnp.where(lax.broadcasted_iota(jnp.int32, (tm, tm), 0) < lax.broadcasted_iota(jnp.int32, (tm, tm), 1),
                        1.0, 0.0).astype(BF16)
    pos = locc_ref[0] + _dot(ind, earlier)
    pos1 = jnp.sum(jnp.where(ind1, pos, 0.0), axis=0, keepdims=True).astype(jnp.int32)
    pos2 = jnp.sum(jnp.where(ind2, pos, 0.0), axis=0, keepdims=True).astype(jnp.int32)
    slot = lax.broadcasted_iota(jnp.int32, (sorted_ref.shape[0], tm), 0)
    onehot = jnp.where((slot == pos1) | (slot == pos2), 1.0, 0.0).astype(BF16)
    sorted_ref[...] = _dot(onehot, h_ref[...]).astype(BF16)
    count = _segment_copies(sorted_ref, hs_ref, loc_s, gbase_s, n16_s, tile, sem)
    _wait_copies(sorted_ref, hs_ref, count, sem)


def _ffn_kernel(be_s, nused_s, hs_ref, wg_ref, wu_ref, wd_ref, o_ref):
    del be_s
    used = pl.program_id(0) < nused_s[0]

    @pl.when(used)
    def _():
        x = hs_ref[...]
        gate = _dot(x, wg_ref[0].astype(BF16))
        up = _dot(x, wu_ref[0].astype(BF16))
        he = (gate * jax.nn.sigmoid(gate) * up).astype(BF16)
        o_ref[...] = _dot(he, wd_ref[0].astype(BF16)).astype(BF16)

    @pl.when(jnp.logical_not(used))
    def _():
        o_ref[...] = jnp.zeros_like(o_ref)


def _combine_kernel(loc_s, gbase_s, n16_s, x1_ref, rc_ref, locr_ref, ho_ref, out_ref, obuf_ref, sem):
    tile = pl.program_id(0)
    tm = x1_ref.shape[0]

    @pl.when(tile == 0)
    def _():
        obuf_ref[...] = jnp.zeros_like(obuf_ref)

    count = _segment_copies(ho_ref, obuf_ref, gbase_s, loc_s, n16_s, tile, sem)
    rc = rc_ref[...]
    e1 = rc[:, 0:1].astype(jnp.int32)
    e2 = rc[:, 1:2].astype(jnp.int32)
    lane_e = lax.broadcasted_iota(jnp.int32, (tm, LANES), 1)
    ind1 = lane_e == e1
    ind2 = lane_e == e2
    ind = jnp.where(ind1 | ind2, 1.0, 0.0).astype(BF16)
    earlier = jnp.where(lax.broadcasted_iota(jnp.int32, (tm, tm), 1) < lax.broadcasted_iota(jnp.int32, (tm, tm), 0),
                        1.0, 0.0).astype(BF16)
    pos = locr_ref[0] + _dot(earlier, ind)
    pos1 = jnp.sum(jnp.where(ind1, pos, 0.0), axis=1, keepdims=True).astype(jnp.int32)
    pos2 = jnp.sum(jnp.where(ind2, pos, 0.0), axis=1, keepdims=True).astype(jnp.int32)
    slot = lax.broadcasted_iota(jnp.int32, (tm, obuf_ref.shape[0]), 1)
    pick1 = jnp.where(slot == pos1, 1.0, 0.0).astype(BF16)
    pick2 = jnp.where(slot == pos2, 1.0, 0.0).astype(BF16)
    _wait_copies(ho_ref, obuf_ref, count, sem)
    o = obuf_ref[...]
    out_ref[...] = x1_ref[...] + rc[:, 2:3] * _dot(pick1, o) + rc[:, 3:4] * _dot(pick2, o)


def _moe(h2, x1, rt, cnt, w_gate_e, w_up_e, w_down_e, tm, rb):
    n = h2.shape[0]
    ntiles = n // tm
    j_rows = _slot_rows(tm)
    nb_max = -(-(2 * n + ntiles * N_EXPERTS * (SEG - 1) + N_EXPERTS * (rb - 1)) // rb)
    rmax = nb_max * rb
    loc, loc_s, gbase_s, n16_s, blk_expert, nused = _moe_plan(cnt[:, :, 0].astype(jnp.int32), rb, nb_max)
    loc_col = loc.astype(F32).reshape(ntiles, N_EXPERTS, 1)
    loc_row = jnp.pad(loc.astype(F32), ((0, 0), (0, LANES - N_EXPERTS))).reshape(ntiles, 1, LANES)
    any_spec = pl.BlockSpec(memory_space=pl.ANY)
    params = pltpu.CompilerParams(dimension_semantics=("arbitrary",), vmem_limit_bytes=VMEM_LIMIT_BYTES)

    hs = pl.pallas_call(
        _dispatch_kernel,
        grid_spec=pltpu.PrefetchScalarGridSpec(
            num_scalar_prefetch=3, grid=(ntiles,),
            in_specs=[pl.BlockSpec((tm, D_MODEL), lambda i, *_: (i, 0)),
                      pl.BlockSpec((ROUTE_REC, tm), lambda i, *_: (0, i)),
                      pl.BlockSpec((1, N_EXPERTS, 1), lambda i, *_: (i, 0, 0)),
                      any_spec],
            out_specs=any_spec,
            scratch_shapes=[pltpu.VMEM((j_rows, D_MODEL), BF16), pltpu.SemaphoreType.DMA]),
        out_shape=jax.ShapeDtypeStruct((rmax, D_MODEL), BF16),
        input_output_aliases={6: 0},
        compiler_params=params,
        name="moe_dispatch",
    )(loc_s, gbase_s, n16_s, h2, rt, loc_col, jnp.zeros((rmax, D_MODEL), BF16))

    blk_row = lambda b, be, nu: (jnp.minimum(b, nu[0] - 1), 0)
    ho = pl.pallas_call(
        _ffn_kernel,
        grid_spec=pltpu.PrefetchScalarGridSpec(
            num_scalar_prefetch=2, grid=(nb_max,),
            in_specs=[pl.BlockSpec((rb, D_MODEL), blk_row),
                      pl.BlockSpec((1, D_MODEL, D_EXPERT), lambda b, be, nu: (be[b], 0, 0)),
                      pl.BlockSpec((1, D_MODEL, D_EXPERT), lambda b, be, nu: (be[b], 0, 0)),
                      pl.BlockSpec((1, D_EXPERT, D_MODEL), lambda b, be, nu: (be[b], 0, 0))],
            out_specs=pl.BlockSpec((rb, D_MODEL), lambda b, be, nu: (b, 0))),
        out_shape=jax.ShapeDtypeStruct((rmax, D_MODEL), BF16),
        compiler_params=params,
        name="moe_ffn",
    )(blk_expert, nused, hs, w_gate_e, w_up_e, w_down_e)

    return pl.pallas_call(
        _combine_kernel,
        grid_spec=pltpu.PrefetchScalarGridSpec(
            num_scalar_prefetch=3, grid=(ntiles,),
            in_specs=[pl.BlockSpec((tm, D_MODEL), lambda i, *_: (i, 0)),
                      pl.BlockSpec((tm, ROUTE_REC), lambda i, *_: (i, 0)),
                      pl.BlockSpec((1, 1, LANES), lambda i, *_: (i, 0, 0)),
                      any_spec],
            out_specs=pl.BlockSpec((tm, D_MODEL), lambda i, *_: (i, 0)),
            scratch_shapes=[pltpu.VMEM((j_rows, D_MODEL), BF16), pltpu.SemaphoreType.DMA]),
        out_shape=jax.ShapeDtypeStruct((n, D_MODEL), F32),
        compiler_params=params,
        name="moe_combine",
    )(loc_s, gbase_s, n16_s, x1, rt.T, loc_row, ho)


def _row_tile(n, want):
    t = min(want, n)
    while n % t:
        t //= 2
    return t


def kernel(x, norm1_g, w_in, mu_prev, mu_next, w0_f, w_up_f, w0_b, w_up_b, a0_f, a_up_f, a0_b, a_up_b, g_up, k_k, k_a, r_k, lnx_g, lnx_b, q_gain, k_gain, rpb, b_gate, w_o_rwkv, w_o_na, w_out, norm2_g, w_router_group, b_router_group, w_router_expert, b_router_expert, w_gate_e, w_up_e, w_down_e):
    batch, seq, d = x.shape
    assert d == D_MODEL and seq % CHUNK == 0 and seq % GRID_W == 0
    n = batch * seq
    for l in range(norm1_g.shape[0]):
        xf = x.reshape(n, d)
        prw, q, k, v, gate = _in_proj(xf, norm1_g[l], w_in[l].astype(BF16), q_gain[l], k_gain[l],
                                      _row_tile(n, 512))
        yf, bonf, g, yb, bonb = _rwkv(prw, batch, seq, mu_prev[l], mu_next[l], w0_f[l], w_up_f[l],
                                      w0_b[l], w_up_b[l], a0_f[l], a_up_f[l], a0_b[l], a_up_b[l],
                                      g_up[l], k_k[l], k_a[l], r_k[l].reshape(-1))
        yna = _natten(q, k, v, rpb[l], batch, seq)
        tm = _row_tile(n, MOE_TM)
        x1, h2, rt, cnt = _out_proj(xf, yf, yb, bonf, bonb, g, yna, gate, lnx_g[l], lnx_b[l], b_gate[l],
                                    w_o_rwkv[l], w_o_na[l], w_out[l], norm2_g[l],
                                    w_router_group[l], b_router_group[l],
                                    w_router_expert[l], b_router_expert[l], tm)
        out = _moe(h2, x1, rt, cnt, w_gate_e[l], w_up_e[l], w_down_e[l], tm, MOE_RB)
        x = out.reshape(batch, seq, d)
    return x
```

```python
import functools

import numpy as np
import jax
import jax.numpy as jnp
from jax import lax
from jax.experimental import pallas as pl
from jax.experimental.pallas import tpu as pltpu

D_MODEL = 1024
HEAD_DIM = 64
RWKV_HEADS = 8
RWKV_WIDTH = RWKV_HEADS * HEAD_DIM
DECAY_RANK = 64
ICLR_RANK = 64
GATE_RANK = 128
NA_HEADS = 8
NA_WIDTH = NA_HEADS * HEAD_DIM
GRID_W = 64
NA_MAX_WIN_ROWS = 8
NA_WIN_COLS = 16
N_GROUPS = 4
EXPERTS_PER_GROUP = 8
N_EXPERTS = N_GROUPS * EXPERTS_PER_GROUP
D_EXPERT = 256
RMS_EPS = 1e-6
GN_EPS = 64e-5
NEG_INF = -1e30
RWKV_COLS = 3 * RWKV_WIDTH + 2 * DECAY_RANK + 2 * ICLR_RANK + GATE_RANK
NA_COLS = 3 * NA_WIDTH
GATE_COLS = 2 * D_MODEL
IN_COLS = RWKV_COLS + NA_COLS + GATE_COLS

LANES = 128
BF16_SUBLANES = 16
VMEM_LIMIT_BYTES = 56 * 1024 * 1024

CHUNK = 64
RWKV_CHUNKS_PER_STEP = 4
PAIR = LANES // HEAD_DIM
N_PAIRS = RWKV_HEADS // PAIR
EXP_M05 = float(np.exp(-0.5))
NA_ROWS_PER_STEP = 4
SUBLANES = 8
ROUTER_ROWS = SUBLANES + N_EXPERTS
ROUTE_REC = SUBLANES
MOE_TM = 512
MOE_RB = 512
SEG = BF16_SUBLANES

F32 = jnp.float32
BF16 = jnp.bfloat16


def _dot(a, b):
    return jnp.dot(a, b, preferred_element_type=F32)


def _dot_nt(a, b):
    return lax.dot_general(a, b, (((1,), (1,)), ((), ())), preferred_element_type=F32)


def _dot_tn(a, b):
    return lax.dot_general(a, b, (((0,), (0,)), ((), ())), preferred_element_type=F32)


def _split2(z):
    hi = z.astype(BF16)
    lo = (z - hi.astype(F32)).astype(BF16)
    return hi, lo


def _seg_dot(z, bd):
    hi, lo = _split2(z)
    return _dot(hi, bd) + _dot(lo, bd)


def _block_diag_heads(width, value):
    idx = np.arange(width) // HEAD_DIM
    return jnp.asarray((idx[:, None] == idx[None, :]).astype(np.float32) * value, dtype=BF16)


def _inproj_kernel(x_ref, g1_ref, w_ref, qg_ref, kg_ref, bd_ref,
                   prw_ref, q_ref, k_ref, v_ref, gate_ref):
    x = x_ref[...]
    ms = jnp.mean(x * x, axis=-1, keepdims=True)
    h = (x * lax.rsqrt(ms + RMS_EPS) * g1_ref[...]).astype(BF16)

    def proj(lo, hi):
        return _dot(h, w_ref[:, lo:hi])

    col = 0
    while col < RWKV_COLS:
        nxt = min(col + 512, RWKV_COLS)
        prw_ref[:, col:nxt] = proj(col, nxt).astype(BF16)
        col = nxt

    def head_rms(t, gain):
        msq = _dot((t * t).astype(BF16), bd_ref[...])
        return t * lax.rsqrt(msq + RMS_EPS) * gain

    c0 = RWKV_COLS
    q_ref[...] = head_rms(proj(c0, c0 + NA_WIDTH), qg_ref[...]).astype(BF16)
    k_ref[...] = head_rms(proj(c0 + NA_WIDTH, c0 + 2 * NA_WIDTH), kg_ref[...]).astype(BF16)
    v_ref[...] = proj(c0 + 2 * NA_WIDTH, c0 + 3 * NA_WIDTH).astype(BF16)
    c1 = c0 + NA_COLS
    for j in range(GATE_COLS // 512):
        gate_ref[:, j * 512:(j + 1) * 512] = proj(c1 + j * 512, c1 + (j + 1) * 512).astype(BF16)


def _in_proj(xf, g1, w_in_b, q_gain, k_gain, tm):
    n = xf.shape[0]
    qg = (jnp.tile(q_gain, NA_HEADS) * (HEAD_DIM ** -0.5)).reshape(1, NA_WIDTH)
    kg = jnp.tile(k_gain, NA_HEADS).reshape(1, NA_WIDTH)
    bd = _block_diag_heads(NA_WIDTH, 1.0 / HEAD_DIM)
    const = lambda i: (0, 0)
    row = lambda i: (i, 0)
    return pl.pallas_call(
        _inproj_kernel,
        grid=(n // tm,),
        in_specs=[
            pl.BlockSpec((tm, D_MODEL), row),
            pl.BlockSpec((1, D_MODEL), const),
            pl.BlockSpec((D_MODEL, IN_COLS), const),
            pl.BlockSpec((1, NA_WIDTH), const),
            pl.BlockSpec((1, NA_WIDTH), const),
            pl.BlockSpec((NA_WIDTH, NA_WIDTH), const),
        ],
        out_specs=[
            pl.BlockSpec((tm, RWKV_COLS), row),
            pl.BlockSpec((tm, NA_WIDTH), row),
            pl.BlockSpec((tm, NA_WIDTH), row),
            pl.BlockSpec((tm, NA_WIDTH), row),
            pl.BlockSpec((tm, GATE_COLS), row),
        ],
        out_shape=[
            jax.ShapeDtypeStruct((n, RWKV_COLS), BF16),
            jax.ShapeDtypeStruct((n, NA_WIDTH), BF16),
            jax.ShapeDtypeStruct((n, NA_WIDTH), BF16),
            jax.ShapeDtypeStruct((n, NA_WIDTH), BF16),
            jax.ShapeDtypeStruct((n, GATE_COLS), BF16),
        ],
        compiler_params=pltpu.CompilerParams(
            dimension_semantics=("parallel",), vmem_limit_bytes=VMEM_LIMIT_BYTES),
        name="in_proj",
    )(xf, g1.reshape(1, D_MODEL), w_in_b, qg, kg, bd)


def _expand(z, m0):
    return jnp.concatenate([jnp.where(m0, z, 0.0), jnp.where(m0, 0.0, z)], axis=0).astype(BF16)


def _level_mask(t_idx, s_idx, b, reverse):
    same = (t_idx // (2 * b)) == (s_idx // (2 * b))
    t_hi = (t_idx % (2 * b)) >= b
    s_hi = (s_idx % (2 * b)) >= b
    if reverse:
        return same & jnp.logical_not(t_hi) & s_hi
    return same & t_hi & jnp.logical_not(s_hi)


def _rwkv_prep(ps, d, prm, with_gate):
    reverse = d == 1
    w = RWKV_WIDTH
    r = ps[:, 0:w]
    k = ps[:, w:2 * w]
    v = ps[:, 2 * w:3 * w]
    c_dec = 3 * w
    c_icl = c_dec + 2 * DECAY_RANK
    c_gate = c_icl + 2 * ICLR_RANK
    bd1 = prm["bd1"][...]

    w_raw = prm["w0"][d:d + 1, :] + _dot(jnp.tanh(ps[:, c_dec:c_icl]).astype(BF16), prm["wup"][d])
    lw = -EXP_M05 * jax.nn.sigmoid(w_raw)
    a = jax.nn.sigmoid(prm["a0"][d:d + 1, :] + _dot(ps[:, c_icl:c_gate].astype(BF16), prm["aup"][d]))
    kk = k * prm["k_k"][...]
    kk = kk * lax.rsqrt(_seg_dot(kk * kk, bd1) + 1e-12)
    kd = k * (1.0 + (a - 1.0) * prm["k_a"][...])
    bonus = _seg_dot(r * kd * prm["r_k"][...], bd1) * v

    nrow = ps.shape[0]
    rowc = lax.broadcasted_iota(jnp.int32, (nrow, nrow), 0)
    colc = lax.broadcasted_iota(jnp.int32, (nrow, nrow), 1)
    in_chunk = (rowc // CHUNK) == (colc // CHUNK)
    tri = jnp.where(in_chunk & ((colc >= rowc) if reverse else (colc <= rowc)), 1.0, 0.0).astype(BF16)
    lw_hi = lw.astype(BF16)
    rem = lw - lw_hi.astype(F32)
    lw_mid = rem.astype(BF16)
    lw_lo = (rem - lw_mid.astype(F32)).astype(BF16)
    cum = _dot(tri, lw_hi) + _dot(tri, lw_mid) + _dot(tri, lw_lo)
    tots = [jnp.sum(lw[c * CHUNK:(c + 1) * CHUNK], axis=0, keepdims=True) for c in range(nrow // CHUNK)]
    tot = jnp.concatenate([jnp.broadcast_to(t, (CHUNK, w)) for t in tots], axis=0)
    e_rest = jnp.exp(tot - cum)
    e_neg = jnp.exp(-cum)
    beta = kk * a
    out = dict(rt=r * jnp.exp(cum), at=-kk * jnp.exp(cum - lw), kh=kd * e_neg, bh=beta * e_neg,
               kw=kd * e_rest, bw=beta * e_rest, v=v, wc=[jnp.exp(t) for t in tots], bonus=bonus)
    if with_gate:
        out["gate"] = _dot(jax.nn.sigmoid(ps[:, c_gate:c_gate + GATE_RANK]).astype(BF16), prm["gup"][...])
    return out


def _rwkv_chains(preps, s_ref):
    lane = lax.broadcasted_iota(jnp.int32, (1, LANES), 1)
    m0 = lane < HEAD_DIM
    t_idx = lax.broadcasted_iota(jnp.int32, (CHUNK, LANES), 0)
    s_idx = lax.broadcasted_iota(jnp.int32, (CHUNK, LANES), 1) % CHUNK
    eye = jnp.where(s_idx == t_idx, 1.0, 0.0)
    strict = [s_idx < t_idx, s_idx > t_idx]
    incl = [s_idx <= t_idx, s_idx >= t_idx]
    bd_r = lax.broadcasted_iota(jnp.int32, (LANES, LANES), 0) // HEAD_DIM
    bd_c = lax.broadcasted_iota(jnp.int32, (LANES, LANES), 1) // HEAD_DIM
    bdmask = bd_r == bd_c

    ndir = len(preps)
    nck = preps[0]["v"].shape[0] // CHUNK
    chains = [(d, ck, pr) for ck in range(nck) for pr in range(N_PAIRS) for d in range(ndir)]

    def part(c, name):
        d, ck, pr = c
        return preps[d][name][ck * CHUNK:(ck + 1) * CHUNK, pr * LANES:(pr + 1) * LANES]

    ab, ak = {}, {}
    for c in chains:
        x = jnp.concatenate([part(c, "at"), part(c, "rt")], axis=0).astype(BF16)
        abk = _dot_nt(x, jnp.concatenate([_expand(part(c, "bh"), m0), _expand(part(c, "kh"), m0)], axis=0))
        ab[c], ak[c] = abk[:, :LANES], abk[:, LANES:]
    nmat, nb, a_rb, av, tinv = {}, {}, {}, {}, {}
    for c in chains:
        d = c[0]
        nmat[c] = jnp.where(strict[d], ab[c][:CHUNK], 0.0)
        nb[c] = nmat[c].astype(BF16)
        a_rb[c] = jnp.where(incl[d], ab[c][CHUNK:], 0.0).astype(BF16)
        a_k = jnp.concatenate([jnp.where(strict[d], ak[c][:CHUNK], 0.0),
                               jnp.where(incl[d], ak[c][CHUNK:], 0.0)], axis=0).astype(BF16)
        av[c] = _dot(a_k, _expand(part(c, "v"), m0))
        tinv[c] = eye + jnp.where(_level_mask(t_idx, s_idx, 1, d == 1), nmat[c], 0.0)
    b = 2
    while b < CHUNK:
        e = {c: _dot(nb[c], _expand(tinv[c], m0)) for c in chains}
        for c in chains:
            f = _dot(tinv[c].astype(BF16), _expand(e[c], m0))
            tinv[c] = tinv[c] + jnp.where(_level_mask(t_idx, s_idx, b, c[0] == 1), f, 0.0)
        b *= 2
    tu = {c: _dot(tinv[c].astype(BF16),
                  jnp.concatenate([_expand(part(c, "at"), m0), _expand(av[c][:CHUNK], m0)], axis=1))
          for c in chains}
    qy = {c: _dot(a_rb[c], jnp.concatenate([_expand(tu[c][:, :LANES], m0), _expand(tu[c][:, LANES:], m0)],
                                           axis=1))
          for c in chains}
    mt, bt = {}, {}
    for c in chains:
        tn = _dot_tn(tu[c].astype(BF16), part(c, "bw").astype(BF16))
        vk = _dot_tn(part(c, "v").astype(BF16), part(c, "kw").astype(BF16))
        mt[c] = jnp.where(bdmask, tn[:LANES], 0.0).astype(BF16)
        bt[c] = jnp.where(bdmask, tn[LANES:] + vk, 0.0)
    state = {(d, pr): s_ref[d, pr] for d in range(ndir) for pr in range(N_PAIRS)}
    ys = {}
    for step in range(nck):
        for pr in range(N_PAIRS):
            for d in range(ndir):
                c = (d, step if d == 0 else nck - 1 - step, pr)
                s0 = state[(d, pr)]
                s0b = s0.astype(BF16)
                q = part(c, "rt") + qy[c][:, :LANES]
                ys[c] = _dot_nt(q.astype(BF16), s0b) + qy[c][:, LANES:] + av[c][CHUNK:]
                wc = preps[d]["wc"][c[1]][:, pr * LANES:(pr + 1) * LANES]
                state[(d, pr)] = s0 * wc + _dot(s0b, mt[c]) + bt[c]
    for (d, pr), s_new in state.items():
        s_ref[d, pr] = s_new
    return [jnp.concatenate([jnp.concatenate([ys[(d, ck, pr)] for pr in range(N_PAIRS)], axis=1)
                             for ck in range(nck)], axis=0) for d in range(ndir)]


def _rwkv_kernel(pf_ref, pfp_ref, pfn_ref, pb_ref, pbp_ref, pbn_ref,
                 mup_ref, mun_ref, w0_ref, wup_ref, a0_ref, aup_ref, gup_ref,
                 kk_ref, ka_ref, rk_ref, bd1_ref,
                 yf_ref, bonf_ref, g_ref, yb_ref, bonb_ref,
                 buf_ref, s_ref, *, nc):
    i = pl.program_id(1)

    @pl.when(i == 0)
    def _():
        s_ref[...] = jnp.zeros_like(s_ref)

    prm = dict(w0=w0_ref, wup=wup_ref, a0=a0_ref, aup=aup_ref, gup=gup_ref,
               k_k=kk_ref, k_a=ka_ref, r_k=rk_ref, bd1=bd1_ref)
    pad = BF16_SUBLANES
    nrow = pf_ref.shape[0]

    def shifted(d, main_ref, prev_ref, next_ref, c):
        has_prev = jnp.where(c > 0, 1.0, 0.0)
        has_next = jnp.where(c < nc - 1, 1.0, 0.0)
        buf_ref[d, 0:pad, :] = prev_ref[...].astype(F32) * has_prev
        buf_ref[d, pad:pad + nrow, :] = main_ref[...].astype(F32)
        buf_ref[d, pad + nrow:, :] = next_ref[...].astype(F32) * has_next
        p = buf_ref[d, pad:pad + nrow, :]
        p_prev = buf_ref[d, pad - 1:pad - 1 + nrow, :]
        p_next = buf_ref[d, pad + 1:pad + 1 + nrow, :]
        return p + mup_ref[...] * (p_prev - p) + mun_ref[...] * (p_next - p)

    preps = [_rwkv_prep(shifted(0, pf_ref, pfp_ref, pfn_ref, i), 0, prm, True),
             _rwkv_prep(shifted(1, pb_ref, pbp_ref, pbn_ref, nc - 1 - i), 1, prm, False)]
    y_f, y_b = _rwkv_chains(preps, s_ref)
    yf_ref[...] = y_f
    bonf_ref[...] = preps[0]["bonus"]
    g_ref[...] = preps[0]["gate"]
    yb_ref[...] = y_b
    bonb_ref[...] = preps[1]["bonus"]


def _rwkv(prw, batch, seq, mu_prev, mu_next, w0_f, w_up_f, w0_b, w_up_b, a0_f, a_up_f, a0_b, a_up_b,
          g_up, k_k, k_a, r_k):
    n = batch * seq
    cps = RWKV_CHUNKS_PER_STEP if seq % (CHUNK * RWKV_CHUNKS_PER_STEP) == 0 else 1
    nrow = cps * CHUNK
    nc = seq // nrow
    sub = nrow // BF16_SUBLANES
    nsub = n // BF16_SUBLANES
    w = RWKV_WIDTH
    zd = jnp.zeros((DECAY_RANK, w), F32)
    zi = jnp.zeros((ICLR_RANK, w), F32)
    wup = jnp.stack([jnp.concatenate([w_up_f, zd], 0), jnp.concatenate([zd, w_up_b], 0)]).astype(BF16)
    aup = jnp.stack([jnp.concatenate([a_up_f, zi], 0), jnp.concatenate([zi, a_up_b], 0)]).astype(BF16)
    w0 = jnp.stack([w0_f, w0_b])
    a0 = jnp.stack([a0_f, a0_b])
    bd1 = _block_diag_heads(w, 1.0)

    def fwd_c(b, i):
        return i

    def bwd_c(b, i):
        return nc - 1 - i

    def main(cf):
        return lambda b, i: (b * nc + cf(b, i), 0)

    def prev(cf):
        return lambda b, i: (jnp.maximum((b * nc + cf(b, i)) * sub - 1, 0), 0)

    def nxt(cf):
        return lambda b, i: (jnp.minimum((b * nc + cf(b, i)) * sub + sub, nsub - 1), 0)

    const2 = lambda b, i: (0, 0)
    const3 = lambda b, i: (0, 0, 0)
    pspec = lambda f: pl.BlockSpec((nrow, RWKV_COLS), f)
    nspec = lambda f: pl.BlockSpec((BF16_SUBLANES, RWKV_COLS), f)
    ospec = lambda f: pl.BlockSpec((nrow, w), f)
    vec = pl.BlockSpec((1, w), const2)
    out_sds = jax.ShapeDtypeStruct((n, w), F32)
    return pl.pallas_call(
        functools.partial(_rwkv_kernel, nc=nc),
        grid=(batch, nc),
        in_specs=[
            pspec(main(fwd_c)), nspec(prev(fwd_c)), nspec(nxt(fwd_c)),
            pspec(main(bwd_c)), nspec(prev(bwd_c)), nspec(nxt(bwd_c)),
            pl.BlockSpec((1, RWKV_COLS), const2), pl.BlockSpec((1, RWKV_COLS), const2),
            pl.BlockSpec((2, w), const2), pl.BlockSpec((2, 2 * DECAY_RANK, w), const3),
            pl.BlockSpec((2, w), const2), pl.BlockSpec((2, 2 * ICLR_RANK, w), const3),
            pl.BlockSpec((GATE_RANK, w), const2),
            vec, vec, vec,
            pl.BlockSpec((w, w), const2),
        ],
        out_specs=[ospec(main(fwd_c)), ospec(main(fwd_c)), ospec(main(fwd_c)),
                   ospec(main(bwd_c)), ospec(main(bwd_c))],
        out_shape=[out_sds] * 5,
        scratch_shapes=[
            pltpu.VMEM((2, nrow + 2 * BF16_SUBLANES, RWKV_COLS), F32),
            pltpu.VMEM((2, N_PAIRS, LANES, LANES), F32),
        ],
        compiler_params=pltpu.CompilerParams(
            dimension_semantics=("arbitrary", "arbitrary"), vmem_limit_bytes=VMEM_LIMIT_BYTES),
        name="rwkv7",
    )(prw, prw, prw, prw, prw, prw,
      mu_prev.reshape(1, RWKV_COLS), mu_next.reshape(1, RWKV_COLS), w0, wup, a0, aup,
      g_up.astype(BF16), k_k.reshape(1, w), k_a.reshape(1, w), r_k.reshape(1, w), bd1)


def _na_kernel(q_ref, k_ref, v_ref, bias_ref, o_ref, *, rows, kh, rblk):
    j = pl.program_id(1)
    lane = lax.broadcasted_iota(jnp.int32, (1, LANES), 1)
    m0 = lane < HEAD_DIM
    zero = jnp.zeros((), BF16)
    npairs = NA_HEADS // PAIR
    chains = [(r, pr) for r in range(rblk) for pr in range(npairs)]
    starts, deltas = [], []
    for r in range(rblk):
        i = j * rblk + r
        start = jnp.clip(i - kh // 2, 0, rows - kh)
        starts.append(pl.multiple_of(start * GRID_W, GRID_W))
        deltas.append(i - start)
    s = {}
    for (r, pr) in chains:
        sl = slice(pr * LANES, (pr + 1) * LANES)
        qp = q_ref[r * GRID_W:(r + 1) * GRID_W, sl]
        qs = jnp.concatenate([jnp.where(m0, qp, zero), jnp.where(m0, zero, qp)], axis=0)
        kb = k_ref[pl.ds(starts[r], kh * GRID_W), sl]
        s[(r, pr)] = _dot_nt(qs, kb) + bias_ref[deltas[r], pr]
    e, l = {}, {}
    for c in chains:
        m = jnp.max(s[c], axis=-1, keepdims=True)
        p = jnp.exp(s[c] - m)
        l[c] = jnp.sum(p, axis=-1, keepdims=True)
        e[c] = p.astype(BF16)
    for r in range(rblk):
        outs = []
        for pr in range(npairs):
            sl = slice(pr * LANES, (pr + 1) * LANES)
            vb = v_ref[pl.ds(starts[r], kh * GRID_W), sl]
            pv = _dot(e[(r, pr)], vb) / l[(r, pr)]
            outs.append(jnp.where(m0, pv[:GRID_W], pv[GRID_W:]))
        o_ref[r * GRID_W:(r + 1) * GRID_W, :] = jnp.concatenate(outs, axis=1).astype(BF16)


def _na_bias_table(rpb, kh):
    cols = np.arange(GRID_W)
    col_start = np.clip(cols - NA_WIN_COLS // 2, 0, GRID_W - NA_WIN_COLS)
    col_mask = (cols[None, :] >= col_start[:, None]) & (cols[None, :] < col_start[:, None] + NA_WIN_COLS)
    col_off = np.clip(cols[None, :] - cols[:, None] + NA_WIN_COLS - 1, 0, 2 * NA_WIN_COLS - 2)
    delta = np.arange(kh)
    row_off = np.arange(kh)[None, :] - delta[:, None] + NA_MAX_WIN_ROWS - 1
    col_sel = (col_off[None] == np.arange(2 * NA_WIN_COLS - 1)[:, None, None]).astype(np.float32)
    row_sel = (row_off[:, :, None] == np.arange(2 * NA_MAX_WIN_ROWS - 1)).astype(np.float32)
    hi = lax.Precision.HIGHEST
    t = jnp.einsum("hoc,cqk->hoqk", rpb.astype(F32), col_sel, precision=hi)
    t = jnp.einsum("dro,hoqk->dhqrk", row_sel, t, precision=hi)
    t = jnp.where(col_mask[None, None, :, None, :], t, NEG_INF)
    return t.reshape(kh, NA_HEADS // PAIR, PAIR * GRID_W, kh * GRID_W)


def _natten(q, k, v, rpb, batch, seq):
    n = batch * seq
    rows = seq // GRID_W
    kh = min(NA_MAX_WIN_ROWS, rows)
    bias = _na_bias_table(rpb, kh)
    rblk = NA_ROWS_PER_STEP if rows % NA_ROWS_PER_STEP == 0 else 1
    nblk = rows // rblk

    return pl.pallas_call(
        functools.partial(_na_kernel, rows=rows, kh=kh, rblk=rblk),
        grid=(batch, nblk),
        in_specs=[
            pl.BlockSpec((rblk * GRID_W, NA_WIDTH), lambda b, i: (b * nblk + i, 0)),
            pl.BlockSpec((seq, NA_WIDTH), lambda b, i: (b, 0)),
            pl.BlockSpec((seq, NA_WIDTH), lambda b, i: (b, 0)),
            pl.BlockSpec((kh, NA_HEADS // PAIR, PAIR * GRID_W, kh * GRID_W), lambda b, i: (0, 0, 0, 0),
                         pipeline_mode=pl.Buffered(1)),
        ],
        out_specs=pl.BlockSpec((rblk * GRID_W, NA_WIDTH), lambda b, i: (b * nblk + i, 0)),
        out_shape=jax.ShapeDtypeStruct((n, NA_WIDTH), BF16),
        compiler_params=pltpu.CompilerParams(
            dimension_semantics=("parallel", "arbitrary"), vmem_limit_bytes=VMEM_LIMIT_BYTES),
        name="natten2d",
    )(q, k, v, bias)


def _route_t(logits_t):
    tm = logits_t.shape[1]
    sub = lax.broadcasted_iota(jnp.int32, (SUBLANES, tm), 0)
    ninf = -jnp.inf
    gvalid = sub < N_GROUPS
    gl = jnp.where(gvalid, logits_t[0:SUBLANES], ninf)
    gmax = jnp.max(gl, axis=0, keepdims=True)
    g_sel = jnp.min(jnp.where(gl == gmax, sub, SUBLANES), axis=0, keepdims=True)
    p_group = 1.0 / jnp.sum(jnp.where(gvalid, jnp.exp(gl - gmax), 0.0), axis=0, keepdims=True)
    ev = logits_t[SUBLANES:2 * SUBLANES]
    for g in range(1, N_GROUPS):
        lo = SUBLANES + g * EXPERTS_PER_GROUP
        ev = jnp.where(g_sel == g, logits_t[lo:lo + EXPERTS_PER_GROUP], ev)
    m1 = jnp.max(ev, axis=0, keepdims=True)
    i1 = jnp.min(jnp.where(ev == m1, sub, SUBLANES), axis=0, keepdims=True)
    ev2 = jnp.where(sub == i1, ninf, ev)
    m2 = jnp.max(ev2, axis=0, keepdims=True)
    i2 = jnp.min(jnp.where(ev2 == m2, sub, SUBLANES), axis=0, keepdims=True)
    t = jnp.exp(m2 - m1)
    w1 = p_group / (1.0 + t)
    w2 = p_group * t / (1.0 + t)
    e1 = g_sel * EXPERTS_PER_GROUP + i1
    e2 = g_sel * EXPERTS_PER_GROUP + i2
    rec = jnp.where(sub == 0, e1.astype(F32),
                    jnp.where(sub == 1, e2.astype(F32),
                              jnp.where(sub == 2, w1, jnp.where(sub == 3, w2, 0.0))))
    return rec, e1, e2


def _out_kernel(x_ref, yf_ref, yb_ref, bonf_ref, bonb_ref, g_ref, yna_ref, gate_ref,
                lng_ref, lnb_ref, bgate_ref, worw_ref, wona_ref, wout_ref, n2g_ref, wrh_ref, wrl_ref, br_ref,
                bd_ref, x1_ref, h2_ref, rt_ref, cnt_ref):
    bd = bd_ref[...]
    y = yf_ref[...] + yb_ref[...]
    yc = y - _seg_dot(y, bd)
    var = _seg_dot(yc * yc, bd)
    yn = yc * lax.rsqrt(var + GN_EPS) * lng_ref[...] + lnb_ref[...]
    ya = (yn + bonf_ref[...] + bonb_ref[...]) * g_ref[...]
    ya_o = _dot(ya.astype(BF16), worw_ref[...])
    yb_o = _dot(yna_ref[...], wona_ref[...])
    gl = gate_ref[...].astype(F32) + bgate_ref[...]
    mix = jax.nn.sigmoid(gl[:, :D_MODEL]) * ya_o + jax.nn.sigmoid(gl[:, D_MODEL:]) * yb_o
    x1 = x_ref[...] + _dot(mix.astype(BF16), wout_ref[...])
    x1_ref[...] = x1
    ms = jnp.mean(x1 * x1, axis=-1, keepdims=True)
    h2 = x1 * lax.rsqrt(ms + RMS_EPS) * n2g_ref[...]
    h2_hi, h2_lo = _split2(h2)
    h2_ref[...] = h2_hi
    logits_t = (_dot_nt(wrh_ref[...], h2_hi) + _dot_nt(wrh_ref[...], h2_lo)
                + _dot_nt(wrl_ref[...], h2_hi) + br_ref[...])
    rec, e1, e2 = _route_t(logits_t)
    rt_ref[...] = rec
    rows_e = lax.broadcasted_iota(jnp.int32, (N_EXPERTS, rec.shape[1]), 0)
    chosen = jnp.where((rows_e == e1) | (rows_e == e2), 1.0, 0.0)
    cnt_ref[0] = jnp.broadcast_to(jnp.sum(chosen, axis=1, keepdims=True), (N_EXPERTS, LANES))


def _out_proj(xf, yf, yb, bonf, bonb, g, yna, gate, lnx_g, lnx_b, b_gate, w_o_rwkv, w_o_na, w_out,
              norm2_g, w_rg, b_rg, w_re, b_re, tm):
    n = xf.shape[0]
    w = RWKV_WIDTH
    assert EXPERTS_PER_GROUP == SUBLANES and N_GROUPS <= SUBLANES
    gpad = SUBLANES - N_GROUPS
    wr_t = jnp.concatenate([w_rg.T, jnp.zeros((gpad, D_MODEL), F32), w_re.T], axis=0)
    wr_hi = wr_t.astype(BF16)
    wr_lo = (wr_t - wr_hi.astype(F32)).astype(BF16)
    br = jnp.concatenate([b_rg, jnp.zeros((gpad,), F32), b_re]).reshape(ROUTER_ROWS, 1)
    bd = _block_diag_heads(w, 1.0 / HEAD_DIM)
    const = lambda i: (0, 0)
    row = lambda i: (i, 0)
    rs = lambda c: pl.BlockSpec((tm, c), row)
    cs = lambda r, c: pl.BlockSpec((r, c), const)
    return pl.pallas_call(
        _out_kernel,
        grid=(n // tm,),
        in_specs=[rs(D_MODEL), rs(w), rs(w), rs(w), rs(w), rs(w), rs(NA_WIDTH), rs(GATE_COLS),
                  cs(1, w), cs(1, w), cs(1, GATE_COLS), cs(w, D_MODEL), cs(NA_WIDTH, D_MODEL),
                  cs(D_MODEL, D_MODEL), cs(1, D_MODEL), cs(ROUTER_ROWS, D_MODEL), cs(ROUTER_ROWS, D_MODEL),
                  cs(ROUTER_ROWS, 1), cs(w, w)],
        out_specs=[rs(D_MODEL), rs(D_MODEL), pl.BlockSpec((ROUTE_REC, tm), lambda i: (0, i)),
                   pl.BlockSpec((1, N_EXPERTS, LANES), lambda i: (i, 0, 0))],
        out_shape=[jax.ShapeDtypeStruct((n, D_MODEL), F32),
                   jax.ShapeDtypeStruct((n, D_MODEL), BF16),
                   jax.ShapeDtypeStruct((ROUTE_REC, n), F32),
                   jax.ShapeDtypeStruct((n // tm, N_EXPERTS, LANES), F32)],
        compiler_params=pltpu.CompilerParams(
            dimension_semantics=("parallel",), vmem_limit_bytes=VMEM_LIMIT_BYTES),
        name="out_proj",
    )(xf, yf, yb, bonf, bonb, g, yna, gate,
      lnx_g.reshape(1, w), lnx_b.reshape(1, w), b_gate.reshape(1, GATE_COLS),
      w_o_rwkv.astype(BF16), w_o_na.astype(BF16), w_out.astype(BF16),
      norm2_g.reshape(1, D_MODEL), wr_hi, wr_lo, br, bd)


def _slot_rows(tm):
    rows = 2 * tm + N_EXPERTS * (SEG - 1)
    return -(-rows // LANES) * LANES


def _moe_plan(cnt, rb, nb_max):
    ntiles = cnt.shape[0]
    e_before = jnp.asarray(np.tril(np.ones((N_EXPERTS, N_EXPERTS), np.int32), -1))
    t_before = jnp.asarray(np.tril(np.ones((ntiles, ntiles), np.int32), -1))
    pc = ((cnt + SEG - 1) // SEG) * SEG
    loc = jnp.sum(pc[:, None, :] * e_before[None], axis=2)
    within = jnp.sum(t_before[:, :, None] * pc[None], axis=1)
    filled = jnp.sum(pc, axis=0)
    region = ((filled + rb - 1) // rb) * rb
    region_off = jnp.sum(region[None, :] * e_before, axis=1)
    gbase = region_off[None, :] + within
    blk_end = (region_off + region) // rb
    nused = blk_end[-1:]
    blk = jnp.minimum(jnp.arange(nb_max, dtype=jnp.int32), nused - 1)
    blk_expert = jnp.minimum(jnp.sum((blk_end[None, :] <= blk[:, None]).astype(jnp.int32), axis=1),
                             N_EXPERTS - 1)
    i32 = lambda a: a.reshape(-1).astype(jnp.int32)
    return (loc, i32(loc), i32(gbase), i32(pc // SEG), i32(blk_expert), i32(nused),
            i32(region_off + filled), i32((region - filled) // SEG))


def _segment_copies(src_ref, dst_ref, src_s, dst_s, n16_s, tile, sem):
    def seg(e, total):
        k = tile * N_EXPERTS + e
        s0, d0, nblk = src_s[k], dst_s[k], n16_s[k]

        def blk(b, carry):
            pltpu.make_async_copy(src_ref.at[pl.ds(pl.multiple_of(s0 + b * SEG, SEG), SEG)],
                                  dst_ref.at[pl.ds(pl.multiple_of(d0 + b * SEG, SEG), SEG)], sem).start()
            return carry

        lax.fori_loop(0, nblk, blk, 0)
        return total + nblk

    return lax.fori_loop(0, N_EXPERTS, seg, 0)


def _wait_copies(src_ref, dst_ref, count, sem):
    def one(b, carry):
        pltpu.make_async_copy(src_ref.at[pl.ds(0, SEG)], dst_ref.at[pl.ds(0, SEG)], sem).wait()
        return carry

    lax.fori_loop(0, count, one, 0)


def _tile_copy_count(n16_s, tile):
    return lax.fori_loop(0, N_EXPERTS, lambda e, total: total + n16_s[tile * N_EXPERTS + e], 0)


def _dispatch_kernel(loc_s, gbase_s, n16_s, tail_s, tailn_s, nused_s, h_ref, rt_ref, locc_ref, hs_ref,
                     sorted_ref, zero_ref, sems, *, nb_max, rb):
    tile = pl.program_id(0)
    ntiles = pl.num_programs(0)
    slot = tile % 2
    sorted_ref = sorted_ref.at[slot]
    sem = sems.at[slot]

    @pl.when(tile >= 2)
    def _():
        _wait_copies(sorted_ref, hs_ref, _tile_copy_count(n16_s, tile - 2), sem)

    tm = h_ref.shape[0]
    rt = rt_ref[...]
    e1 = rt[0:1].astype(jnp.int32)
    e2 = rt[1:2].astype(jnp.int32)
    rows_e = lax.broadcasted_iota(jnp.int32, (N_EXPERTS, tm), 0)
    ind1 = rows_e == e1
    ind2 = rows_e == e2
    ind = jnp.where(ind1 | ind2, 1.0, 0.0).astype(BF16)
    earlier = jnp.where(lax.broadcasted_iota(jnp.int32, (tm, tm), 0) < lax.broadcasted_iota(jnp.int32, (tm, tm), 1),
                        1.0, 0.0).astype(BF16)
    pos = locc_ref[0] + _dot(ind, earlier)
    pos1 = jnp.sum(jnp.where(ind1, pos, 0.0), axis=0, keepdims=True).astype(jnp.int32)
    pos2 = jnp.sum(jnp.where(ind2, pos, 0.0), axis=0, keepdims=True).astype(jnp.int32)
    row_id = lax.broadcasted_iota(jnp.int32, (sorted_ref.shape[0], tm), 0)
    onehot = jnp.where((row_id == pos1) | (row_id == pos2), 1.0, 0.0).astype(BF16)
    sorted_ref[...] = _dot(onehot, h_ref[...]).astype(BF16)
    count = _segment_copies(sorted_ref, hs_ref, loc_s, gbase_s, n16_s, tile, sem)

    @pl.when(tile == ntiles - 1)
    def _():
        zero_ref[...] = jnp.zeros_like(zero_ref)

        def tail(e, total):
            def piece(b, carry):
                pltpu.make_async_copy(
                    zero_ref.at[pl.ds(0, SEG)],
                    hs_ref.at[pl.ds(pl.multiple_of(tail_s[e] + b * SEG, SEG), SEG)], sems.at[2]).start()
                return carry

            lax.fori_loop(0, tailn_s[e], piece, 0)
            return total + tailn_s[e]

        n_tail = lax.fori_loop(0, N_EXPERTS, tail, 0)

        def block(b, carry):
            pltpu.make_async_copy(zero_ref, hs_ref.at[pl.ds(pl.multiple_of(b * rb, rb), rb)], sems.at[3]).start()
            return carry

        lax.fori_loop(nused_s[0], nb_max, block, 0)
        _wait_copies(zero_ref, hs_ref, n_tail, sems.at[2])

        def block_wait(b, carry):
            pltpu.make_async_copy(zero_ref, hs_ref.at[pl.ds(0, rb)], sems.at[3]).wait()
            return carry

        lax.fori_loop(nused_s[0], nb_max, block_wait, 0)
        _wait_copies(sorted_ref, hs_ref, count, sem)

        @pl.when(tile >= 1)
        def _():
            _wait_copies(sorted_ref, hs_ref, _tile_copy_count(n16_s, tile - 1), sems.at[1 - slot])


def _ffn_kernel(be_s, nused_s, hs_ref, wg_ref, wu_ref, wd_ref, o_ref):
    del be_s
    used = pl.program_id(0) < nused_s[0]

    @pl.when(used)
    def _():
        x = hs_ref[...]
        gate = _dot(x, wg_ref[0].astype(BF16))
        up = _dot(x, wu_ref[0].astype(BF16))
        he = (gate * jax.nn.sigmoid(gate) * up).astype(BF16)
        o_ref[...] = _dot(he, wd_ref[0].astype(BF16)).astype(BF16)

    @pl.when(jnp.logical_not(used))
    def _():
        o_ref[...] = jnp.zeros_like(o_ref)


def _combine_kernel(loc_s, gbase_s, n16_s, x1_ref, rc_ref, locr_ref, ho_ref, out_ref, obuf_ref, sems):
    tile = pl.program_id(0)
    ntiles = pl.num_programs(0)
    slot = tile % 2
    tm = x1_ref.shape[0]

    @pl.when(tile == 0)
    def _():
        obuf_ref[...] = jnp.zeros_like(obuf_ref)
        _segment_copies(ho_ref, obuf_ref.at[0], gbase_s, loc_s, n16_s, 0, sems.at[0])

    @pl.when(tile + 1 < ntiles)
    def _():
        _segment_copies(ho_ref, obuf_ref.at[1 - slot], gbase_s, loc_s, n16_s, tile + 1, sems.at[1 - slot])

    obuf_ref = obuf_ref.at[slot]
    rc = rc_ref[...]
    e1 = rc[:, 0:1].astype(jnp.int32)
    e2 = rc[:, 1:2].astype(jnp.int32)
    lane_e = lax.broadcasted_iota(jnp.int32, (tm, LANES), 1)
    ind1 = lane_e == e1
    ind2 = lane_e == e2
    ind = jnp.where(ind1 | ind2, 1.0, 0.0).astype(BF16)
    earlier = jnp.where(lax.broadcasted_iota(jnp.int32, (tm, tm), 1) < lax.broadcasted_iota(jnp.int32, (tm, tm), 0),
                        1.0, 0.0).astype(BF16)
    pos = locr_ref[0] + _dot(earlier, ind)
    pos1 = jnp.sum(jnp.where(ind1, pos, 0.0), axis=1, keepdims=True).astype(jnp.int32)
    pos2 = jnp.sum(jnp.where(ind2, pos, 0.0), axis=1, keepdims=True).astype(jnp.int32)
    row_id = lax.broadcasted_iota(jnp.int32, (tm, obuf_ref.shape[0]), 1)
    pick1 = jnp.where(row_id == pos1, 1.0, 0.0).astype(BF16)
    pick2 = jnp.where(row_id == pos2, 1.0, 0.0).astype(BF16)
    _wait_copies(ho_ref, obuf_ref, _tile_copy_count(n16_s, tile), sems.at[slot])
    o = obuf_ref[...]
    out_ref[...] = x1_ref[...] + rc[:, 2:3] * _dot(pick1, o) + rc[:, 3:4] * _dot(pick2, o)


def _moe(h2, x1, rt, cnt, w_gate_e, w_up_e, w_down_e, tm, rb):
    n = h2.shape[0]
    ntiles = n // tm
    j_rows = _slot_rows(tm)
    nb_max = -(-(2 * n + ntiles * N_EXPERTS * (SEG - 1) + N_EXPERTS * (rb - 1)) // rb)
    rmax = nb_max * rb
    loc, loc_s, gbase_s, n16_s, blk_expert, nused, tail_s, tailn_s = _moe_plan(
        cnt[:, :, 0].astype(jnp.int32), rb, nb_max)
    loc_col = loc.astype(F32).reshape(ntiles, N_EXPERTS, 1)
    loc_row = jnp.pad(loc.astype(F32), ((0, 0), (0, LANES - N_EXPERTS))).reshape(ntiles, 1, LANES)
    any_spec = pl.BlockSpec(memory_space=pl.ANY)
    params = pltpu.CompilerParams(dimension_semantics=("arbitrary",), vmem_limit_bytes=VMEM_LIMIT_BYTES)

    hs = pl.pallas_call(
        functools.partial(_dispatch_kernel, nb_max=nb_max, rb=rb),
        grid_spec=pltpu.PrefetchScalarGridSpec(
            num_scalar_prefetch=6, grid=(ntiles,),
            in_specs=[pl.BlockSpec((tm, D_MODEL), lambda i, *_: (i, 0)),
                      pl.BlockSpec((ROUTE_REC, tm), lambda i, *_: (0, i)),
                      pl.BlockSpec((1, N_EXPERTS, 1), lambda i, *_: (i, 0, 0))],
            out_specs=any_spec,
            scratch_shapes=[pltpu.VMEM((2, j_rows, D_MODEL), BF16), pltpu.VMEM((rb, D_MODEL), BF16),
                            pltpu.SemaphoreType.DMA((4,))]),
        out_shape=jax.ShapeDtypeStruct((rmax, D_MODEL), BF16),
        compiler_params=params,
        name="moe_dispatch",
    )(loc_s, gbase_s, n16_s, tail_s, tailn_s, nused, h2, rt, loc_col)

    blk_row = lambda b, be, nu: (jnp.minimum(b, nu[0] - 1), 0)
    ho = pl.pallas_call(
        _ffn_kernel,
        grid_spec=pltpu.PrefetchScalarGridSpec(
            num_scalar_prefetch=2, grid=(nb_max,),
            in_specs=[pl.BlockSpec((rb, D_MODEL), blk_row),
                      pl.BlockSpec((1, D_MODEL, D_EXPERT), lambda b, be, nu: (be[b], 0, 0)),
                      pl.BlockSpec((1, D_MODEL, D_EXPERT), lambda b, be, nu: (be[b], 0, 0)),
                      pl.BlockSpec((1, D_EXPERT, D_MODEL), lambda b, be, nu: (be[b], 0, 0))],
            out_specs=pl.BlockSpec((rb, D_MODEL), lambda b, be, nu: (b, 0))),
        out_shape=jax.ShapeDtypeStruct((rmax, D_MODEL), BF16),
        compiler_params=params,
        name="moe_ffn",
    )(blk_expert, nused, hs, w_gate_e, w_up_e, w_down_e)

    return pl.pallas_call(
        _combine_kernel,
        grid_spec=pltpu.PrefetchScalarGridSpec(
            num_scalar_prefetch=3, grid=(ntiles,),
            in_specs=[pl.BlockSpec((tm, D_MODEL), lambda i, *_: (i, 0)),
                      pl.BlockSpec((tm, ROUTE_REC), lambda i, *_: (i, 0)),
                      pl.BlockSpec((1, 1, LANES), lambda i, *_: (i, 0, 0)),
                      any_spec],
            out_specs=pl.BlockSpec((tm, D_MODEL), lambda i, *_: (i, 0)),
            scratch_shapes=[pltpu.VMEM((2, j_rows, D_MODEL), BF16), pltpu.SemaphoreType.DMA((2,))]),
        out_shape=jax.ShapeDtypeStruct((n, D_MODEL), F32),
        compiler_params=params,
        name="moe_combine",
    )(loc_s, gbase_s, n16_s, x1, rt.T, loc_row, ho)


def _row_tile(n, want):
    t = min(want, n)
    while n % t:
        t //= 2
    return t


def kernel(x, norm1_g, w_in, mu_prev, mu_next, w0_f, w_up_f, w0_b, w_up_b, a0_f, a_up_f, a0_b, a_up_b, g_up, k_k, k_a, r_k, lnx_g, lnx_b, q_gain, k_gain, rpb, b_gate, w_o_rwkv, w_o_na, w_out, norm2_g, w_router_group, b_router_group, w_router_expert, b_router_expert, w_gate_e, w_up_e, w_down_e):
    batch, seq, d = x.shape
    assert d == D_MODEL and seq % CHUNK == 0 and seq % GRID_W == 0
    n = batch * seq
    for l in range(norm1_g.shape[0]):
        xf = x.reshape(n, d)
        prw, q, k, v, gate = _in_proj(xf, norm1_g[l], w_in[l].astype(BF16), q_gain[l], k_gain[l],
                                      _row_tile(n, 512))
        yf, bonf, g, yb, bonb = _rwkv(prw, batch, seq, mu_prev[l], mu_next[l], w0_f[l], w_up_f[l],
                                      w0_b[l], w_up_b[l], a0_f[l], a_up_f[l], a0_b[l], a_up_b[l],
                                      g_up[l], k_k[l], k_a[l], r_k[l].reshape(-1))
        yna = _natten(q, k, v, rpb[l], batch, seq)
        tm = _row_tile(n, MOE_TM)
        x1, h2, rt, cnt = _out_proj(xf, yf, yb, bonf, bonb, g, yna, gate, lnx_g[l], lnx_b[l], b_gate[l],
                                    w_o_rwkv[l], w_o_na[l], w_out[l], norm2_g[l],
                                    w_router_group[l], b_router_group[l],
                                    w_router_expert[l], b_router_expert[l], tm)
        out = _moe(h2, x1, rt, cnt, w_gate_e[l], w_up_e[l], w_down_e[l], tm, MOE_RB)
        x = out.reshape(batch, seq, d)
    return x
```

```python
import functools

import numpy as np
import jax
import jax.numpy as jnp
from jax import lax
from jax.experimental import pallas as pl
from jax.experimental.pallas import tpu as pltpu

D_MODEL = 1024
HEAD_DIM = 64
RWKV_HEADS = 8
RWKV_WIDTH = RWKV_HEADS * HEAD_DIM
DECAY_RANK = 64
ICLR_RANK = 64
GATE_RANK = 128
NA_HEADS = 8
NA_WIDTH = NA_HEADS * HEAD_DIM
GRID_W = 64
NA_MAX_WIN_ROWS = 8
NA_WIN_COLS = 16
N_GROUPS = 4
EXPERTS_PER_GROUP = 8
N_EXPERTS = N_GROUPS * EXPERTS_PER_GROUP
D_EXPERT = 256
RMS_EPS = 1e-6
GN_EPS = 64e-5
NEG_INF = -1e30
RWKV_COLS = 3 * RWKV_WIDTH + 2 * DECAY_RANK + 2 * ICLR_RANK + GATE_RANK
NA_COLS = 3 * NA_WIDTH
GATE_COLS = 2 * D_MODEL
IN_COLS = RWKV_COLS + NA_COLS + GATE_COLS

LANES = 128
BF16_SUBLANES = 16
VMEM_LIMIT_BYTES = 56 * 1024 * 1024

CHUNK = 64
RWKV_CHUNKS_PER_STEP = 4
PAIR = LANES // HEAD_DIM
MXU_DIM = 256
RWKV_GW = LANES
RWKV_HG = RWKV_GW // HEAD_DIM
RWKV_NG = RWKV_WIDTH // RWKV_GW
EXP_M05 = float(np.exp(-0.5))
NA_ROWS_PER_STEP = 4
SUBLANES = 8
ROUTER_ROWS = SUBLANES + N_EXPERTS
ROUTE_REC = SUBLANES
MOE_TM = 512
MOE_RB = 512
SEG = BF16_SUBLANES

F32 = jnp.float32
BF16 = jnp.bfloat16


def _dot(a, b):
    return jnp.dot(a, b, preferred_element_type=F32)


def _dot_nt(a, b):
    return lax.dot_general(a, b, (((1,), (1,)), ((), ())), preferred_element_type=F32)


def _dot_tn(a, b):
    return lax.dot_general(a, b, (((0,), (0,)), ((), ())), preferred_element_type=F32)


def _split2(z):
    hi = z.astype(BF16)
    lo = (z - hi.astype(F32)).astype(BF16)
    return hi, lo


def _grouped_dot(zb, bd):
    g = bd.shape[0]
    return jnp.concatenate([_dot(zb[:, j:j + g], bd) for j in range(0, zb.shape[1], g)], axis=1)


def _seg_dot(z, bd):
    hi, lo = _split2(z)
    return _grouped_dot(hi, bd) + _grouped_dot(lo, bd)


def _block_diag_heads(value):
    idx = np.arange(MXU_DIM) // HEAD_DIM
    return jnp.asarray((idx[:, None] == idx[None, :]).astype(np.float32) * value, dtype=BF16)


def _inproj_kernel(x_ref, g1_ref, w_ref, qg_ref, kg_ref, bd_ref,
                   prw_ref, q_ref, k_ref, v_ref, gate_ref):
    x = x_ref[...]
    ms = jnp.mean(x * x, axis=-1, keepdims=True)
    h = (x * lax.rsqrt(ms + RMS_EPS) * g1_ref[...]).astype(BF16)

    def proj(lo, hi):
        return _dot(h, w_ref[:, lo:hi])

    col = 0
    while col < RWKV_COLS:
        nxt = min(col + 512, RWKV_COLS)
        prw_ref[:, col:nxt] = proj(col, nxt).astype(BF16)
        col = nxt

    def head_rms(t, gain):
        msq = _grouped_dot((t * t).astype(BF16), bd_ref[...])
        return t * lax.rsqrt(msq + RMS_EPS) * gain

    c0 = RWKV_COLS
    q_ref[...] = head_rms(proj(c0, c0 + NA_WIDTH), qg_ref[...]).astype(BF16)
    k_ref[...] = head_rms(proj(c0 + NA_WIDTH, c0 + 2 * NA_WIDTH), kg_ref[...]).astype(BF16)
    v_ref[...] = proj(c0 + 2 * NA_WIDTH, c0 + 3 * NA_WIDTH).astype(BF16)
    c1 = c0 + NA_COLS
    for j in range(GATE_COLS // 512):
        gate_ref[:, j * 512:(j + 1) * 512] = proj(c1 + j * 512, c1 + (j + 1) * 512).astype(BF16)


def _in_proj(xf, g1, w_in_b, q_gain, k_gain, tm):
    n = xf.shape[0]
    qg = (jnp.tile(q_gain, NA_HEADS) * (HEAD_DIM ** -0.5)).reshape(1, NA_WIDTH)
    kg = jnp.tile(k_gain, NA_HEADS).reshape(1, NA_WIDTH)
    bd = _block_diag_heads(1.0 / HEAD_DIM)
    const = lambda i: (0, 0)
    row = lambda i: (i, 0)
    return pl.pallas_call(
        _inproj_kernel,
        grid=(n // tm,),
        in_specs=[
            pl.BlockSpec((tm, D_MODEL), row),
            pl.BlockSpec((1, D_MODEL), const),
            pl.BlockSpec((D_MODEL, IN_COLS), const),
            pl.BlockSpec((1, NA_WIDTH), const),
            pl.BlockSpec((1, NA_WIDTH), const),
            pl.BlockSpec((MXU_DIM, MXU_DIM), const),
        ],
        out_specs=[
            pl.BlockSpec((tm, RWKV_COLS), row),
            pl.BlockSpec((tm, NA_WIDTH), row),
            pl.BlockSpec((tm, NA_WIDTH), row),
            pl.BlockSpec((tm, NA_WIDTH), row),
            pl.BlockSpec((tm, GATE_COLS), row),
        ],
        out_shape=[
            jax.ShapeDtypeStruct((n, RWKV_COLS), BF16),
            jax.ShapeDtypeStruct((n, NA_WIDTH), BF16),
            jax.ShapeDtypeStruct((n, NA_WIDTH), BF16),
            jax.ShapeDtypeStruct((n, NA_WIDTH), BF16),
            jax.ShapeDtypeStruct((n, GATE_COLS), BF16),
        ],
        compiler_params=pltpu.CompilerParams(
            dimension_semantics=("parallel",), vmem_limit_bytes=VMEM_LIMIT_BYTES),
        name="in_proj",
    )(xf, g1.reshape(1, D_MODEL), w_in_b, qg, kg, bd)


def _expand(z, lane_head):
    return jnp.concatenate([jnp.where(lane_head == h, z, 0.0) for h in range(RWKV_HG)], axis=0).astype(BF16)


def _level_mask(t_idx, s_idx, b, reverse):
    same = (t_idx // (2 * b)) == (s_idx // (2 * b))
    t_hi = (t_idx % (2 * b)) >= b
    s_hi = (s_idx % (2 * b)) >= b
    if reverse:
        return same & jnp.logical_not(t_hi) & s_hi
    return same & t_hi & jnp.logical_not(s_hi)


def _rwkv_prep(ps, d, prm, with_gate):
    reverse = d == 1
    w = RWKV_WIDTH
    r = ps[:, 0:w]
    k = ps[:, w:2 * w]
    v = ps[:, 2 * w:3 * w]
    c_dec = 3 * w
    c_icl = c_dec + 2 * DECAY_RANK
    c_gate = c_icl + 2 * ICLR_RANK
    bd1 = prm["bd1"][...]

    w_raw = prm["w0"][d:d + 1, :] + _dot(jnp.tanh(ps[:, c_dec:c_icl]).astype(BF16), prm["wup"][d])
    a = jax.nn.sigmoid(prm["a0"][d:d + 1, :] + _dot(ps[:, c_icl:c_gate].astype(BF16), prm["aup"][d]))
    lw = -EXP_M05 * jax.nn.sigmoid(w_raw)
    kk = k * prm["k_k"][...]
    kk = kk * lax.rsqrt(_seg_dot(kk * kk, bd1) + 1e-12)
    kd = k * (1.0 + (a - 1.0) * prm["k_a"][...])
    bonus = _seg_dot(r * kd * prm["r_k"][...], bd1) * v

    nrow = ps.shape[0]
    rowc = lax.broadcasted_iota(jnp.int32, (nrow, nrow), 0)
    colc = lax.broadcasted_iota(jnp.int32, (nrow, nrow), 1)
    in_chunk = (rowc // CHUNK) == (colc // CHUNK)
    tri = jnp.where(in_chunk & ((colc >= rowc) if reverse else (colc <= rowc)), 1.0, 0.0).astype(BF16)
    lw_hi = lw.astype(BF16)
    rem = lw - lw_hi.astype(F32)
    lw_mid = rem.astype(BF16)
    lw_lo = (rem - lw_mid.astype(F32)).astype(BF16)
    cum = _dot(tri, lw_hi) + _dot(tri, lw_mid) + _dot(tri, lw_lo)
    tots =[jnp.sum(lw[c * CHUNK:(c + 1) * CHUNK], axis=0, keepdims=True) for c in range(nrow // CHUNK)]
    tot = jnp.concatenate([jnp.broadcast_to(t, (CHUNK, w)) for t in tots], axis=0)
    e_rest = jnp.exp(tot - cum)
    e_neg = jnp.exp(-cum)
    beta = kk * a
    out = dict(rt=r * jnp.exp(cum), at=-kk * jnp.exp(cum - lw), kh=kd * e_neg, bh=beta * e_neg,
               kw=kd * e_rest, bw=beta * e_rest, v=v, wc=[jnp.exp(t) for t in tots], bonus=bonus)
    if with_gate:
        out["gate"] = _dot(jax.nn.sigmoid(ps[:, c_gate:c_gate + GATE_RANK]).astype(BF16), prm["gup"][...])
    return out


def _rwkv_chains(preps, dirs, s_ref):
    gw = RWKV_GW
    m0 = lax.broadcasted_iota(jnp.int32, (1, gw), 1) // HEAD_DIM
    t_idx = lax.broadcasted_iota(jnp.int32, (CHUNK, gw), 0)
    s_idx = lax.broadcasted_iota(jnp.int32, (CHUNK, gw), 1) % CHUNK
    eye = jnp.where(s_idx == t_idx, 1.0, 0.0)
    strict = [s_idx < t_idx, s_idx > t_idx]
    incl = [s_idx <= t_idx, s_idx >= t_idx]
    bd_r = lax.broadcasted_iota(jnp.int32, (gw, gw), 0) // HEAD_DIM
    bd_c = lax.broadcasted_iota(jnp.int32, (gw, gw), 1) // HEAD_DIM
    bdmask = bd_r == bd_c

    ndir = len(preps)
    nck = preps[0]["v"].shape[0] // CHUNK
    chains = [(d, ck, pr) for ck in range(nck) for pr in range(RWKV_NG) for d in range(ndir)]

    def part(c, name):
        d, ck, pr = c
        return preps[d][name][ck * CHUNK:(ck + 1) * CHUNK, pr * gw:(pr + 1) * gw]

    ab, ak = {}, {}
    for c in chains:
        x = jnp.concatenate([part(c, "at"), part(c, "rt")], axis=0).astype(BF16)
        abk = _dot_nt(x, jnp.concatenate([_expand(part(c, "bh"), m0), _expand(part(c, "kh"), m0)], axis=0))
        ab[c], ak[c] = abk[:, :RWKV_HG * CHUNK], abk[:, RWKV_HG * CHUNK:]
    nmat, nb, a_rb, av, tinv = {}, {}, {}, {}, {}
    for c in chains:
        d = dirs[c[0]]
        nmat[c] = jnp.where(strict[d], ab[c][:CHUNK], 0.0)
        nb[c] = nmat[c].astype(BF16)
        a_rb[c] = jnp.where(incl[d], ab[c][CHUNK:], 0.0).astype(BF16)
        a_k = jnp.concatenate([jnp.where(strict[d], ak[c][:CHUNK], 0.0),
                               jnp.where(incl[d], ak[c][CHUNK:], 0.0)], axis=0).astype(BF16)
        av[c] = _dot(a_k, _expand(part(c, "v"), m0))
        tinv[c] = eye + jnp.where(_level_mask(t_idx, s_idx, 1, d == 1), nmat[c], 0.0)
    b = 2
    while b < CHUNK:
        e = {c: _dot(nb[c], _expand(tinv[c], m0)) for c in chains}
        for c in chains:
            f = _dot(tinv[c].astype(BF16), _expand(e[c], m0))
            tinv[c] = tinv[c] + jnp.where(_level_mask(t_idx, s_idx, b, dirs[c[0]] == 1), f, 0.0)
        b *= 2
    tu = {c: _dot(tinv[c].astype(BF16),
                  jnp.concatenate([_expand(part(c, "at"), m0), _expand(av[c][:CHUNK], m0)], axis=1))
          for c in chains}
    qy = {c: _dot(a_rb[c], jnp.concatenate([_expand(tu[c][:, :gw], m0), _expand(tu[c][:, gw:], m0)], axis=1))
          for c in chains}
    mt, bt = {}, {}
    for c in chains:
        bw = part(c, "bw").astype(BF16)
        ut_bw = _dot_tn(tu[c][:, :gw].astype(BF16), bw)
        ulv = jnp.concatenate([tu[c][:, gw:], part(c, "v")], axis=0).astype(BF16)
        ulv_bk = _dot_tn(ulv, jnp.concatenate([bw, part(c, "kw").astype(BF16)], axis=0))
        mt[c] = jnp.where(bdmask, ut_bw, 0.0).astype(BF16)
        bt[c] = jnp.where(bdmask, ulv_bk, 0.0)
    state = {(d, pr): s_ref[dirs[d], pr] for d in range(ndir) for pr in range(RWKV_NG)}
    ys = {}
    for step in range(nck):
        for pr in range(RWKV_NG):
            for d in range(ndir):
                c = (d, step if dirs[d] == 0 else nck - 1 - step, pr)
                s0 = state[(d, pr)]
                s0b = s0.astype(BF16)
                q = part(c, "rt") + qy[c][:, :gw]
                ys[c] = _dot_nt(q.astype(BF16), s0b) + qy[c][:, gw:] + av[c][CHUNK:]
                wc = preps[d]["wc"][c[1]][:, pr * gw:(pr + 1) * gw]
                state[(d, pr)] = s0 * wc + _dot(s0b, mt[c]) + bt[c]
    for (d, pr), s_new in state.items():
        s_ref[dirs[d], pr] = s_new
    return [jnp.concatenate([jnp.concatenate([ys[(d, ck, pr)] for pr in range(RWKV_NG)], axis=1)
                             for ck in range(nck)], axis=0) for d in range(ndir)]


def _rwkv_kernel(pf_ref, pfp_ref, pfn_ref, pb_ref, pbp_ref, pbn_ref,
                 mup_ref, mun_ref, w0_ref, wup_ref, a0_ref, aup_ref, gup_ref,
                 kk_ref, ka_ref, rk_ref, bd1_ref,
                 yf_ref, bonf_ref, g_ref, yb_ref, bonb_ref,
                 buf_ref, s_ref, *, nc):
    i = pl.program_id(1)

    @pl.when(i == 0)
    def _():
        s_ref[...] = jnp.zeros_like(s_ref)

    prm = dict(w0=w0_ref, wup=wup_ref, a0=a0_ref, aup=aup_ref, gup=gup_ref,
               k_k=kk_ref, k_a=ka_ref, r_k=rk_ref, bd1=bd1_ref)
    pad = BF16_SUBLANES
    nrow = pf_ref.shape[0]

    def shifted(d, main_ref, prev_ref, next_ref, c):
        has_prev = jnp.where(c > 0, 1.0, 0.0)
        has_next = jnp.where(c < nc - 1, 1.0, 0.0)
        buf_ref[d, 0:pad, :] = prev_ref[...].astype(F32) * has_prev
        buf_ref[d, pad:pad + nrow, :] = main_ref[...].astype(F32)
        buf_ref[d, pad + nrow:, :] = next_ref[...].astype(F32) * has_next
        p = buf_ref[d, pad:pad + nrow, :]
        p_prev = buf_ref[d, pad - 1:pad - 1 + nrow, :]
        p_next = buf_ref[d, pad + 1:pad + 1 + nrow, :]
        return p + mup_ref[...] * (p_prev - p) + mun_ref[...] * (p_next - p)

    preps = [_rwkv_prep(shifted(0, pf_ref, pfp_ref, pfn_ref, i), 0, prm, True),
             _rwkv_prep(shifted(1, pb_ref, pbp_ref, pbn_ref, nc - 1 - i), 1, prm, False)]
    y_f, y_b = _rwkv_chains(preps, [0, 1], s_ref)
    yf_ref[...] = y_f
    bonf_ref[...] = preps[0]["bonus"]
    g_ref[...] = preps[0]["gate"]
    yb_ref[...] = y_b
    bonb_ref[...] = preps[1]["bonus"]


def _rwkv(prw, batch, seq, mu_prev, mu_next, w0_f, w_up_f, w0_b, w_up_b, a0_f, a_up_f, a0_b, a_up_b,
          g_up, k_k, k_a, r_k):
    n = batch * seq
    cps = RWKV_CHUNKS_PER_STEP if seq % (CHUNK * RWKV_CHUNKS_PER_STEP) == 0 else 1
    nrow = cps * CHUNK
    nc = seq // nrow
    sub = nrow // BF16_SUBLANES
    nsub = n // BF16_SUBLANES
    w = RWKV_WIDTH
    zd = jnp.zeros((DECAY_RANK, w), F32)
    zi = jnp.zeros((ICLR_RANK, w), F32)
    wup = jnp.stack([jnp.concatenate([w_up_f, zd], 0), jnp.concatenate([zd, w_up_b], 0)]).astype(BF16)
    aup = jnp.stack([jnp.concatenate([a_up_f, zi], 0), jnp.concatenate([zi, a_up_b], 0)]).astype(BF16)
    w0 = jnp.stack([w0_f, w0_b])
    a0 = jnp.stack([a0_f, a0_b])
    bd1 = _block_diag_heads(1.0)

    def fwd_c(b, i):
        return i

    def bwd_c(b, i):
        return nc - 1 - i

    def main(cf):
        return lambda b, i: (b * nc + cf(b, i), 0)

    def prev(cf):
        return lambda b, i: (jnp.maximum((b * nc + cf(b, i)) * sub - 1, 0), 0)

    def nxt(cf):
        return lambda b, i: (jnp.minimum((b * nc + cf(b, i)) * sub + sub, nsub - 1), 0)

    const2 = lambda b, i: (0, 0)
    const3 = lambda b, i: (0, 0, 0)
    pspec = lambda f: pl.BlockSpec((nrow, RWKV_COLS), f)
    nspec = lambda f: pl.BlockSpec((BF16_SUBLANES, RWKV_COLS), f)
    ospec = lambda f: pl.BlockSpec((nrow, w), f)
    vec = pl.BlockSpec((1, w), const2)
    out_sds = jax.ShapeDtypeStruct((n, w), F32)
    return pl.pallas_call(
        functools.partial(_rwkv_kernel, nc=nc),
        grid=(batch, nc),
        in_specs=[
            pspec(main(fwd_c)), nspec(prev(fwd_c)), nspec(nxt(fwd_c)),
            pspec(main(bwd_c)), nspec(prev(bwd_c)), nspec(nxt(bwd_c)),
            pl.BlockSpec((1, RWKV_COLS), const2), pl.BlockSpec((1, RWKV_COLS), const2),
            pl.BlockSpec((2, w), const2), pl.BlockSpec((2, 2 * DECAY_RANK, w), const3),
            pl.BlockSpec((2, w), const2), pl.BlockSpec((2, 2 * ICLR_RANK, w), const3),
            pl.BlockSpec((GATE_RANK, w), const2),
            vec, vec, vec,
            pl.BlockSpec((MXU_DIM, MXU_DIM), const2),
        ],
        out_specs=[ospec(main(fwd_c)), ospec(main(fwd_c)), ospec(main(fwd_c)),
                   ospec(main(bwd_c)), ospec(main(bwd_c))],
        out_shape=[out_sds] * 5,
        scratch_shapes=[
            pltpu.VMEM((2, nrow + 2 * BF16_SUBLANES, RWKV_COLS), F32),
            pltpu.VMEM((2, RWKV_NG, RWKV_GW, RWKV_GW), F32),
        ],
        compiler_params=pltpu.CompilerParams(
            dimension_semantics=("arbitrary", "arbitrary"), vmem_limit_bytes=VMEM_LIMIT_BYTES),
        name="rwkv7",
    )(prw, prw, prw, prw, prw, prw,
      mu_prev.reshape(1, RWKV_COLS), mu_next.reshape(1, RWKV_COLS), w0, wup, a0, aup,
      g_up.astype(BF16), k_k.reshape(1, w), k_a.reshape(1, w), r_k.reshape(1, w), bd1)


def _na_kernel(q_ref, k_ref, v_ref, bias_ref, o_ref, *, rows, kh, rblk):
    j = pl.program_id(1)
    lane = lax.broadcasted_iota(jnp.int32, (1, LANES), 1)
    m0 = lane < HEAD_DIM
    zero = jnp.zeros((), BF16)
    npairs = NA_HEADS // PAIR
    chains = [(r, pr) for r in range(rblk) for pr in range(npairs)]
    starts, deltas = [], []
    for r in range(rblk):
        i = j * rblk + r
        start = jnp.clip(i - kh // 2, 0, rows - kh)
        starts.append(pl.multiple_of(start * GRID_W, GRID_W))
        deltas.append(i - start)
    s = {}
    for (r, pr) in chains:
        sl = slice(pr * LANES, (pr + 1) * LANES)
        qp = q_ref[r * GRID_W:(r + 1) * GRID_W, sl]
        qs = jnp.concatenate([jnp.where(m0, qp, zero), jnp.where(m0, zero, qp)], axis=0)
        kb = k_ref[pl.ds(starts[r], kh * GRID_W), sl]
        s[(r, pr)] = _dot_nt(qs, kb) + bias_ref[deltas[r], pr]
    e, l = {}, {}
    for c in chains:
        m = jnp.max(s[c], axis=-1, keepdims=True)
        p = jnp.exp(s[c] - m)
        l[c] = jnp.sum(p, axis=-1, keepdims=True)
        e[c] = p.astype(BF16)
    for r in range(rblk):
        outs = []
        for pr in range(npairs):
            sl = slice(pr * LANES, (pr + 1) * LANES)
            vb = v_ref[pl.ds(starts[r], kh * GRID_W), sl]
            pv = _dot(e[(r, pr)], vb) / l[(r, pr)]
            outs.append(jnp.where(m0, pv[:GRID_W], pv[GRID_W:]))
        o_ref[r * GRID_W:(r + 1) * GRID_W, :] = jnp.concatenate(outs, axis=1).astype(BF16)


def _na_bias_table(rpb, kh):
    cols = np.arange(GRID_W)
    col_start = np.clip(cols - NA_WIN_COLS // 2, 0, GRID_W - NA_WIN_COLS)
    col_mask = (cols[None, :] >= col_start[:, None]) & (cols[None, :] < col_start[:, None] + NA_WIN_COLS)
    col_off = np.clip(cols[None, :] - cols[:, None] + NA_WIN_COLS - 1, 0, 2 * NA_WIN_COLS - 2)
    delta = np.arange(kh)
    row_off = np.arange(kh)[None, :] - delta[:, None] + NA_MAX_WIN_ROWS - 1
    col_sel = (col_off[None] == np.arange(2 * NA_WIN_COLS - 1)[:, None, None]).astype(np.float32)
    row_sel = (row_off[:, :, None] == np.arange(2 * NA_MAX_WIN_ROWS - 1)).astype(np.float32)
    hi = lax.Precision.HIGHEST
    t = jnp.einsum("hoc,cqk->hoqk", rpb.astype(F32), col_sel, precision=hi)
    t = jnp.einsum("dro,hoqk->dhqrk", row_sel, t, precision=hi)
    t = jnp.where(col_mask[None, None, :, None, :], t, NEG_INF)
    return t.reshape(kh, NA_HEADS // PAIR, PAIR * GRID_W, kh * GRID_W)


def _natten(q, k, v, rpb, batch, seq):
    n = batch * seq
    rows = seq // GRID_W
    kh = min(NA_MAX_WIN_ROWS, rows)
    bias = _na_bias_table(rpb, kh)
    rblk = NA_ROWS_PER_STEP if rows % NA_ROWS_PER_STEP == 0 else 1
    nblk = rows // rblk

    return pl.pallas_call(
        functools.partial(_na_kernel, rows=rows, kh=kh, rblk=rblk),
        grid=(batch, nblk),
        in_specs=[
            pl.BlockSpec((rblk * GRID_W, NA_WIDTH), lambda b, i: (b * nblk + i, 0)),
            pl.BlockSpec((seq, NA_WIDTH), lambda b, i: (b, 0)),
            pl.BlockSpec((seq, NA_WIDTH), lambda b, i: (b, 0)),
            pl.BlockSpec((kh, NA_HEADS // PAIR, PAIR * GRID_W, kh * GRID_W), lambda b, i: (0, 0, 0, 0),
                         pipeline_mode=pl.Buffered(1)),
        ],
        out_specs=pl.BlockSpec((rblk * GRID_W, NA_WIDTH), lambda b, i: (b * nblk + i, 0)),
        out_shape=jax.ShapeDtypeStruct((n, NA_WIDTH), BF16),
        compiler_params=pltpu.CompilerParams(
            dimension_semantics=("parallel", "arbitrary"), vmem_limit_bytes=VMEM_LIMIT_BYTES),
        name="natten2d",
    )(q, k, v, bias)


def _route_t(logits_t):
    tm = logits_t.shape[1]
    sub = lax.broadcasted_iota(jnp.int32, (SUBLANES, tm), 0)
    ninf = -jnp.inf
    gvalid = sub < N_GROUPS
    gl = jnp.where(gvalid, logits_t[0:SUBLANES], ninf)
    gmax = jnp.max(gl, axis=0, keepdims=True)
    g_sel = jnp.min(jnp.where(gl == gmax, sub, SUBLANES), axis=0, keepdims=True)
    p_group = 1.0 / jnp.sum(jnp.where(gvalid, jnp.exp(gl - gmax), 0.0), axis=0, keepdims=True)
    ev = logits_t[SUBLANES:2 * SUBLANES]
    for g in range(1, N_GROUPS):
        lo = SUBLANES + g * EXPERTS_PER_GROUP
        ev = jnp.where(g_sel == g, logits_t[lo:lo + EXPERTS_PER_GROUP], ev)
    m1 = jnp.max(ev, axis=0, keepdims=True)
    i1 = jnp.min(jnp.where(ev == m1, sub, SUBLANES), axis=0, keepdims=True)
    ev2 = jnp.where(sub == i1, ninf, ev)
    m2 = jnp.max(ev2, axis=0, keepdims=True)
    i2 = jnp.min(jnp.where(ev2 == m2, sub, SUBLANES), axis=0, keepdims=True)
    t = jnp.exp(m2 - m1)
    w1 = p_group / (1.0 + t)
    w2 = p_group * t / (1.0 + t)
    e1 = g_sel * EXPERTS_PER_GROUP + i1
    e2 = g_sel * EXPERTS_PER_GROUP + i2
    rec = jnp.where(sub == 0, e1.astype(F32),
                    jnp.where(sub == 1, e2.astype(F32),
                              jnp.where(sub == 2, w1, jnp.where(sub == 3, w2, 0.0))))
    return rec, e1, e2


def _out_kernel(x_ref, yf_ref, yb_ref, bonf_ref, bonb_ref, g_ref, yna_ref, gate_ref,
                lng_ref, lnb_ref, bgate_ref, worw_ref, wona_ref, wout_ref, n2g_ref, wrh_ref, wrl_ref, br_ref,
                bd_ref, x1_ref, h2_ref, rt_ref, cnt_ref):
    bd = bd_ref[...]
    y = yf_ref[...] + yb_ref[...]
    yc = y - _seg_dot(y, bd)
    var = _seg_dot(yc * yc, bd)
    yn = yc * lax.rsqrt(var + GN_EPS) * lng_ref[...] + lnb_ref[...]
    ya = (yn + bonf_ref[...] + bonb_ref[...]) * g_ref[...]
    ya_o = _dot(ya.astype(BF16), worw_ref[...])
    yb_o = _dot(yna_ref[...], wona_ref[...])
    gl = gate_ref[...].astype(F32) + bgate_ref[...]
    mix = jax.nn.sigmoid(gl[:, :D_MODEL]) * ya_o + jax.nn.sigmoid(gl[:, D_MODEL:]) * yb_o
    x1 = x_ref[...] + _dot(mix.astype(BF16), wout_ref[...])
    x1_ref[...] = x1
    ms = jnp.mean(x1 * x1, axis=-1, keepdims=True)
    h2 = x1 * lax.rsqrt(ms + RMS_EPS) * n2g_ref[...]
    h2_hi, h2_lo = _split2(h2)
    h2_ref[...] = h2_hi
    logits_t = (_dot_nt(wrh_ref[...], h2_hi) + _dot_nt(wrh_ref[...], h2_lo)
                + _dot_nt(wrl_ref[...], h2_hi) + br_ref[...])
    rec, e1, e2 = _route_t(logits_t)
    rt_ref[...] = rec
    rows_e = lax.broadcasted_iota(jnp.int32, (N_EXPERTS, rec.shape[1]), 0)
    chosen = jnp.where((rows_e == e1) | (rows_e == e2), 1.0, 0.0)
    cnt_ref[0] = jnp.broadcast_to(jnp.sum(chosen, axis=1, keepdims=True), (N_EXPERTS, LANES))


def _out_proj(xf, yf, yb, bonf, bonb, g, yna, gate, lnx_g, lnx_b, b_gate, w_o_rwkv, w_o_na, w_out,
              norm2_g, w_rg, b_rg, w_re, b_re, tm):
    n = xf.shape[0]
    w = RWKV_WIDTH
    assert EXPERTS_PER_GROUP == SUBLANES and N_GROUPS <= SUBLANES
    gpad = SUBLANES - N_GROUPS
    wr_t = jnp.concatenate([w_rg.T, jnp.zeros((gpad, D_MODEL), F32), w_re.T], axis=0)
    wr_hi = wr_t.astype(BF16)
    wr_lo = (wr_t - wr_hi.astype(F32)).astype(BF16)
    br = jnp.concatenate([b_rg, jnp.zeros((gpad,), F32), b_re]).reshape(ROUTER_ROWS, 1)
    bd = _block_diag_heads(1.0 / HEAD_DIM)
    const = lambda i: (0, 0)
    row = lambda i: (i, 0)
    rs = lambda c: pl.BlockSpec((tm, c), row)
    cs = lambda r, c: pl.BlockSpec((r, c), const)
    return pl.pallas_call(
        _out_kernel,
        grid=(n // tm,),
        in_specs=[rs(D_MODEL), rs(w), rs(w), rs(w), rs(w), rs(w), rs(NA_WIDTH), rs(GATE_COLS),
                  cs(1, w), cs(1, w), cs(1, GATE_COLS), cs(w, D_MODEL), cs(NA_WIDTH, D_MODEL),
                  cs(D_MODEL, D_MODEL), cs(1, D_MODEL), cs(ROUTER_ROWS, D_MODEL), cs(ROUTER_ROWS, D_MODEL),
                  cs(ROUTER_ROWS, 1), cs(MXU_DIM, MXU_DIM)],
        out_specs=[rs(D_MODEL), rs(D_MODEL), pl.BlockSpec((ROUTE_REC, tm), lambda i: (0, i)),
                   pl.BlockSpec((1, N_EXPERTS, LANES), lambda i: (i, 0, 0))],
        out_shape=[jax.ShapeDtypeStruct((n, D_MODEL), F32),
                   jax.ShapeDtypeStruct((n, D_MODEL), BF16),
                   jax.ShapeDtypeStruct((ROUTE_REC, n), F32),
                   jax.ShapeDtypeStruct((n // tm, N_EXPERTS, LANES), F32)],
        compiler_params=pltpu.CompilerParams(
            dimension_semantics=("parallel",), vmem_limit_bytes=VMEM_LIMIT_BYTES),
        name="out_proj",
    )(xf, yf, yb, bonf, bonb, g, yna, gate,
      lnx_g.reshape(1, w), lnx_b.reshape(1, w), b_gate.reshape(1, GATE_COLS),
      w_o_rwkv.astype(BF16), w_o_na.astype(BF16), w_out.astype(BF16),
      norm2_g.reshape(1, D_MODEL), wr_hi, wr_lo, br, bd)


def _slot_rows(tm):
    rows = 2 * tm + N_EXPERTS * (SEG - 1)
    return -(-rows // LANES) * LANES


def _moe_plan(cnt, rb, nb_max, pieces_per_tile):
    ntiles = cnt.shape[0]
    e_before = jnp.asarray(np.tril(np.ones((N_EXPERTS, N_EXPERTS), np.int32), -1))
    t_before = jnp.asarray(np.tril(np.ones((ntiles, ntiles), np.int32), -1))
    pc = ((cnt + SEG - 1) // SEG) * SEG
    loc = jnp.sum(pc[:, None, :] * e_before[None], axis=2)
    within = jnp.sum(t_before[:, :, None] * pc[None], axis=1)
    filled = jnp.sum(pc, axis=0)
    region = ((filled + rb - 1) // rb) * rb
    region_off = jnp.sum(region[None, :] * e_before, axis=1)
    gbase = region_off[None, :] + within
    blk_end = (region_off + region) // rb
    nused = blk_end[-1:]
    blk = jnp.minimum(jnp.arange(nb_max, dtype=jnp.int32), nused - 1)
    blk_expert = jnp.minimum(jnp.sum((blk_end[None, :] <= blk[:, None]).astype(jnp.int32), axis=1),
                             N_EXPERTS - 1)
    n16 = pc // SEG
    end16 = (loc + pc) // SEG
    piece = jnp.arange(pieces_per_tile, dtype=jnp.int32)
    seg_of_piece = jnp.minimum(jnp.sum((end16[:, None, :] <= piece[None, :, None]).astype(jnp.int32), axis=2),
                               N_EXPERTS - 1)
    onehot = (seg_of_piece[:, :, None] == jnp.arange(N_EXPERTS)[None, None, :]).astype(jnp.int32)
    piece_row = jnp.sum(onehot * (gbase - loc)[:, None, :], axis=2) + piece[None, :] * SEG
    i32 = lambda a: a.reshape(-1).astype(jnp.int32)
    return (loc, i32(piece_row), i32(jnp.sum(n16, axis=1)), i32(blk_expert), i32(nused),
            i32(region_off + filled), i32((region - filled) // SEG))


def _piece_copies(local_ref, global_ref, row_s, tile, count, sem, outbound, pieces_per_tile):
    def one(p, carry):
        loc_rows = local_ref.at[pl.ds(pl.multiple_of(p * SEG, SEG), SEG)]
        glob_rows = global_ref.at[pl.ds(pl.multiple_of(row_s[tile * pieces_per_tile + p], SEG), SEG)]
        if outbound:
            pltpu.make_async_copy(loc_rows, glob_rows, sem).start()
        else:
            pltpu.make_async_copy(glob_rows, loc_rows, sem).start()
        return carry

    lax.fori_loop(0, count, one, 0)


def _wait_copies(src_ref, dst_ref, count, sem):
    def one(b, carry):
        pltpu.make_async_copy(src_ref.at[pl.ds(0, SEG)], dst_ref.at[pl.ds(0, SEG)], sem).wait()
        return carry

    lax.fori_loop(0, count, one, 0)


def _dispatch_kernel(row_s, npiece_s, tail_s, tailn_s, nused_s, h_ref, rt_ref, locc_ref, hs_ref,
                     sorted_ref, zero_ref, sems, *, nb_max, rb):
    tile = pl.program_id(0)
    ntiles = pl.num_programs(0)
    slot = tile % 2
    sorted_ref = sorted_ref.at[slot]
    sem = sems.at[slot]
    pieces_per_tile = sorted_ref.shape[0] // SEG

    @pl.when(tile >= 2)
    def _():
        _wait_copies(sorted_ref, hs_ref, npiece_s[tile - 2], sem)

    tm = h_ref.shape[0]
    rt = rt_ref[...]
    e1 = rt[0:1].astype(jnp.int32)
    e2 = rt[1:2].astype(jnp.int32)
    rows_e = lax.broadcasted_iota(jnp.int32, (N_EXPERTS, tm), 0)
    ind1 = rows_e == e1
    ind2 = rows_e == e2
    ind = jnp.where(ind1 | ind2, 1.0, 0.0).astype(BF16)
    earlier = jnp.where(lax.broadcasted_iota(jnp.int32, (tm, tm), 0) < lax.broadcasted_iota(jnp.int32, (tm, tm), 1),
                        1.0, 0.0).astype(BF16)
    pos = locc_ref[0] + _dot(ind, earlier)
    pos1 = jnp.sum(jnp.where(ind1, pos, 0.0), axis=0, keepdims=True).astype(jnp.int32)
    pos2 = jnp.sum(jnp.where(ind2, pos, 0.0), axis=0, keepdims=True).astype(jnp.int32)
    row_id = lax.broadcasted_iota(jnp.int32, (sorted_ref.shape[0], tm), 0)
    onehot = jnp.where((row_id == pos1) | (row_id == pos2), 1.0, 0.0).astype(BF16)
    sorted_ref[...] = _dot(onehot, h_ref[...]).astype(BF16)
    count = npiece_s[tile]
    _piece_copies(sorted_ref, hs_ref, row_s, tile, count, sem, True, pieces_per_tile)

    @pl.when(tile == ntiles - 1)
    def _():
        zero_ref[...] = jnp.zeros_like(zero_ref)

        def tail(e, total):
            def piece(b, carry):
                pltpu.make_async_copy(
                    zero_ref.at[pl.ds(0, SEG)],
                    hs_ref.at[pl.ds(pl.multiple_of(tail_s[e] + b * SEG, SEG), SEG)], sems.at[2]).start()
                return carry

            lax.fori_loop(0, tailn_s[e], piece, 0)
            return total + tailn_s[e]

        n_tail = lax.fori_loop(0, N_EXPERTS, tail, 0)

        def block(b, carry):
            pltpu.make_async_copy(zero_ref, hs_ref.at[pl.ds(pl.multiple_of(b * rb, rb), rb)], sems.at[3]).start()
            return carry

        lax.fori_loop(nused_s[0], nb_max, block, 0)
        _wait_copies(zero_ref, hs_ref, n_tail, sems.at[2])

        def block_wait(b, carry):
            pltpu.make_async_copy(zero_ref, hs_ref.at[pl.ds(0, rb)], sems.at[3]).wait()
            return carry

        lax.fori_loop(nused_s[0], nb_max, block_wait, 0)
        _wait_copies(sorted_ref, hs_ref, count, sem)

        @pl.when(tile >= 1)
        def _():
            _wait_copies(sorted_ref, hs_ref, npiece_s[tile - 1], sems.at[1 - slot])


def _ffn_kernel(be_s, nused_s, hs_ref, wg_ref, wu_ref, wd_ref, o_ref, wgb_ref, wub_ref, wdb_ref):
    b = pl.program_id(0)
    used = b < nused_s[0]

    @pl.when(used & ((b == 0) | (be_s[b] != be_s[jnp.maximum(b - 1, 0)])))
    def _():
        wgb_ref[...] = wg_ref[0].astype(BF16)
        wub_ref[...] = wu_ref[0].astype(BF16)
        wdb_ref[...] = wd_ref[0].astype(BF16)

    @pl.when(used)
    def _():
        x = hs_ref[...]
        gate = _dot(x, wgb_ref[...])
        up = _dot(x, wub_ref[...])
        he = (gate * jax.nn.sigmoid(gate) * up).astype(BF16)
        o_ref[...] = _dot(he, wdb_ref[...]).astype(BF16)

    @pl.when(jnp.logical_not(used))
    def _():
        o_ref[...] = jnp.zeros_like(o_ref)


def _combine_kernel(row_s, npiece_s, x1_ref, rc_ref, locr_ref, ho_ref, out_ref, obuf_ref, sems):
    tile = pl.program_id(0)
    ntiles = pl.num_programs(0)
    slot = tile % 2
    tm = x1_ref.shape[0]
    pieces_per_tile = obuf_ref.shape[1] // SEG

    @pl.when(tile == 0)
    def _():
        obuf_ref[...] = jnp.zeros_like(obuf_ref)
        _piece_copies(obuf_ref.at[0], ho_ref, row_s, 0, npiece_s[0], sems.at[0], False, pieces_per_tile)

    @pl.when(tile + 1 < ntiles)
    def _():
        _piece_copies(obuf_ref.at[1 - slot], ho_ref, row_s, tile + 1, npiece_s[tile + 1], sems.at[1 - slot],
                      False, pieces_per_tile)

    obuf_ref = obuf_ref.at[slot]
    rc = rc_ref[...]
    e1 = rc[:, 0:1].astype(jnp.int32)
    e2 = rc[:, 1:2].astype(jnp.int32)
    lane_e = lax.broadcasted_iota(jnp.int32, (tm, LANES), 1)
    ind1 = lane_e == e1
    ind2 = lane_e == e2
    ind = jnp.where(ind1 | ind2, 1.0, 0.0).astype(BF16)
    earlier = jnp.where(lax.broadcasted_iota(jnp.int32, (tm, tm), 1) < lax.broadcasted_iota(jnp.int32, (tm, tm), 0),
                        1.0, 0.0).astype(BF16)
    pos = locr_ref[0] + _dot(earlier, ind)
    pos1 = jnp.sum(jnp.where(ind1, pos, 0.0), axis=1, keepdims=True).astype(jnp.int32)
    pos2 = jnp.sum(jnp.where(ind2, pos, 0.0), axis=1, keepdims=True).astype(jnp.int32)
    row_id = lax.broadcasted_iota(jnp.int32, (tm, obuf_ref.shape[0]), 1)
    pick = jnp.where(row_id == pos1, rc[:, 2:3], jnp.where(row_id == pos2, rc[:, 3:4], 0.0)).astype(BF16)
    _wait_copies(ho_ref, obuf_ref, npiece_s[tile], sems.at[slot])
    out_ref[...] = x1_ref[...] + _dot(pick, obuf_ref[...])


def _moe(h2, x1, rt, cnt, w_gate_e, w_up_e, w_down_e, tm, rb):
    n = h2.shape[0]
    ntiles = n // tm
    j_rows = _slot_rows(tm)
    nb_max = -(-(2 * n + ntiles * N_EXPERTS * (SEG - 1) + N_EXPERTS * (rb - 1)) // rb)
    rmax = nb_max * rb
    loc, row_s, npiece_s, blk_expert, nused, tail_s, tailn_s = _moe_plan(
        cnt[:, :, 0].astype(jnp.int32), rb, nb_max, j_rows // SEG)
    loc_col = loc.astype(F32).reshape(ntiles, N_EXPERTS, 1)
    loc_row = jnp.pad(loc.astype(F32), ((0, 0), (0, LANES - N_EXPERTS))).reshape(ntiles, 1, LANES)
    any_spec = pl.BlockSpec(memory_space=pl.ANY)
    params = pltpu.CompilerParams(dimension_semantics=("arbitrary",), vmem_limit_bytes=VMEM_LIMIT_BYTES)

    hs = pl.pallas_call(
        functools.partial(_dispatch_kernel, nb_max=nb_max, rb=rb),
        grid_spec=pltpu.PrefetchScalarGridSpec(
            num_scalar_prefetch=5, grid=(ntiles,),
            in_specs=[pl.BlockSpec((tm, D_MODEL), lambda i, *_: (i, 0)),
                      pl.BlockSpec((ROUTE_REC, tm), lambda i, *_: (0, i)),
                      pl.BlockSpec((1, N_EXPERTS, 1), lambda i, *_: (i, 0, 0))],
            out_specs=any_spec,
            scratch_shapes=[pltpu.VMEM((2, j_rows, D_MODEL), BF16), pltpu.VMEM((rb, D_MODEL), BF16),
                            pltpu.SemaphoreType.DMA((4,))]),
        out_shape=jax.ShapeDtypeStruct((rmax, D_MODEL), BF16),
        compiler_params=params,
        name="moe_dispatch",
    )(row_s, npiece_s, tail_s, tailn_s, nused, h2, rt, loc_col)

    blk_row = lambda b, be, nu: (jnp.minimum(b, nu[0] - 1), 0)
    ho = pl.pallas_call(
        _ffn_kernel,
        grid_spec=pltpu.PrefetchScalarGridSpec(
            num_scalar_prefetch=2, grid=(nb_max,),
            in_specs=[pl.BlockSpec((rb, D_MODEL), blk_row),
                      pl.BlockSpec((1, D_MODEL, D_EXPERT), lambda b, be, nu: (be[b], 0, 0)),
                      pl.BlockSpec((1, D_MODEL, D_EXPERT), lambda b, be, nu: (be[b], 0, 0)),
                      pl.BlockSpec((1, D_EXPERT, D_MODEL), lambda b, be, nu: (be[b], 0, 0))],
            out_specs=pl.BlockSpec((rb, D_MODEL), lambda b, be, nu: (b, 0)),
            scratch_shapes=[pltpu.VMEM((D_MODEL, D_EXPERT), BF16), pltpu.VMEM((D_MODEL, D_EXPERT), BF16),
                            pltpu.VMEM((D_EXPERT, D_MODEL), BF16)]),
        out_shape=jax.ShapeDtypeStruct((rmax, D_MODEL), BF16),
        compiler_params=params,
        name="moe_ffn",
    )(blk_expert, nused, hs, w_gate_e, w_up_e, w_down_e)

    return pl.pallas_call(
        _combine_kernel,
        grid_spec=pltpu.PrefetchScalarGridSpec(
            num_scalar_prefetch=2, grid=(ntiles,),
            in_specs=[pl.BlockSpec((tm, D_MODEL), lambda i, *_: (i, 0)),
                      pl.BlockSpec((tm, ROUTE_REC), lambda i, *_: (i, 0)),
                      pl.BlockSpec((1, 1, LANES), lambda i, *_: (i, 0, 0)),
                      any_spec],
            out_specs=pl.BlockSpec((tm, D_MODEL), lambda i, *_: (i, 0)),
            scratch_shapes=[pltpu.VMEM((2, j_rows, D_MODEL), BF16), pltpu.SemaphoreType.DMA((2,))]),
        out_shape=jax.ShapeDtypeStruct((n, D_MODEL), F32),
        compiler_params=params,
        name="moe_combine",
    )(row_s, npiece_s, x1, rt.T, loc_row, ho)


def _row_tile(n, want):
    t = min(want, n)
    while n % t:
        t //= 2
    return t


def kernel(x, norm1_g, w_in, mu_prev, mu_next, w0_f, w_up_f, w0_b, w_up_b, a0_f, a_up_f, a0_b, a_up_b, g_up, k_k, k_a, r_k, lnx_g, lnx_b, q_gain, k_gain, rpb, b_gate, w_o_rwkv, w_o_na, w_out, norm2_g, w_router_group, b_router_group, w_router_expert, b_router_expert, w_gate_e, w_up_e, w_down_e):
    batch, seq, d = x.shape
    assert d == D_MODEL and seq % CHUNK == 0 and seq % GRID_W == 0
    n = batch * seq
    for l in range(norm1_g.shape[0]):
        xf = x.reshape(n, d)
        prw, q, k, v, gate = _in_proj(xf, norm1_g[l], w_in[l].astype(BF16), q_gain[l], k_gain[l],
                                      _row_tile(n, 512))
        yf, bonf, g, yb, bonb = _rwkv(prw, batch, seq, mu_prev[l], mu_next[l], w0_f[l], w_up_f[l],
                                      w0_b[l], w_up_b[l], a0_f[l], a_up_f[l], a0_b[l], a_up_b[l],
                                      g_up[l], k_k[l], k_a[l], r_k[l].reshape(-1))
        yna = _natten(q, k, v, rpb[l], batch, seq)
        tm = _row_tile(n, MOE_TM)
        x1, h2, rt, cnt = _out_proj(xf, yf, yb, bonf, bonb, g, yna, gate, lnx_g[l], lnx_b[l], b_gate[l],
                                    w_o_rwkv[l], w_o_na[l], w_out[l], norm2_g[l],
                                    w_router_group[l], b_router_group[l],
                                    w_router_expert[l], b_router_expert[l], tm)
        out = _moe(h2, x1, rt, cnt, w_gate_e[l], w_up_e[l], w_down_e[l], tm, MOE_RB)
        x = out.reshape(batch, seq, d)
    return x
```

```python
import functools

import numpy as np
import jax
import jax.numpy as jnp
from jax import lax
from jax.experimental import pallas as pl
from jax.experimental.pallas import tpu as pltpu

D_MODEL = 1024
HEAD_DIM = 64
RWKV_HEADS = 8
RWKV_WIDTH = RWKV_HEADS * HEAD_DIM
DECAY_RANK = 64
ICLR_RANK = 64
GATE_RANK = 128
NA_HEADS = 8
NA_WIDTH = NA_HEADS * HEAD_DIM
GRID_W = 64
NA_MAX_WIN_ROWS = 8
NA_WIN_COLS = 16
N_GROUPS = 4
EXPERTS_PER_GROUP = 8
N_EXPERTS = N_GROUPS * EXPERTS_PER_GROUP
D_EXPERT = 256
RMS_EPS = 1e-6
GN_EPS = 64e-5
NEG_INF = -1e30
RWKV_COLS = 3 * RWKV_WIDTH + 2 * DECAY_RANK + 2 * ICLR_RANK + GATE_RANK
NA_COLS = 3 * NA_WIDTH
GATE_COLS = 2 * D_MODEL
IN_COLS = RWKV_COLS + NA_COLS + GATE_COLS

LANES = 128
BF16_SUBLANES = 16
VMEM_LIMIT_BYTES = 56 * 1024 * 1024

CHUNK = 64
RWKV_CHUNKS_PER_STEP = 4
PAIR = LANES // HEAD_DIM
MXU_DIM = 256
RWKV_GW = LANES
RWKV_HG = RWKV_GW // HEAD_DIM
RWKV_NG = RWKV_WIDTH // RWKV_GW
EXP_M05 = float(np.exp(-0.5))
NA_ROWS_PER_STEP = 4
SUBLANES = 8
ROUTER_ROWS = SUBLANES + N_EXPERTS
ROUTE_REC = SUBLANES
MOE_TM = 512
MOE_RB = 512
SEG = BF16_SUBLANES

F32 = jnp.float32
BF16 = jnp.bfloat16


def _dot(a, b):
    return jnp.dot(a, b, preferred_element_type=F32)


def _dot_nt(a, b):
    return lax.dot_general(a, b, (((1,), (1,)), ((), ())), preferred_element_type=F32)


def _dot_tn(a, b):
    return lax.dot_general(a, b, (((0,), (0,)), ((), ())), preferred_element_type=F32)


def _split2(z):
    hi = z.astype(BF16)
    lo = (z - hi.astype(F32)).astype(BF16)
    return hi, lo


def _grouped_dot(zb, bd):
    g = bd.shape[0]
    return jnp.concatenate([_dot(zb[:, j:j + g], bd) for j in range(0, zb.shape[1], g)], axis=1)


def _seg_dot(z, bd):
    hi, lo = _split2(z)
    return _grouped_dot(hi, bd) + _grouped_dot(lo, bd)


def _block_diag_heads(value):
    idx = np.arange(MXU_DIM) // HEAD_DIM
    return jnp.asarray((idx[:, None] == idx[None, :]).astype(np.float32) * value, dtype=BF16)


def _inproj_kernel(x_ref, g1_ref, w_ref, qg_ref, kg_ref, bd_ref,
                   prw_ref, q_ref, k_ref, v_ref, gate_ref):
    x = x_ref[...]
    ms = jnp.mean(x * x, axis=-1, keepdims=True)
    h = (x * lax.rsqrt(ms + RMS_EPS) * g1_ref[...]).astype(BF16)

    def proj(lo, hi):
        return _dot(h, w_ref[:, lo:hi])

    col = 0
    while col < RWKV_COLS:
        nxt = min(col + 512, RWKV_COLS)
        prw_ref[:, col:nxt] = proj(col, nxt).astype(BF16)
        col = nxt

    def head_rms(t, gain):
        msq = _grouped_dot((t * t).astype(BF16), bd_ref[...])
        return t * lax.rsqrt(msq + RMS_EPS) * gain

    c0 = RWKV_COLS
    q_ref[...] = head_rms(proj(c0, c0 + NA_WIDTH), qg_ref[...]).astype(BF16)
    k_ref[...] = head_rms(proj(c0 + NA_WIDTH, c0 + 2 * NA_WIDTH), kg_ref[...]).astype(BF16)
    v_ref[...] = proj(c0 + 2 * NA_WIDTH, c0 + 3 * NA_WIDTH).astype(BF16)
    c1 = c0 + NA_COLS
    for j in range(GATE_COLS // 512):
        gate_ref[:, j * 512:(j + 1) * 512] = proj(c1 + j * 512, c1 + (j + 1) * 512).astype(BF16)


def _in_proj(xf, g1, w_in_b, q_gain, k_gain, tm):
    n = xf.shape[0]
    qg = (jnp.tile(q_gain, NA_HEADS) * (HEAD_DIM ** -0.5)).reshape(1, NA_WIDTH)
    kg = jnp.tile(k_gain, NA_HEADS).reshape(1, NA_WIDTH)
    bd = _block_diag_heads(1.0 / HEAD_DIM)
    const = lambda i: (0, 0)
    row = lambda i: (i, 0)
    return pl.pallas_call(
        _inproj_kernel,
        grid=(n // tm,),
        in_specs=[
            pl.BlockSpec((tm, D_MODEL), row),
            pl.BlockSpec((1, D_MODEL), const),
            pl.BlockSpec((D_MODEL, IN_COLS), const),
            pl.BlockSpec((1, NA_WIDTH), const),
            pl.BlockSpec((1, NA_WIDTH), const),
            pl.BlockSpec((MXU_DIM, MXU_DIM), const),
        ],
        out_specs=[
            pl.BlockSpec((tm, RWKV_COLS), row),
            pl.BlockSpec((tm, NA_WIDTH), row),
            pl.BlockSpec((tm, NA_WIDTH), row),
            pl.BlockSpec((tm, NA_WIDTH), row),
            pl.BlockSpec((tm, GATE_COLS), row),
        ],
        out_shape=[
            jax.ShapeDtypeStruct((n, RWKV_COLS), BF16),
            jax.ShapeDtypeStruct((n, NA_WIDTH), BF16),
            jax.ShapeDtypeStruct((n, NA_WIDTH), BF16),
            jax.ShapeDtypeStruct((n, NA_WIDTH), BF16),
            jax.ShapeDtypeStruct((n, GATE_COLS), BF16),
        ],
        compiler_params=pltpu.CompilerParams(
            dimension_semantics=("parallel",), vmem_limit_bytes=VMEM_LIMIT_BYTES),
        name="in_proj",
    )(xf, g1.reshape(1, D_MODEL), w_in_b, qg, kg, bd)


def _expand(z, lane_head):
    return jnp.concatenate([jnp.where(lane_head == h, z, 0.0) for h in range(RWKV_HG)], axis=0).astype(BF16)


def _level_mask(t_idx, s_idx, b, reverse):
    same = (t_idx // (2 * b)) == (s_idx // (2 * b))
    t_hi = (t_idx % (2 * b)) >= b
    s_hi = (s_idx % (2 * b)) >= b
    if reverse:
        return same & jnp.logical_not(t_hi) & s_hi
    return same & t_hi & jnp.logical_not(s_hi)


def _rwkv_prep(ps, d, prm, with_gate):
    reverse = d == 1
    w = RWKV_WIDTH
    r = ps[:, 0:w]
    k = ps[:, w:2 * w]
    v = ps[:, 2 * w:3 * w]
    c_dec = 3 * w
    c_icl = c_dec + 2 * DECAY_RANK
    c_gate = c_icl + 2 * ICLR_RANK
    bd1 = prm["bd1"][...]

    w_raw = prm["w0"][d:d + 1, :] + _dot(jnp.tanh(ps[:, c_dec:c_icl]).astype(BF16), prm["wup"][d])
    a = jax.nn.sigmoid(prm["a0"][d:d + 1, :] + _dot(ps[:, c_icl:c_gate].astype(BF16), prm["aup"][d]))
    lw = -EXP_M05 * jax.nn.sigmoid(w_raw)
    kk = k * prm["k_k"][...]
    kk = kk * lax.rsqrt(_grouped_dot((kk * kk).astype(BF16), bd1) + 1e-12)
    kd = k * (1.0 + (a - 1.0) * prm["k_a"][...])
    bonus = _seg_dot(r * kd * prm["r_k"][...], bd1) * v

    nrow = ps.shape[0]
    rowc = lax.broadcasted_iota(jnp.int32, (nrow, nrow), 0)
    colc = lax.broadcasted_iota(jnp.int32, (nrow, nrow), 1)
    in_chunk = (rowc // CHUNK) == (colc // CHUNK)
    tri = jnp.where(in_chunk & ((colc >= rowc) if reverse else (colc <= rowc)), 1.0, 0.0).astype(BF16)
    lw_hi = lw.astype(BF16)
    rem = lw - lw_hi.astype(F32)
    lw_mid = rem.astype(BF16)
    lw_lo = (rem - lw_mid.astype(F32)).astype(BF16)
    cum = _dot(tri, lw_hi) + _dot(tri, lw_mid) + _dot(tri, lw_lo)
    tots =[jnp.sum(lw[c * CHUNK:(c + 1) * CHUNK], axis=0, keepdims=True) for c in range(nrow // CHUNK)]
    tot = jnp.concatenate([jnp.broadcast_to(t, (CHUNK, w)) for t in tots], axis=0)
    e_rest = jnp.exp(tot - cum)
    e_neg = jnp.exp(-cum)
    beta = kk * a
    out = dict(rt=r * jnp.exp(cum), at=-kk * jnp.exp(cum - lw), kh=kd * e_neg, bh=beta * e_neg,
               kw=kd * e_rest, bw=beta * e_rest, v=v, wc=[jnp.exp(t) for t in tots], bonus=bonus)
    if with_gate:
        out["gate"] = _dot(jax.nn.sigmoid(ps[:, c_gate:c_gate + GATE_RANK]).astype(BF16), prm["gup"][...])
    return out


def _rwkv_chains(preps, dirs, s_ref):
    gw = RWKV_GW
    m0 = lax.broadcasted_iota(jnp.int32, (1, gw), 1) // HEAD_DIM
    t_idx = lax.broadcasted_iota(jnp.int32, (CHUNK, gw), 0)
    s_idx = lax.broadcasted_iota(jnp.int32, (CHUNK, gw), 1) % CHUNK
    eye = jnp.where(s_idx == t_idx, 1.0, 0.0)
    strict = [s_idx < t_idx, s_idx > t_idx]
    incl = [s_idx <= t_idx, s_idx >= t_idx]
    bd_r = lax.broadcasted_iota(jnp.int32, (gw, gw), 0) // HEAD_DIM
    bd_c = lax.broadcasted_iota(jnp.int32, (gw, gw), 1) // HEAD_DIM
    bdmask = bd_r == bd_c

    ndir = len(preps)
    nck = preps[0]["v"].shape[0] // CHUNK
    chains = [(d, ck, pr) for ck in range(nck) for pr in range(RWKV_NG) for d in range(ndir)]

    def part(c, name):
        d, ck, pr = c
        return preps[d][name][ck * CHUNK:(ck + 1) * CHUNK, pr * gw:(pr + 1) * gw]

    ab, ak = {}, {}
    for c in chains:
        x = jnp.concatenate([part(c, "at"), part(c, "rt")], axis=0).astype(BF16)
        abk = _dot_nt(x, jnp.concatenate([_expand(part(c, "bh"), m0), _expand(part(c, "kh"), m0)], axis=0))
        ab[c], ak[c] = abk[:, :RWKV_HG * CHUNK], abk[:, RWKV_HG * CHUNK:]
    nmat, nb, a_rb, av, tinv = {}, {}, {}, {}, {}
    for c in chains:
        d = dirs[c[0]]
        nmat[c] = jnp.where(strict[d], ab[c][:CHUNK], 0.0)
        nb[c] = nmat[c].astype(BF16)
        a_rb[c] = jnp.where(incl[d], ab[c][CHUNK:], 0.0).astype(BF16)
        a_k = jnp.concatenate([jnp.where(strict[d], ak[c][:CHUNK], 0.0),
                               jnp.where(incl[d], ak[c][CHUNK:], 0.0)], axis=0).astype(BF16)
        av[c] = _dot(a_k, _expand(part(c, "v"), m0))
        tinv[c] = eye + jnp.where(_level_mask(t_idx, s_idx, 1, d == 1), nmat[c], 0.0)
    b = 2
    while b < CHUNK:
        e = {c: _dot(nb[c], _expand(tinv[c], m0)) for c in chains}
        for c in chains:
            f = _dot(tinv[c].astype(BF16), _expand(e[c], m0))
            tinv[c] = tinv[c] + jnp.where(_level_mask(t_idx, s_idx, b, dirs[c[0]] == 1), f, 0.0)
        b *= 2
    tu = {c: _dot(tinv[c].astype(BF16),
                  jnp.concatenate([_expand(part(c, "at"), m0), _expand(av[c][:CHUNK], m0)], axis=1))
          for c in chains}
    qy = {c: _dot(a_rb[c], jnp.concatenate([_expand(tu[c][:, :gw], m0), _expand(tu[c][:, gw:], m0)], axis=1))
          for c in chains}
    mt, bt = {}, {}
    for c in chains:
        bw = part(c, "bw").astype(BF16)
        ut_bw = _dot_tn(tu[c][:, :gw].astype(BF16), bw)
        ulv = jnp.concatenate([tu[c][:, gw:], part(c, "v")], axis=0).astype(BF16)
        ulv_bk = _dot_tn(ulv, jnp.concatenate([bw, part(c, "kw").astype(BF16)], axis=0))
        mt[c] = jnp.where(bdmask, ut_bw, 0.0).astype(BF16)
        bt[c] = jnp.where(bdmask, ulv_bk, 0.0)
    state = {(d, pr): s_ref[dirs[d], pr] for d in range(ndir) for pr in range(RWKV_NG)}
    ys = {}
    for step in range(nck):
        for pr in range(RWKV_NG):
            for d in range(ndir):
                c = (d, step if dirs[d] == 0 else nck - 1 - step, pr)
                s0 = state[(d, pr)]
                s0b = s0.astype(BF16)
                q = part(c, "rt") + qy[c][:, :gw]
                ys[c] = _dot_nt(q.astype(BF16), s0b) + qy[c][:, gw:] + av[c][CHUNK:]
                wc = preps[d]["wc"][c[1]][:, pr * gw:(pr + 1) * gw]
                state[(d, pr)] = s0 * wc + _dot(s0b, mt[c]) + bt[c]
    for (d, pr), s_new in state.items():
        s_ref[dirs[d], pr] = s_new
    return [jnp.concatenate([jnp.concatenate([ys[(d, ck, pr)] for pr in range(RWKV_NG)], axis=1)
                             for ck in range(nck)], axis=0) for d in range(ndir)]


def _rwkv_kernel(pf_ref, pfp_ref, pfn_ref, pb_ref, pbp_ref, pbn_ref,
                 mup_ref, mun_ref, w0_ref, wup_ref, a0_ref, aup_ref, gup_ref,
                 kk_ref, ka_ref, rk_ref, bd1_ref,
                 yf_ref, bonf_ref, g_ref, yb_ref, bonb_ref, s_ref, *, nc):
    i = pl.program_id(1)

    @pl.when(i == 0)
    def _():
        s_ref[...] = jnp.zeros_like(s_ref)

    prm = dict(w0=w0_ref, wup=wup_ref, a0=a0_ref, aup=aup_ref, gup=gup_ref,
               k_k=kk_ref, k_a=ka_ref, r_k=rk_ref, bd1=bd1_ref)
    nrow = pf_ref.shape[0]
    rows = lax.broadcasted_iota(jnp.int32, (nrow, nrow), 0)
    cols = lax.broadcasted_iota(jnp.int32, (nrow, nrow), 1)
    take_prev = jnp.where(cols == rows - 1, 1.0, 0.0).astype(BF16)
    take_next = jnp.where(cols == rows + 1, 1.0, 0.0).astype(BF16)
    sub = lax.broadcasted_iota(jnp.int32, (SUBLANES, 1), 0)

    def shifted(main_ref, prev_ref, next_ref, c):
        pm = main_ref[...]
        p = pm.astype(F32)
        edge_prev = prev_ref[...].astype(F32)[BF16_SUBLANES - 1:, :] * jnp.where(c > 0, 1.0, 0.0)
        edge_next = next_ref[...].astype(F32)[:1, :] * jnp.where(c < nc - 1, 1.0, 0.0)
        p_prev = _dot(take_prev, pm)
        p_next = _dot(take_next, pm)
        p_prev = jnp.concatenate([jnp.where(sub == 0, edge_prev, p_prev[:SUBLANES]), p_prev[SUBLANES:]], axis=0)
        p_next = jnp.concatenate([p_next[:nrow - SUBLANES],
                                  jnp.where(sub == SUBLANES - 1, edge_next, p_next[nrow - SUBLANES:])], axis=0)
        return p + mup_ref[...] * (p_prev - p) + mun_ref[...] * (p_next - p)

    preps = [_rwkv_prep(shifted(pf_ref, pfp_ref, pfn_ref, i), 0, prm, True),
             _rwkv_prep(shifted(pb_ref, pbp_ref, pbn_ref, nc - 1 - i), 1, prm, False)]
    y_f, y_b = _rwkv_chains(preps, [0, 1], s_ref)
    yf_ref[...] = y_f
    bonf_ref[...] = preps[0]["bonus"].astype(BF16)
    g_ref[...] = preps[0]["gate"].astype(BF16)
    yb_ref[...] = y_b
    bonb_ref[...] = preps[1]["bonus"].astype(BF16)


def _rwkv(prw, batch, seq, mu_prev, mu_next, w0_f, w_up_f, w0_b, w_up_b, a0_f, a_up_f, a0_b, a_up_b,
          g_up, k_k, k_a, r_k):
    n = batch * seq
    cps = RWKV_CHUNKS_PER_STEP if seq % (CHUNK * RWKV_CHUNKS_PER_STEP) == 0 else 1
    nrow = cps * CHUNK
    nc = seq // nrow
    sub = nrow // BF16_SUBLANES
    nsub = n // BF16_SUBLANES
    w = RWKV_WIDTH
    zd = jnp.zeros((DECAY_RANK, w), F32)
    zi = jnp.zeros((ICLR_RANK, w), F32)
    wup = jnp.stack([jnp.concatenate([w_up_f, zd], 0), jnp.concatenate([zd, w_up_b], 0)]).astype(BF16)
    aup = jnp.stack([jnp.concatenate([a_up_f, zi], 0), jnp.concatenate([zi, a_up_b], 0)]).astype(BF16)
    w0 = jnp.stack([w0_f, w0_b])
    a0 = jnp.stack([a0_f, a0_b])
    bd1 = _block_diag_heads(1.0)

    def fwd_c(b, i):
        return i

    def bwd_c(b, i):
        return nc - 1 - i

    def main(cf):
        return lambda b, i: (b * nc + cf(b, i), 0)

    def prev(cf):
        return lambda b, i: (jnp.maximum((b * nc + cf(b, i)) * sub - 1, 0), 0)

    def nxt(cf):
        return lambda b, i: (jnp.minimum((b * nc + cf(b, i)) * sub + sub, nsub - 1), 0)

    const2 = lambda b, i: (0, 0)
    const3 = lambda b, i: (0, 0, 0)
    pspec = lambda f: pl.BlockSpec((nrow, RWKV_COLS), f)
    nspec = lambda f: pl.BlockSpec((BF16_SUBLANES, RWKV_COLS), f)
    ospec = lambda f: pl.BlockSpec((nrow, w), f)
    vec = pl.BlockSpec((1, w), const2)
    out_sds = jax.ShapeDtypeStruct((n, w), F32)
    out_bf = jax.ShapeDtypeStruct((n, w), BF16)
    return pl.pallas_call(
        functools.partial(_rwkv_kernel, nc=nc),
        grid=(batch, nc),
        in_specs=[
            pspec(main(fwd_c)), nspec(prev(fwd_c)), nspec(nxt(fwd_c)),
            pspec(main(bwd_c)), nspec(prev(bwd_c)), nspec(nxt(bwd_c)),
            pl.BlockSpec((1, RWKV_COLS), const2), pl.BlockSpec((1, RWKV_COLS), const2),
            pl.BlockSpec((2, w), const2), pl.BlockSpec((2, 2 * DECAY_RANK, w), const3),
            pl.BlockSpec((2, w), const2), pl.BlockSpec((2, 2 * ICLR_RANK, w), const3),
            pl.BlockSpec((GATE_RANK, w), const2),
            vec, vec, vec,
            pl.BlockSpec((MXU_DIM, MXU_DIM), const2),
        ],
        out_specs=[ospec(main(fwd_c)), ospec(main(fwd_c)), ospec(main(fwd_c)),
                   ospec(main(bwd_c)), ospec(main(bwd_c))],
        out_shape=[out_sds, out_bf, out_bf, out_sds, out_bf],
        scratch_shapes=[pltpu.VMEM((2, RWKV_NG, RWKV_GW, RWKV_GW), F32)],
        compiler_params=pltpu.CompilerParams(
            dimension_semantics=("arbitrary", "arbitrary"), vmem_limit_bytes=VMEM_LIMIT_BYTES),
        name="rwkv7",
    )(prw, prw, prw, prw, prw, prw,
      mu_prev.reshape(1, RWKV_COLS), mu_next.reshape(1, RWKV_COLS), w0, wup, a0, aup,
      g_up.astype(BF16), k_k.reshape(1, w), k_a.reshape(1, w), r_k.reshape(1, w), bd1)


def _na_kernel(q_ref, k_ref, v_ref, bias_ref, o_ref, *, rows, kh, rblk):
    j = pl.program_id(1)
    lane = lax.broadcasted_iota(jnp.int32, (1, LANES), 1)
    m0 = lane < HEAD_DIM
    zero = jnp.zeros((), BF16)
    npairs = NA_HEADS // PAIR
    chains = [(r, pr) for r in range(rblk) for pr in range(npairs)]
    starts, deltas = [], []
    for r in range(rblk):
        i = j * rblk + r
        start = jnp.clip(i - kh // 2, 0, rows - kh)
        starts.append(pl.multiple_of(start * GRID_W, GRID_W))
        deltas.append(i - start)
    s = {}
    for (r, pr) in chains:
        sl = slice(pr * LANES, (pr + 1) * LANES)
        qp = q_ref[r * GRID_W:(r + 1) * GRID_W, sl]
        qs = jnp.concatenate([jnp.where(m0, qp, zero), jnp.where(m0, zero, qp)], axis=0)
        kb = k_ref[pl.ds(starts[r], kh * GRID_W), sl]
        s[(r, pr)] = _dot_nt(qs, kb) + bias_ref[deltas[r], pr]
    e, l = {}, {}
    for c in chains:
        m = jnp.max(s[c], axis=-1, keepdims=True)
        p = jnp.exp(s[c] - m)
        l[c] = jnp.sum(p, axis=-1, keepdims=True)
        e[c] = p.astype(BF16)
    for r in range(rblk):
        outs = []
        for pr in range(npairs):
            sl = slice(pr * LANES, (pr + 1) * LANES)
            vb = v_ref[pl.ds(starts[r], kh * GRID_W), sl]
            pv = _dot(e[(r, pr)], vb) / l[(r, pr)]
            outs.append(jnp.where(m0, pv[:GRID_W], pv[GRID_W:]))
        o_ref[r * GRID_W:(r + 1) * GRID_W, :] = jnp.concatenate(outs, axis=1).astype(BF16)


def _na_bias_table(rpb, kh):
    cols = np.arange(GRID_W)
    col_start = np.clip(cols - NA_WIN_COLS // 2, 0, GRID_W - NA_WIN_COLS)
    col_mask = (cols[None, :] >= col_start[:, None]) & (cols[None, :] < col_start[:, None] + NA_WIN_COLS)
    col_off = np.clip(cols[None, :] - cols[:, None] + NA_WIN_COLS - 1, 0, 2 * NA_WIN_COLS - 2)
    delta = np.arange(kh)
    row_off = np.arange(kh)[None, :] - delta[:, None] + NA_MAX_WIN_ROWS - 1
    col_sel = (col_off[None] == np.arange(2 * NA_WIN_COLS - 1)[:, None, None]).astype(np.float32)
    row_sel = (row_off[:, :, None] == np.arange(2 * NA_MAX_WIN_ROWS - 1)).astype(np.float32)
    hi = lax.Precision.HIGHEST
    t = jnp.einsum("hoc,cqk->hoqk", rpb.astype(F32), col_sel, precision=hi)
    t = jnp.einsum("dro,hoqk->dhqrk", row_sel, t, precision=hi)
    t = jnp.where(col_mask[None, None, :, None, :], t, NEG_INF)
    return t.reshape(kh, NA_HEADS // PAIR, PAIR * GRID_W, kh * GRID_W)


def _natten(q, k, v, rpb, batch, seq):
    n = batch * seq
    rows = seq // GRID_W
    kh = min(NA_MAX_WIN_ROWS, rows)
    bias = _na_bias_table(rpb, kh)
    rblk = NA_ROWS_PER_STEP if rows % NA_ROWS_PER_STEP == 0 else 1
    nblk = rows // rblk

    return pl.pallas_call(
        functools.partial(_na_kernel, rows=rows, kh=kh, rblk=rblk),
        grid=(batch, nblk),
        in_specs=[
            pl.BlockSpec((rblk * GRID_W, NA_WIDTH), lambda b, i: (b * nblk + i, 0)),
            pl.BlockSpec((seq, NA_WIDTH), lambda b, i: (b, 0)),
            pl.BlockSpec((seq, NA_WIDTH), lambda b, i: (b, 0)),
            pl.BlockSpec((kh, NA_HEADS // PAIR, PAIR * GRID_W, kh * GRID_W), lambda b, i: (0, 0, 0, 0),
                         pipeline_mode=pl.Buffered(1)),
        ],
        out_specs=pl.BlockSpec((rblk * GRID_W, NA_WIDTH), lambda b, i: (b * nblk + i, 0)),
        out_shape=jax.ShapeDtypeStruct((n, NA_WIDTH), BF16),
        compiler_params=pltpu.CompilerParams(
            dimension_semantics=("parallel", "arbitrary"), vmem_limit_bytes=VMEM_LIMIT_BYTES),
        name="natten2d",
    )(q, k, v, bias)


def _route_t(logits_t):
    tm = logits_t.shape[1]
    sub = lax.broadcasted_iota(jnp.int32, (SUBLANES, tm), 0)
    ninf = -jnp.inf
    gvalid = sub < N_GROUPS
    gl = jnp.where(gvalid, logits_t[0:SUBLANES], ninf)
    gmax = jnp.max(gl, axis=0, keepdims=True)
    g_sel = jnp.min(jnp.where(gl == gmax, sub, SUBLANES), axis=0, keepdims=True)
    p_group = 1.0 / jnp.sum(jnp.where(gvalid, jnp.exp(gl - gmax), 0.0), axis=0, keepdims=True)
    ev = logits_t[SUBLANES:2 * SUBLANES]
    for g in range(1, N_GROUPS):
        lo = SUBLANES + g * EXPERTS_PER_GROUP
        ev = jnp.where(g_sel == g, logits_t[lo:lo + EXPERTS_PER_GROUP], ev)
    m1 = jnp.max(ev, axis=0, keepdims=True)
    i1 = jnp.min(jnp.where(ev == m1, sub, SUBLANES), axis=0, keepdims=True)
    ev2 = jnp.where(sub == i1, ninf, ev)
    m2 = jnp.max(ev2, axis=0, keepdims=True)
    i2 = jnp.min(jnp.where(ev2 == m2, sub, SUBLANES), axis=0, keepdims=True)
    t = jnp.exp(m2 - m1)
    w1 = p_group / (1.0 + t)
    w2 = p_group * t / (1.0 + t)
    e1 = g_sel * EXPERTS_PER_GROUP + i1
    e2 = g_sel * EXPERTS_PER_GROUP + i2
    rec = jnp.where(sub == 0, e1.astype(F32),
                    jnp.where(sub == 1, e2.astype(F32),
                              jnp.where(sub == 2, w1, jnp.where(sub == 3, w2, 0.0))))
    return rec, e1, e2


def _out_kernel(x_ref, yf_ref, yb_ref, bonf_ref, bonb_ref, g_ref, yna_ref, gate_ref,
                lng_ref, lnb_ref, bgate_ref, worw_ref, wona_ref, wout_ref, n2g_ref, wrh_ref, wrl_ref, br_ref,
                bd_ref, x1_ref, h2_ref, rt_ref, cnt_ref):
    bd = bd_ref[...]
    y = yf_ref[...] + yb_ref[...]
    yc = y - _seg_dot(y, bd)
    var = _seg_dot(yc * yc, bd)
    yn = yc * lax.rsqrt(var + GN_EPS) * lng_ref[...] + lnb_ref[...]
    ya = (yn + bonf_ref[...].astype(F32) + bonb_ref[...].astype(F32)) * g_ref[...].astype(F32)
    ya_o = _dot(ya.astype(BF16), worw_ref[...])
    yb_o = _dot(yna_ref[...], wona_ref[...])
    gl = gate_ref[...].astype(F32) + bgate_ref[...]
    mix = jax.nn.sigmoid(gl[:, :D_MODEL]) * ya_o + jax.nn.sigmoid(gl[:, D_MODEL:]) * yb_o
    x1 = x_ref[...] + _dot(mix.astype(BF16), wout_ref[...])
    x1_ref[...] = x1
    ms = jnp.mean(x1 * x1, axis=-1, keepdims=True)
    h2 = x1 * lax.rsqrt(ms + RMS_EPS) * n2g_ref[...]
    h2_hi, h2_lo = _split2(h2)
    h2_ref[...] = h2_hi
    logits_t = (_dot_nt(wrh_ref[...], h2_hi) + _dot_nt(wrh_ref[...], h2_lo)
                + _dot_nt(wrl_ref[...], h2_hi) + br_ref[...])
    rec, e1, e2 = _route_t(logits_t)
    rt_ref[...] = rec
    rows_e = lax.broadcasted_iota(jnp.int32, (N_EXPERTS, rec.shape[1]), 0)
    chosen = jnp.where((rows_e == e1) | (rows_e == e2), 1.0, 0.0)
    cnt_ref[0] = jnp.broadcast_to(jnp.sum(chosen, axis=1, keepdims=True), (N_EXPERTS, LANES))


def _out_proj(xf, yf, yb, bonf, bonb, g, yna, gate, lnx_g, lnx_b, b_gate, w_o_rwkv, w_o_na, w_out,
              norm2_g, w_rg, b_rg, w_re, b_re, tm):
    n = xf.shape[0]
    w = RWKV_WIDTH
    assert EXPERTS_PER_GROUP == SUBLANES and N_GROUPS <= SUBLANES
    gpad = SUBLANES - N_GROUPS
    wr_t = jnp.concatenate([w_rg.T, jnp.zeros((gpad, D_MODEL), F32), w_re.T], axis=0)
    wr_hi = wr_t.astype(BF16)
    wr_lo = (wr_t - wr_hi.astype(F32)).astype(BF16)
    br = jnp.concatenate([b_rg, jnp.zeros((gpad,), F32), b_re]).reshape(ROUTER_ROWS, 1)
    bd = _block_diag_heads(1.0 / HEAD_DIM)
    const = lambda i: (0, 0)
    row = lambda i: (i, 0)
    rs = lambda c: pl.BlockSpec((tm, c), row)
    cs = lambda r, c: pl.BlockSpec((r, c), const)
    return pl.pallas_call(
        _out_kernel,
        grid=(n // tm,),
        in_specs=[rs(D_MODEL), rs(w), rs(w), rs(w), rs(w), rs(w), rs(NA_WIDTH), rs(GATE_COLS),
                  cs(1, w), cs(1, w), cs(1, GATE_COLS), cs(w, D_MODEL), cs(NA_WIDTH, D_MODEL),
                  cs(D_MODEL, D_MODEL), cs(1, D_MODEL), cs(ROUTER_ROWS, D_MODEL), cs(ROUTER_ROWS, D_MODEL),
                  cs(ROUTER_ROWS, 1), cs(MXU_DIM, MXU_DIM)],
        out_specs=[rs(D_MODEL), rs(D_MODEL), pl.BlockSpec((ROUTE_REC, tm), lambda i: (0, i)),
                   pl.BlockSpec((1, N_EXPERTS, LANES), lambda i: (i, 0, 0))],
        out_shape=[jax.ShapeDtypeStruct((n, D_MODEL), F32),
                   jax.ShapeDtypeStruct((n, D_MODEL), BF16),
                   jax.ShapeDtypeStruct((ROUTE_REC, n), F32),
                   jax.ShapeDtypeStruct((n // tm, N_EXPERTS, LANES), F32)],
        compiler_params=pltpu.CompilerParams(
            dimension_semantics=("parallel",), vmem_limit_bytes=VMEM_LIMIT_BYTES),
        name="out_proj",
    )(xf, yf, yb, bonf, bonb, g, yna, gate,
      lnx_g.reshape(1, w), lnx_b.reshape(1, w), b_gate.reshape(1, GATE_COLS),
      w_o_rwkv.astype(BF16), w_o_na.astype(BF16), w_out.astype(BF16),
      norm2_g.reshape(1, D_MODEL), wr_hi, wr_lo, br, bd)


def _slot_rows(tm):
    rows = 2 * tm + N_EXPERTS * (SEG - 1)
    return -(-rows // LANES) * LANES


def _moe_plan(cnt, rb, nb_max, pieces_per_tile):
    ntiles = cnt.shape[0]
    e_before = jnp.asarray(np.tril(np.ones((N_EXPERTS, N_EXPERTS), np.int32), -1))
    t_before = jnp.asarray(np.tril(np.ones((ntiles, ntiles), np.int32), -1))
    pc = ((cnt + SEG - 1) // SEG) * SEG
    loc = jnp.sum(pc[:, None, :] * e_before[None], axis=2)
    within = jnp.sum(t_before[:, :, None] * pc[None], axis=1)
    filled = jnp.sum(pc, axis=0)
    region = ((filled + rb - 1) // rb) * rb
    region_off = jnp.sum(region[None, :] * e_before, axis=1)
    gbase = region_off[None, :] + within
    blk_end = (region_off + region) // rb
    nused = blk_end[-1:]
    blk = jnp.minimum(jnp.arange(nb_max, dtype=jnp.int32), nused - 1)
    blk_expert = jnp.minimum(jnp.sum((blk_end[None, :] <= blk[:, None]).astype(jnp.int32), axis=1),
                             N_EXPERTS - 1)
    n16 = pc // SEG
    end16 = (loc + pc) // SEG
    piece = jnp.arange(pieces_per_tile, dtype=jnp.int32)
    seg_of_piece = jnp.minimum(jnp.sum((end16[:, None, :] <= piece[None, :, None]).astype(jnp.int32), axis=2),
                               N_EXPERTS - 1)
    onehot = (seg_of_piece[:, :, None] == jnp.arange(N_EXPERTS)[None, None, :]).astype(jnp.int32)
    piece_row = jnp.sum(onehot * (gbase - loc)[:, None, :], axis=2) + piece[None, :] * SEG
    i32 = lambda a: a.reshape(-1).astype(jnp.int32)
    return (loc, i32(piece_row), i32(jnp.sum(n16, axis=1)), i32(blk_expert), i32(nused),
            i32(region_off + filled), i32((region - filled) // SEG))


def _piece_copies(local_ref, global_ref, row_s, tile, count, sem, outbound, pieces_per_tile):
    def one(p, carry):
        loc_rows = local_ref.at[pl.ds(pl.multiple_of(p * SEG, SEG), SEG)]
        glob_rows = global_ref.at[pl.ds(pl.multiple_of(row_s[tile * pieces_per_tile + p], SEG), SEG)]
        if outbound:
            pltpu.make_async_copy(loc_rows, glob_rows, sem).start()
        else:
            pltpu.make_async_copy(glob_rows, loc_rows, sem).start()
        return carry

    lax.fori_loop(0, count, one, 0)


def _wait_copies(src_ref, dst_ref, count, sem):
    def one(b, carry):
        pltpu.make_async_copy(src_ref.at[pl.ds(0, SEG)], dst_ref.at[pl.ds(0, SEG)], sem).wait()
        return carry

    lax.fori_loop(0, count, one, 0)


def _dispatch_kernel(row_s, npiece_s, tail_s, tailn_s, nused_s, h_ref, rt_ref, locc_ref, hs_ref,
                     sorted_ref, zero_ref, sems, *, nb_max, rb):
    tile = pl.program_id(0)
    ntiles = pl.num_programs(0)
    slot = tile % 2
    sorted_ref = sorted_ref.at[slot]
    sem = sems.at[slot]
    pieces_per_tile = sorted_ref.shape[0] // SEG

    @pl.when(tile >= 2)
    def _():
        _wait_copies(sorted_ref, hs_ref, npiece_s[tile - 2], sem)

    tm = h_ref.shape[0]
    rt = rt_ref[...]
    e1 = rt[0:1].astype(jnp.int32)
    e2 = rt[1:2].astype(jnp.int32)
    rows_e = lax.broadcasted_iota(jnp.int32, (N_EXPERTS, tm), 0)
    ind1 = rows_e == e1
    ind2 = rows_e == e2
    ind = jnp.where(ind1 | ind2, 1.0, 0.0).astype(BF16)
    earlier = jnp.where(lax.broadcasted_iota(jnp.int32, (tm, tm), 0) < lax.broadcasted_iota(jnp.int32, (tm, tm), 1),
                        1.0, 0.0).astype(BF16)
    pos = locc_ref[0] + _dot(ind, earlier)
    pos1 = jnp.sum(jnp.where(ind1, pos, 0.0), axis=0, keepdims=True).astype(jnp.int32)
    pos2 = jnp.sum(jnp.where(ind2, pos, 0.0), axis=0, keepdims=True).astype(jnp.int32)
    row_id = lax.broadcasted_iota(jnp.int32, (sorted_ref.shape[0], tm), 0)
    onehot = jnp.where((row_id == pos1) | (row_id == pos2), 1.0, 0.0).astype(BF16)
    sorted_ref[...] = _dot(onehot, h_ref[...]).astype(BF16)
    count = npiece_s[tile]
    _piece_copies(sorted_ref, hs_ref, row_s, tile, count, sem, True, pieces_per_tile)

    @pl.when(tile == ntiles - 1)
    def _():
        zero_ref[...] = jnp.zeros_like(zero_ref)

        def tail(e, total):
            def piece(b, carry):
                pltpu.make_async_copy(
                    zero_ref.at[pl.ds(0, SEG)],
                    hs_ref.at[pl.ds(pl.multiple_of(tail_s[e] + b * SEG, SEG), SEG)], sems.at[2]).start()
                return carry

            lax.fori_loop(0, tailn_s[e], piece, 0)
            return total + tailn_s[e]

        n_tail = lax.fori_loop(0, N_EXPERTS, tail, 0)

        def block(b, carry):
            pltpu.make_async_copy(zero_ref, hs_ref.at[pl.ds(pl.multiple_of(b * rb, rb), rb)], sems.at[3]).start()
            return carry

        lax.fori_loop(nused_s[0], nb_max, block, 0)
        _wait_copies(zero_ref, hs_ref, n_tail, sems.at[2])

        def block_wait(b, carry):
            pltpu.make_async_copy(zero_ref, hs_ref.at[pl.ds(0, rb)], sems.at[3]).wait()
            return carry

        lax.fori_loop(nused_s[0], nb_max, block_wait, 0)
        _wait_copies(sorted_ref, hs_ref, count, sem)

        @pl.when(tile >= 1)
        def _():
            _wait_copies(sorted_ref, hs_ref, npiece_s[tile - 1], sems.at[1 - slot])


def _ffn_kernel(be_s, nused_s, hs_ref, wg_ref, wu_ref, wd_ref, o_ref, wgb_ref, wub_ref, wdb_ref):
    b = pl.program_id(0)
    used = b < nused_s[0]

    @pl.when(used & ((b == 0) | (be_s[b] != be_s[jnp.maximum(b - 1, 0)])))
    def _():
        wgb_ref[...] = wg_ref[0].astype(BF16)
        wub_ref[...] = wu_ref[0].astype(BF16)
        wdb_ref[...] = wd_ref[0].astype(BF16)

    @pl.when(used)
    def _():
        x = hs_ref[...]
        gate = _dot(x, wgb_ref[...])
        up = _dot(x, wub_ref[...])
        he = (gate * jax.nn.sigmoid(gate) * up).astype(BF16)
        o_ref[...] = _dot(he, wdb_ref[...]).astype(BF16)

    @pl.when(jnp.logical_not(used))
    def _():
        o_ref[...] = jnp.zeros_like(o_ref)


def _combine_kernel(row_s, npiece_s, x1_ref, rc_ref, locr_ref, ho_ref, out_ref, obuf_ref, sems):
    tile = pl.program_id(0)
    ntiles = pl.num_programs(0)
    slot = tile % 2
    tm = x1_ref.shape[0]
    pieces_per_tile = obuf_ref.shape[1] // SEG

    @pl.when(tile == 0)
    def _():
        obuf_ref[...] = jnp.zeros_like(obuf_ref)
        _piece_copies(obuf_ref.at[0], ho_ref, row_s, 0, npiece_s[0], sems.at[0], False, pieces_per_tile)

    @pl.when(tile + 1 < ntiles)
    def _():
        _piece_copies(obuf_ref.at[1 - slot], ho_ref, row_s, tile + 1, npiece_s[tile + 1], sems.at[1 - slot],
                      False, pieces_per_tile)

    obuf_ref = obuf_ref.at[slot]
    rc = rc_ref[...]
    e1 = rc[:, 0:1].astype(jnp.int32)
    e2 = rc[:, 1:2].astype(jnp.int32)
    lane_e = lax.broadcasted_iota(jnp.int32, (tm, LANES), 1)
    ind1 = lane_e == e1
    ind2 = lane_e == e2
    ind = jnp.where(ind1 | ind2, 1.0, 0.0).astype(BF16)
    earlier = jnp.where(lax.broadcasted_iota(jnp.int32, (tm, tm), 1) < lax.broadcasted_iota(jnp.int32, (tm, tm), 0),
                        1.0, 0.0).astype(BF16)
    pos = locr_ref[0] + _dot(earlier, ind)
    pos1 = jnp.sum(jnp.where(ind1, pos, 0.0), axis=1, keepdims=True).astype(jnp.int32)
    pos2 = jnp.sum(jnp.where(ind2, pos, 0.0), axis=1, keepdims=True).astype(jnp.int32)
    row_id = lax.broadcasted_iota(jnp.int32, (tm, obuf_ref.shape[0]), 1)
    pick = jnp.where(row_id == pos1, rc[:, 2:3], jnp.where(row_id == pos2, rc[:, 3:4], 0.0)).astype(BF16)
    _wait_copies(ho_ref, obuf_ref, npiece_s[tile], sems.at[slot])
    out_ref[...] = x1_ref[...] + _dot(pick, obuf_ref[...])


def _moe(h2, x1, rt, cnt, w_gate_e, w_up_e, w_down_e, tm, rb):
    n = h2.shape[0]
    ntiles = n // tm
    j_rows = _slot_rows(tm)
    nb_max = -(-(2 * n + ntiles * N_EXPERTS * (SEG - 1) + N_EXPERTS * (rb - 1)) // rb)
    rmax = nb_max * rb
    loc, row_s, npiece_s, blk_expert, nused, tail_s, tailn_s = _moe_plan(
        cnt[:, :, 0].astype(jnp.int32), rb, nb_max, j_rows // SEG)
    loc_col = loc.astype(F32).reshape(ntiles, N_EXPERTS, 1)
    loc_row = jnp.pad(loc.astype(F32), ((0, 0), (0, LANES - N_EXPERTS))).reshape(ntiles, 1, LANES)
    any_spec = pl.BlockSpec(memory_space=pl.ANY)
    params = pltpu.CompilerParams(dimension_semantics=("arbitrary",), vmem_limit_bytes=VMEM_LIMIT_BYTES)

    hs = pl.pallas_call(
        functools.partial(_dispatch_kernel, nb_max=nb_max, rb=rb),
        grid_spec=pltpu.PrefetchScalarGridSpec(
            num_scalar_prefetch=5, grid=(ntiles,),
            in_specs=[pl.BlockSpec((tm, D_MODEL), lambda i, *_: (i, 0)),
                      pl.BlockSpec((ROUTE_REC, tm), lambda i, *_: (0, i)),
                      pl.BlockSpec((1, N_EXPERTS, 1), lambda i, *_: (i, 0, 0))],
            out_specs=any_spec,
            scratch_shapes=[pltpu.VMEM((2, j_rows, D_MODEL), BF16), pltpu.VMEM((rb, D_MODEL), BF16),
                            pltpu.SemaphoreType.DMA((4,))]),
        out_shape=jax.ShapeDtypeStruct((rmax, D_MODEL), BF16),
        compiler_params=params,
        name="moe_dispatch",
    )(row_s, npiece_s, tail_s, tailn_s, nused, h2, rt, loc_col)

    blk_row = lambda b, be, nu: (jnp.minimum(b, nu[0] - 1), 0)
    ho = pl.pallas_call(
        _ffn_kernel,
        grid_spec=pltpu.PrefetchScalarGridSpec(
            num_scalar_prefetch=2, grid=(nb_max,),
            in_specs=[pl.BlockSpec((rb, D_MODEL), blk_row),
                      pl.BlockSpec((1, D_MODEL, D_EXPERT), lambda b, be, nu: (be[b], 0, 0)),
                      pl.BlockSpec((1, D_MODEL, D_EXPERT), lambda b, be, nu: (be[b], 0, 0)),
                      pl.BlockSpec((1, D_EXPERT, D_MODEL), lambda b, be, nu: (be[b], 0, 0))],
            out_specs=pl.BlockSpec((rb, D_MODEL), lambda b, be, nu: (b, 0)),
            scratch_shapes=[pltpu.VMEM((D_MODEL, D_EXPERT), BF16), pltpu.VMEM((D_MODEL, D_EXPERT), BF16),
                            pltpu.VMEM((D_EXPERT, D_MODEL), BF16)]),
        out_shape=jax.ShapeDtypeStruct((rmax, D_MODEL), BF16),
        compiler_params=params,
        name="moe_ffn",
    )(blk_expert, nused, hs, w_gate_e, w_up_e, w_down_e)

    return pl.pallas_call(
        _combine_kernel,
        grid_spec=pltpu.PrefetchScalarGridSpec(
            num_scalar_prefetch=2, grid=(ntiles,),
            in_specs=[pl.BlockSpec((tm, D_MODEL), lambda i, *_: (i, 0)),
                      pl.BlockSpec((tm, ROUTE_REC), lambda i, *_: (i, 0)),
                      pl.BlockSpec((1, 1, LANES), lambda i, *_: (i, 0, 0)),
                      any_spec],
            out_specs=pl.BlockSpec((tm, D_MODEL), lambda i, *_: (i, 0)),
            scratch_shapes=[pltpu.VMEM((2, j_rows, D_MODEL), BF16), pltpu.SemaphoreType.DMA((2,))]),
        out_shape=jax.ShapeDtypeStruct((n, D_MODEL), F32),
        compiler_params=params,
        name="moe_combine",
    )(row_s, npiece_s, x1, rt.T, loc_row, ho)


def _row_tile(n, want):
    t = min(want, n)
    while n % t:
        t //= 2
    return t


def kernel(x, norm1_g, w_in, mu_prev, mu_next, w0_f, w_up_f, w0_b, w_up_b, a0_f, a_up_f, a0_b, a_up_b, g_up, k_k, k_a, r_k, lnx_g, lnx_b, q_gain, k_gain, rpb, b_gate, w_o_rwkv, w_o_na, w_out, norm2_g, w_router_group, b_router_group, w_router_expert, b_router_expert, w_gate_e, w_up_e, w_down_e):
    batch, seq, d = x.shape
    assert d == D_MODEL and seq % CHUNK == 0 and seq % GRID_W == 0
    n = batch * seq
    for l in range(norm1_g.shape[0]):
        xf = x.reshape(n, d)
        prw, q, k, v, gate = _in_proj(xf, norm1_g[l], w_in[l].astype(BF16), q_gain[l], k_gain[l],
                                      _row_tile(n, 512))
        yf, bonf, g, yb, bonb = _rwkv(prw, batch, seq, mu_prev[l], mu_next[l], w0_f[l], w_up_f[l],
                                      w0_b[l], w_up_b[l], a0_f[l], a_up_f[l], a0_b[l], a_up_b[l],
                                      g_up[l], k_k[l], k_a[l], r_k[l].reshape(-1))
        yna = _natten(q, k, v, rpb[l], batch, seq)
        tm = _row_tile(n, MOE_TM)
        x1, h2, rt, cnt = _out_proj(xf, yf, yb, bonf, bonb, g, yna, gate, lnx_g[l], lnx_b[l], b_gate[l],
                                    w_o_rwkv[l], w_o_na[l], w_out[l], norm2_g[l],
                                    w_router_group[l], b_router_group[l],
                                    w_router_expert[l], b_router_expert[l], tm)
        out = _moe(h2, x1, rt, cnt, w_gate_e[l], w_up_e[l], w_down_e[l], tm, MOE_RB)
        x = out.reshape(batch, seq, d)
    return x
```

```python
import functools

import numpy as np
import jax
import jax.numpy as jnp
from jax import lax
from jax.experimental import pallas as pl
from jax.experimental.pallas import tpu as pltpu

D_MODEL = 1024
HEAD_DIM = 64
RWKV_HEADS = 8
RWKV_WIDTH = RWKV_HEADS * HEAD_DIM
DECAY_RANK = 64
ICLR_RANK = 64
GATE_RANK = 128
NA_HEADS = 8
NA_WIDTH = NA_HEADS * HEAD_DIM
GRID_W = 64
NA_MAX_WIN_ROWS = 8
NA_WIN_COLS = 16
N_GROUPS = 4
EXPERTS_PER_GROUP = 8
N_EXPERTS = N_GROUPS * EXPERTS_PER_GROUP
D_EXPERT = 256
RMS_EPS = 1e-6
GN_EPS = 64e-5
NEG_INF = -1e30
RWKV_COLS = 3 * RWKV_WIDTH + 2 * DECAY_RANK + 2 * ICLR_RANK + GATE_RANK
NA_COLS = 3 * NA_WIDTH
GATE_COLS = 2 * D_MODEL
IN_COLS = RWKV_COLS + NA_COLS + GATE_COLS

LANES = 128
BF16_SUBLANES = 16
VMEM_LIMIT_BYTES = 56 * 1024 * 1024

CHUNK = 64
RWKV_CHUNKS_PER_STEP = 4
PAIR = LANES // HEAD_DIM
MXU_DIM = 256
RWKV_GW = LANES
RWKV_HG = RWKV_GW // HEAD_DIM
RWKV_NG = RWKV_WIDTH // RWKV_GW
EXP_M05 = float(np.exp(-0.5))
NA_ROWS_PER_STEP = 8
SUBLANES = 8
ROUTER_ROWS = SUBLANES + N_EXPERTS
ROUTE_REC = SUBLANES
MOE_TM = 512
MOE_RB = 1024
SEG = BF16_SUBLANES

F32 = jnp.float32
BF16 = jnp.bfloat16


def _dot(a, b):
    return jnp.dot(a, b, preferred_element_type=F32)


def _dot_nt(a, b):
    return lax.dot_general(a, b, (((1,), (1,)), ((), ())), preferred_element_type=F32)


def _dot_tn(a, b):
    return lax.dot_general(a, b, (((0,), (0,)), ((), ())), preferred_element_type=F32)


def _split2(z):
    hi = z.astype(BF16)
    lo = (z - hi.astype(F32)).astype(BF16)
    return hi, lo


def _grouped_dot(zb, bd):
    g = bd.shape[0]
    return jnp.concatenate([_dot(zb[:, j:j + g], bd) for j in range(0, zb.shape[1], g)], axis=1)


def _seg_dot(z, bd):
    hi, lo = _split2(z)
    return _grouped_dot(hi, bd) + _grouped_dot(lo, bd)


def _block_diag_heads(value):
    idx = np.arange(MXU_DIM) // HEAD_DIM
    return jnp.asarray((idx[:, None] == idx[None, :]).astype(np.float32) * value, dtype=BF16)


def _inproj_kernel(x_ref, g1_ref, w_ref, qg_ref, kg_ref, bd_ref,
                   prw_ref, q_ref, k_ref, v_ref, gate_ref):
    x = x_ref[...]
    ms = jnp.mean(x * x, axis=-1, keepdims=True)
    h = (x * lax.rsqrt(ms + RMS_EPS) * g1_ref[...]).astype(BF16)

    def proj(lo, hi):
        return _dot(h, w_ref[:, lo:hi])

    col = 0
    while col < RWKV_COLS:
        nxt = min(col + 512, RWKV_COLS)
        prw_ref[:, col:nxt] = proj(col, nxt).astype(BF16)
        col = nxt

    def head_rms(t, gain):
        msq = _grouped_dot((t * t).astype(BF16), bd_ref[...])
        return t * lax.rsqrt(msq + RMS_EPS) * gain

    c0 = RWKV_COLS
    q_ref[...] = head_rms(proj(c0, c0 + NA_WIDTH), qg_ref[...]).astype(BF16)
    k_ref[...] = head_rms(proj(c0 + NA_WIDTH, c0 + 2 * NA_WIDTH), kg_ref[...]).astype(BF16)
    v_ref[...] = proj(c0 + 2 * NA_WIDTH, c0 + 3 * NA_WIDTH).astype(BF16)
    c1 = c0 + NA_COLS
    for j in range(GATE_COLS // 512):
        gate_ref[:, j * 512:(j + 1) * 512] = proj(c1 + j * 512, c1 + (j + 1) * 512).astype(BF16)


def _in_proj(xf, g1, w_in_b, q_gain, k_gain, tm):
    n = xf.shape[0]
    qg = (jnp.tile(q_gain, NA_HEADS) * (HEAD_DIM ** -0.5)).reshape(1, NA_WIDTH)
    kg = jnp.tile(k_gain, NA_HEADS).reshape(1, NA_WIDTH)
    bd = _block_diag_heads(1.0 / HEAD_DIM)
    const = lambda i: (0, 0)
    row = lambda i: (i, 0)
    return pl.pallas_call(
        _inproj_kernel,
        grid=(n // tm,),
        in_specs=[
            pl.BlockSpec((tm, D_MODEL), row),
            pl.BlockSpec((1, D_MODEL), const),
            pl.BlockSpec((D_MODEL, IN_COLS), const),
            pl.BlockSpec((1, NA_WIDTH), const),
            pl.BlockSpec((1, NA_WIDTH), const),
            pl.BlockSpec((MXU_DIM, MXU_DIM), const),
        ],
        out_specs=[
            pl.BlockSpec((tm, RWKV_COLS), row),
            pl.BlockSpec((tm, NA_WIDTH), row),
            pl.BlockSpec((tm, NA_WIDTH), row),
            pl.BlockSpec((tm, NA_WIDTH), row),
            pl.BlockSpec((tm, GATE_COLS), row),
        ],
        out_shape=[
            jax.ShapeDtypeStruct((n, RWKV_COLS), BF16),
            jax.ShapeDtypeStruct((n, NA_WIDTH), BF16),
            jax.ShapeDtypeStruct((n, NA_WIDTH), BF16),
            jax.ShapeDtypeStruct((n, NA_WIDTH), BF16),
            jax.ShapeDtypeStruct((n, GATE_COLS), BF16),
        ],
        compiler_params=pltpu.CompilerParams(
            dimension_semantics=("parallel",), vmem_limit_bytes=VMEM_LIMIT_BYTES),
        name="in_proj",
    )(xf, g1.reshape(1, D_MODEL), w_in_b, qg, kg, bd)


def _expand(z, lane_head):
    return jnp.concatenate([jnp.where(lane_head == h, z, 0.0) for h in range(RWKV_HG)], axis=0).astype(BF16)


def _level_mask(t_idx, s_idx, b, reverse):
    same = (t_idx // (2 * b)) == (s_idx // (2 * b))
    t_hi = (t_idx % (2 * b)) >= b
    s_hi = (s_idx % (2 * b)) >= b
    if reverse:
        return same & jnp.logical_not(t_hi) & s_hi
    return same & t_hi & jnp.logical_not(s_hi)


def _rwkv_prep(ps, d, prm, with_gate):
    reverse = d == 1
    w = RWKV_WIDTH
    r = ps[:, 0:w]
    k = ps[:, w:2 * w]
    v = ps[:, 2 * w:3 * w]
    c_dec = 3 * w
    c_icl = c_dec + 2 * DECAY_RANK
    c_gate = c_icl + 2 * ICLR_RANK
    bd1 = prm["bd1"][...]

    w_raw = prm["w0"][d:d + 1, :] + _dot(jnp.tanh(ps[:, c_dec:c_icl]).astype(BF16), prm["wup"][d])
    a = jax.nn.sigmoid(prm["a0"][d:d + 1, :] + _dot(ps[:, c_icl:c_gate].astype(BF16), prm["aup"][d]))
    lw = -EXP_M05 * jax.nn.sigmoid(w_raw)
    kk = k * prm["k_k"][...]
    kk = kk * lax.rsqrt(_grouped_dot((kk * kk).astype(BF16), bd1) + 1e-12)
    kd = k * (1.0 + (a - 1.0) * prm["k_a"][...])
    bonus = _seg_dot(r * kd * prm["r_k"][...], bd1) * v

    nrow = ps.shape[0]
    rowc = lax.broadcasted_iota(jnp.int32, (nrow, nrow), 0)
    colc = lax.broadcasted_iota(jnp.int32, (nrow, nrow), 1)
    in_chunk = (rowc // CHUNK) == (colc // CHUNK)
    tri = jnp.where(in_chunk & ((colc >= rowc) if reverse else (colc <= rowc)), 1.0, 0.0).astype(BF16)
    lw_hi = lw.astype(BF16)
    rem = lw - lw_hi.astype(F32)
    lw_mid = rem.astype(BF16)
    lw_lo = (rem - lw_mid.astype(F32)).astype(BF16)
    cum = _dot(tri, lw_hi) + _dot(tri, lw_mid) + _dot(tri, lw_lo)
    tots =[jnp.sum(lw[c * CHUNK:(c + 1) * CHUNK], axis=0, keepdims=True) for c in range(nrow // CHUNK)]
    tot = jnp.concatenate([jnp.broadcast_to(t, (CHUNK, w)) for t in tots], axis=0)
    e_rest = jnp.exp(tot - cum)
    e_neg = jnp.exp(-cum)
    beta = kk * a
    out = dict(rt=r * jnp.exp(cum), at=-kk * jnp.exp(cum - lw), kh=kd * e_neg, bh=beta * e_neg,
               kw=kd * e_rest, bw=beta * e_rest, v=v, wc=[jnp.exp(t) for t in tots], bonus=bonus)
    if with_gate:
        out["gate"] = _dot(jax.nn.sigmoid(ps[:, c_gate:c_gate + GATE_RANK]).astype(BF16), prm["gup"][...])
    return out


def _rwkv_chains(preps, dirs, s_ref):
    gw = RWKV_GW
    m0 = lax.broadcasted_iota(jnp.int32, (1, gw), 1) // HEAD_DIM
    t_idx = lax.broadcasted_iota(jnp.int32, (CHUNK, gw), 0)
    s_idx = lax.broadcasted_iota(jnp.int32, (CHUNK, gw), 1) % CHUNK
    eye = jnp.where(s_idx == t_idx, 1.0, 0.0)
    strict = [s_idx < t_idx, s_idx > t_idx]
    incl = [s_idx <= t_idx, s_idx >= t_idx]
    bd_r = lax.broadcasted_iota(jnp.int32, (gw, gw), 0) // HEAD_DIM
    bd_c = lax.broadcasted_iota(jnp.int32, (gw, gw), 1) // HEAD_DIM
    bdmask = bd_r == bd_c

    ndir = len(preps)
    nck = preps[0]["v"].shape[0] // CHUNK
    chains = [(d, ck, pr) for ck in range(nck) for pr in range(RWKV_NG) for d in range(ndir)]

    def part(c, name):
        d, ck, pr = c
        return preps[d][name][ck * CHUNK:(ck + 1) * CHUNK, pr * gw:(pr + 1) * gw]

    ab, ak = {}, {}
    for c in chains:
        x = jnp.concatenate([part(c, "at"), part(c, "rt")], axis=0).astype(BF16)
        abk = _dot_nt(x, jnp.concatenate([_expand(part(c, "bh"), m0), _expand(part(c, "kh"), m0)], axis=0))
        ab[c], ak[c] = abk[:, :RWKV_HG * CHUNK], abk[:, RWKV_HG * CHUNK:]
    nmat, nb, a_rb, av, tinv = {}, {}, {}, {}, {}
    for c in chains:
        d = dirs[c[0]]
        nmat[c] = jnp.where(strict[d], ab[c][:CHUNK], 0.0)
        nb[c] = nmat[c].astype(BF16)
        a_rb[c] = jnp.where(incl[d], ab[c][CHUNK:], 0.0).astype(BF16)
        a_k = jnp.concatenate([jnp.where(strict[d], ak[c][:CHUNK], 0.0),
                               jnp.where(incl[d], ak[c][CHUNK:], 0.0)], axis=0).astype(BF16)
        av[c] = _dot(a_k, _expand(part(c, "v"), m0))
        tinv[c] = eye + jnp.where(_level_mask(t_idx, s_idx, 1, d == 1), nmat[c], 0.0)
    b = 2
    while b < CHUNK:
        e = {c: _dot(nb[c], _expand(tinv[c], m0)) for c in chains}
        for c in chains:
            f = _dot(tinv[c].astype(BF16), _expand(e[c], m0))
            tinv[c] = tinv[c] + jnp.where(_level_mask(t_idx, s_idx, b, dirs[c[0]] == 1), f, 0.0)
        b *= 2
    tu = {c: _dot(tinv[c].astype(BF16),
                  jnp.concatenate([_expand(part(c, "at"), m0), _expand(av[c][:CHUNK], m0)], axis=1))
          for c in chains}
    qy = {c: _dot(a_rb[c], jnp.concatenate([_expand(tu[c][:, :gw], m0), _expand(tu[c][:, gw:], m0)], axis=1))
          for c in chains}
    mt, bt = {}, {}
    for c in chains:
        bw = part(c, "bw").astype(BF16)
        ut_bw = _dot_tn(tu[c][:, :gw].astype(BF16), bw)
        ulv = jnp.concatenate([tu[c][:, gw:], part(c, "v")], axis=0).astype(BF16)
        ulv_bk = _dot_tn(ulv, jnp.concatenate([bw, part(c, "kw").astype(BF16)], axis=0))
        mt[c] = jnp.where(bdmask, ut_bw, 0.0).astype(BF16)
        bt[c] = jnp.where(bdmask, ulv_bk, 0.0)
    state = {(d, pr): s_ref[dirs[d], pr] for d in range(ndir) for pr in range(RWKV_NG)}
    ys = {}
    for step in range(nck):
        for pr in range(RWKV_NG):
            for d in range(ndir):
                c = (d, step if dirs[d] == 0 else nck - 1 - step, pr)
                s0 = state[(d, pr)]
                s0b = s0.astype(BF16)
                q = part(c, "rt") + qy[c][:, :gw]
                ys[c] = _dot_nt(q.astype(BF16), s0b) + qy[c][:, gw:] + av[c][CHUNK:]
                wc = preps[d]["wc"][c[1]][:, pr * gw:(pr + 1) * gw]
                state[(d, pr)] = s0 * wc + _dot(s0b, mt[c]) + bt[c]
    for (d, pr), s_new in state.items():
        s_ref[dirs[d], pr] = s_new
    return [jnp.concatenate([jnp.concatenate([ys[(d, ck, pr)] for pr in range(RWKV_NG)], axis=1)
                             for ck in range(nck)], axis=0) for d in range(ndir)]


def _rwkv_kernel(pf_ref, pfp_ref, pfn_ref, pb_ref, pbp_ref, pbn_ref,
                 mup_ref, mun_ref, w0_ref, wup_ref, a0_ref, aup_ref, gup_ref,
                 kk_ref, ka_ref, rk_ref, bd1_ref,
                 yf_ref, bonf_ref, g_ref, yb_ref, bonb_ref, s_ref, *, nc):
    i = pl.program_id(1)

    @pl.when(i == 0)
    def _():
        s_ref[...] = jnp.zeros_like(s_ref)

    prm = dict(w0=w0_ref, wup=wup_ref, a0=a0_ref, aup=aup_ref, gup=gup_ref,
               k_k=kk_ref, k_a=ka_ref, r_k=rk_ref, bd1=bd1_ref)
    nrow = pf_ref.shape[0]
    rows = lax.broadcasted_iota(jnp.int32, (nrow, nrow), 0)
    cols = lax.broadcasted_iota(jnp.int32, (nrow, nrow), 1)
    take_prev = jnp.where(cols == rows - 1, 1.0, 0.0).astype(BF16)
    take_next = jnp.where(cols == rows + 1, 1.0, 0.0).astype(BF16)
    sub = lax.broadcasted_iota(jnp.int32, (SUBLANES, 1), 0)

    def shifted(main_ref, prev_ref, next_ref, c):
        pm = main_ref[...]
        p = pm.astype(F32)
        edge_prev = prev_ref[...].astype(F32)[BF16_SUBLANES - 1:, :] * jnp.where(c > 0, 1.0, 0.0)
        edge_next = next_ref[...].astype(F32)[:1, :] * jnp.where(c < nc - 1, 1.0, 0.0)
        p_prev = _dot(take_prev, pm)
        p_next = _dot(take_next, pm)
        p_prev = jnp.concatenate([jnp.where(sub == 0, edge_prev, p_prev[:SUBLANES]), p_prev[SUBLANES:]], axis=0)
        p_next = jnp.concatenate([p_next[:nrow - SUBLANES],
                                  jnp.where(sub == SUBLANES - 1, edge_next, p_next[nrow - SUBLANES:])], axis=0)
        return p + mup_ref[...] * (p_prev - p) + mun_ref[...] * (p_next - p)

    preps = [_rwkv_prep(shifted(pf_ref, pfp_ref, pfn_ref, i), 0, prm, True),
             _rwkv_prep(shifted(pb_ref, pbp_ref, pbn_ref, nc - 1 - i), 1, prm, False)]
    y_f, y_b = _rwkv_chains(preps, [0, 1], s_ref)
    yf_ref[...] = y_f
    bonf_ref[...] = preps[0]["bonus"].astype(BF16)
    g_ref[...] = preps[0]["gate"].astype(BF16)
    yb_ref[...] = y_b
    bonb_ref[...] = preps[1]["bonus"].astype(BF16)


def _rwkv(prw, batch, seq, mu_prev, mu_next, w0_f, w_up_f, w0_b, w_up_b, a0_f, a_up_f, a0_b, a_up_b,
          g_up, k_k, k_a, r_k):
    n = batch * seq
    cps = RWKV_CHUNKS_PER_STEP if seq % (CHUNK * RWKV_CHUNKS_PER_STEP) == 0 else 1
    nrow = cps * CHUNK
    nc = seq // nrow
    sub = nrow // BF16_SUBLANES
    nsub = n // BF16_SUBLANES
    w = RWKV_WIDTH
    zd = jnp.zeros((DECAY_RANK, w), F32)
    zi = jnp.zeros((ICLR_RANK, w), F32)
    wup = jnp.stack([jnp.concatenate([w_up_f, zd], 0), jnp.concatenate([zd, w_up_b], 0)]).astype(BF16)
    aup = jnp.stack([jnp.concatenate([a_up_f, zi], 0), jnp.concatenate([zi, a_up_b], 0)]).astype(BF16)
    w0 = jnp.stack([w0_f, w0_b])
    a0 = jnp.stack([a0_f, a0_b])
    bd1 = _block_diag_heads(1.0)

    def fwd_c(b, i):
        return i

    def bwd_c(b, i):
        return nc - 1 - i

    def main(cf):
        return lambda b, i: (b * nc + cf(b, i), 0)

    def prev(cf):
        return lambda b, i: (jnp.maximum((b * nc + cf(b, i)) * sub - 1, 0), 0)

    def nxt(cf):
        return lambda b, i: (jnp.minimum((b * nc + cf(b, i)) * sub + sub, nsub - 1), 0)

    const2 = lambda b, i: (0, 0)
    const3 = lambda b, i: (0, 0, 0)
    pspec = lambda f: pl.BlockSpec((nrow, RWKV_COLS), f)
    nspec = lambda f: pl.BlockSpec((BF16_SUBLANES, RWKV_COLS), f)
    ospec = lambda f: pl.BlockSpec((nrow, w), f)
    vec = pl.BlockSpec((1, w), const2)
    out_sds = jax.ShapeDtypeStruct((n, w), F32)
    out_bf = jax.ShapeDtypeStruct((n, w), BF16)
    return pl.pallas_call(
        functools.partial(_rwkv_kernel, nc=nc),
        grid=(batch, nc),
        in_specs=[
            pspec(main(fwd_c)), nspec(prev(fwd_c)), nspec(nxt(fwd_c)),
            pspec(main(bwd_c)), nspec(prev(bwd_c)), nspec(nxt(bwd_c)),
            pl.BlockSpec((1, RWKV_COLS), const2), pl.BlockSpec((1, RWKV_COLS), const2),
            pl.BlockSpec((2, w), const2), pl.BlockSpec((2, 2 * DECAY_RANK, w), const3),
            pl.BlockSpec((2, w), const2), pl.BlockSpec((2, 2 * ICLR_RANK, w), const3),
            pl.BlockSpec((GATE_RANK, w), const2),
            vec, vec, vec,
            pl.BlockSpec((MXU_DIM, MXU_DIM), const2),
        ],
        out_specs=[ospec(main(fwd_c)), ospec(main(fwd_c)), ospec(main(fwd_c)),
                   ospec(main(bwd_c)), ospec(main(bwd_c))],
        out_shape=[out_sds, out_bf, out_bf, out_sds, out_bf],
        scratch_shapes=[pltpu.VMEM((2, RWKV_NG, RWKV_GW, RWKV_GW), F32)],
        compiler_params=pltpu.CompilerParams(
            dimension_semantics=("arbitrary", "arbitrary"), vmem_limit_bytes=VMEM_LIMIT_BYTES),
        name="rwkv7",
    )(prw, prw, prw, prw, prw, prw,
      mu_prev.reshape(1, RWKV_COLS), mu_next.reshape(1, RWKV_COLS), w0, wup, a0, aup,
      g_up.astype(BF16), k_k.reshape(1, w), k_a.reshape(1, w), r_k.reshape(1, w), bd1)


def _na_kernel(q_ref, k_ref, v_ref, bias_ref, o_ref, *, rows, kh, rblk):
    j = pl.program_id(1)
    lane = lax.broadcasted_iota(jnp.int32, (1, LANES), 1)
    m0 = lane < HEAD_DIM
    zero = jnp.zeros((), BF16)
    npairs = NA_HEADS // PAIR
    chains = [(r, pr) for r in range(rblk) for pr in range(npairs)]
    starts, deltas = [], []
    for r in range(rblk):
        i = j * rblk + r
        start = jnp.clip(i - kh // 2, 0, rows - kh)
        starts.append(pl.multiple_of(start * GRID_W, GRID_W))
        deltas.append(i - start)
    s = {}
    for (r, pr) in chains:
        sl = slice(pr * LANES, (pr + 1) * LANES)
        qp = q_ref[r * GRID_W:(r + 1) * GRID_W, sl]
        qs = jnp.concatenate([jnp.where(m0, qp, zero), jnp.where(m0, zero, qp)], axis=0)
        kb = k_ref[pl.ds(starts[r], kh * GRID_W), sl]
        s[(r, pr)] = _dot_nt(qs, kb) + bias_ref[deltas[r], pr]
    e, l = {}, {}
    for c in chains:
        m = jnp.max(s[c], axis=-1, keepdims=True)
        p = jnp.exp(s[c] - m)
        l[c] = jnp.sum(p, axis=-1, keepdims=True)
        e[c] = p.astype(BF16)
    for r in range(rblk):
        outs = []
        for pr in range(npairs):
            sl = slice(pr * LANES, (pr + 1) * LANES)
            vb = v_ref[pl.ds(starts[r], kh * GRID_W), sl]
            pv = _dot(e[(r, pr)], vb) / l[(r, pr)]
            outs.append(jnp.where(m0, pv[:GRID_W], pv[GRID_W:]))
        o_ref[r * GRID_W:(r + 1) * GRID_W, :] = jnp.concatenate(outs, axis=1).astype(BF16)


def _na_bias_table(rpb, kh):
    cols = np.arange(GRID_W)
    col_start = np.clip(cols - NA_WIN_COLS // 2, 0, GRID_W - NA_WIN_COLS)
    col_mask = (cols[None, :] >= col_start[:, None]) & (cols[None, :] < col_start[:, None] + NA_WIN_COLS)
    col_off = np.clip(cols[None, :] - cols[:, None] + NA_WIN_COLS - 1, 0, 2 * NA_WIN_COLS - 2)
    delta = np.arange(kh)
    row_off = np.arange(kh)[None, :] - delta[:, None] + NA_MAX_WIN_ROWS - 1
    col_sel = (col_off[None] == np.arange(2 * NA_WIN_COLS - 1)[:, None, None]).astype(np.float32)
    row_sel = (row_off[:, :, None] == np.arange(2 * NA_MAX_WIN_ROWS - 1)).astype(np.float32)
    hi = lax.Precision.HIGHEST
    t = jnp.einsum("hoc,cqk->hoqk", rpb.astype(F32), col_sel, precision=hi)
    t = jnp.einsum("dro,hoqk->dhqrk", row_sel, t, precision=hi)
    t = jnp.where(col_mask[None, None, :, None, :], t, NEG_INF)
    return t.reshape(kh, NA_HEADS // PAIR, PAIR * GRID_W, kh * GRID_W)


def _natten(q, k, v, rpb, batch, seq):
    n = batch * seq
    rows = seq // GRID_W
    kh = min(NA_MAX_WIN_ROWS, rows)
    bias = _na_bias_table(rpb, kh)
    rblk = NA_ROWS_PER_STEP if rows % NA_ROWS_PER_STEP == 0 else 1
    nblk = rows // rblk

    return pl.pallas_call(
        functools.partial(_na_kernel, rows=rows, kh=kh, rblk=rblk),
        grid=(batch, nblk),
        in_specs=[
            pl.BlockSpec((rblk * GRID_W, NA_WIDTH), lambda b, i: (b * nblk + i, 0)),
            pl.BlockSpec((seq, NA_WIDTH), lambda b, i: (b, 0)),
            pl.BlockSpec((seq, NA_WIDTH), lambda b, i: (b, 0)),
            pl.BlockSpec((kh, NA_HEADS // PAIR, PAIR * GRID_W, kh * GRID_W), lambda b, i: (0, 0, 0, 0),
                         pipeline_mode=pl.Buffered(1)),
        ],
        out_specs=pl.BlockSpec((rblk * GRID_W, NA_WIDTH), lambda b, i: (b * nblk + i, 0)),
        out_shape=jax.ShapeDtypeStruct((n, NA_WIDTH), BF16),
        compiler_params=pltpu.CompilerParams(
            dimension_semantics=("parallel", "arbitrary"), vmem_limit_bytes=VMEM_LIMIT_BYTES),
        name="natten2d",
    )(q, k, v, bias)


def _route_t(logits_t):
    tm = logits_t.shape[1]
    sub = lax.broadcasted_iota(jnp.int32, (SUBLANES, tm), 0)
    ninf = -jnp.inf
    gvalid = sub < N_GROUPS
    gl = jnp.where(gvalid, logits_t[0:SUBLANES], ninf)
    gmax = jnp.max(gl, axis=0, keepdims=True)
    g_sel = jnp.min(jnp.where(gl == gmax, sub, SUBLANES), axis=0, keepdims=True)
    p_group = 1.0 / jnp.sum(jnp.where(gvalid, jnp.exp(gl - gmax), 0.0), axis=0, keepdims=True)
    ev = logits_t[SUBLANES:2 * SUBLANES]
    for g in range(1, N_GROUPS):
        lo = SUBLANES + g * EXPERTS_PER_GROUP
        ev = jnp.where(g_sel == g, logits_t[lo:lo + EXPERTS_PER_GROUP], ev)
    m1 = jnp.max(ev, axis=0, keepdims=True)
    i1 = jnp.min(jnp.where(ev == m1, sub, SUBLANES), axis=0, keepdims=True)
    ev2 = jnp.where(sub == i1, ninf, ev)
    m2 = jnp.max(ev2, axis=0, keepdims=True)
    i2 = jnp.min(jnp.where(ev2 == m2, sub, SUBLANES), axis=0, keepdims=True)
    t = jnp.exp(m2 - m1)
    w1 = p_group / (1.0 + t)
    w2 = p_group * t / (1.0 + t)
    e1 = g_sel * EXPERTS_PER_GROUP + i1
    e2 = g_sel * EXPERTS_PER_GROUP + i2
    rec = jnp.where(sub == 0, e1.astype(F32),
                    jnp.where(sub == 1, e2.astype(F32),
                              jnp.where(sub == 2, w1, jnp.where(sub == 3, w2, 0.0))))
    return rec, e1, e2


def _out_kernel(x_ref, yf_ref, yb_ref, bonf_ref, bonb_ref, g_ref, yna_ref, gate_ref,
                lng_ref, lnb_ref, bgate_ref, worw_ref, wona_ref, wout_ref, n2g_ref, wrh_ref, wrl_ref, br_ref,
                bd_ref, x1_ref, h2_ref, rt_ref, cnt_ref):
    bd = bd_ref[...]
    y = yf_ref[...] + yb_ref[...]
    yc = y - _seg_dot(y, bd)
    var = _seg_dot(yc * yc, bd)
    yn = yc * lax.rsqrt(var + GN_EPS) * lng_ref[...] + lnb_ref[...]
    ya = (yn + bonf_ref[...].astype(F32) + bonb_ref[...].astype(F32)) * g_ref[...].astype(F32)
    ya_o = _dot(ya.astype(BF16), worw_ref[...])
    yb_o = _dot(yna_ref[...], wona_ref[...])
    gl = gate_ref[...].astype(F32) + bgate_ref[...]
    mix = jax.nn.sigmoid(gl[:, :D_MODEL]) * ya_o + jax.nn.sigmoid(gl[:, D_MODEL:]) * yb_o
    x1 = x_ref[...] + _dot(mix.astype(BF16), wout_ref[...])
    x1_ref[...] = x1
    ms = jnp.mean(x1 * x1, axis=-1, keepdims=True)
    h2 = x1 * lax.rsqrt(ms + RMS_EPS) * n2g_ref[...]
    h2_hi, h2_lo = _split2(h2)
    h2_ref[...] = h2_hi
    logits_t = (_dot_nt(wrh_ref[...], h2_hi) + _dot_nt(wrh_ref[...], h2_lo)
                + _dot_nt(wrl_ref[...], h2_hi) + br_ref[...])
    rec, e1, e2 = _route_t(logits_t)
    rt_ref[...] = rec
    rows_e = lax.broadcasted_iota(jnp.int32, (N_EXPERTS, rec.shape[1]), 0)
    chosen = jnp.where((rows_e == e1) | (rows_e == e2), 1.0, 0.0)
    cnt_ref[0] = jnp.broadcast_to(jnp.sum(chosen, axis=1, keepdims=True), (N_EXPERTS, LANES))


def _out_proj(xf, yf, yb, bonf, bonb, g, yna, gate, lnx_g, lnx_b, b_gate, w_o_rwkv, w_o_na, w_out,
              norm2_g, w_rg, b_rg, w_re, b_re, tm):
    n = xf.shape[0]
    w = RWKV_WIDTH
    assert EXPERTS_PER_GROUP == SUBLANES and N_GROUPS <= SUBLANES
    gpad = SUBLANES - N_GROUPS
    wr_t = jnp.concatenate([w_rg.T, jnp.zeros((gpad, D_MODEL), F32), w_re.T], axis=0)
    wr_hi = wr_t.astype(BF16)
    wr_lo = (wr_t - wr_hi.astype(F32)).astype(BF16)
    br = jnp.concatenate([b_rg, jnp.zeros((gpad,), F32), b_re]).reshape(ROUTER_ROWS, 1)
    bd = _block_diag_heads(1.0 / HEAD_DIM)
    const = lambda i: (0, 0)
    row = lambda i: (i, 0)
    rs = lambda c: pl.BlockSpec((tm, c), row)
    cs = lambda r, c: pl.BlockSpec((r, c), const)
    return pl.pallas_call(
        _out_kernel,
        grid=(n // tm,),
        in_specs=[rs(D_MODEL), rs(w), rs(w), rs(w), rs(w), rs(w), rs(NA_WIDTH), rs(GATE_COLS),
                  cs(1, w), cs(1, w), cs(1, GATE_COLS), cs(w, D_MODEL), cs(NA_WIDTH, D_MODEL),
                  cs(D_MODEL, D_MODEL), cs(1, D_MODEL), cs(ROUTER_ROWS, D_MODEL), cs(ROUTER_ROWS, D_MODEL),
                  cs(ROUTER_ROWS, 1), cs(MXU_DIM, MXU_DIM)],
        out_specs=[rs(D_MODEL), rs(D_MODEL), pl.BlockSpec((ROUTE_REC, tm), lambda i: (0, i)),
                   pl.BlockSpec((1, N_EXPERTS, LANES), lambda i: (i, 0, 0))],
        out_shape=[jax.ShapeDtypeStruct((n, D_MODEL), F32),
                   jax.ShapeDtypeStruct((n, D_MODEL), BF16),
                   jax.ShapeDtypeStruct((ROUTE_REC, n), F32),
                   jax.ShapeDtypeStruct((n // tm, N_EXPERTS, LANES), F32)],
        compiler_params=pltpu.CompilerParams(
            dimension_semantics=("parallel",), vmem_limit_bytes=VMEM_LIMIT_BYTES),
        name="out_proj",
    )(xf, yf, yb, bonf, bonb, g, yna, gate,
      lnx_g.reshape(1, w), lnx_b.reshape(1, w), b_gate.reshape(1, GATE_COLS),
      w_o_rwkv.astype(BF16), w_o_na.astype(BF16), w_out.astype(BF16),
      norm2_g.reshape(1, D_MODEL), wr_hi, wr_lo, br, bd)


def _slot_rows(tm):
    rows = 2 * tm + N_EXPERTS * (SEG - 1)
    return -(-rows // LANES) * LANES


def _moe_plan(cnt, rb, nb_max, pieces_per_tile):
    ntiles = cnt.shape[0]
    e_before = jnp.asarray(np.tril(np.ones((N_EXPERTS, N_EXPERTS), np.int32), -1))
    t_before = jnp.asarray(np.tril(np.ones((ntiles, ntiles), np.int32), -1))
    pc = ((cnt + SEG - 1) // SEG) * SEG
    loc = jnp.sum(pc[:, None, :] * e_before[None], axis=2)
    within = jnp.sum(t_before[:, :, None] * pc[None], axis=1)
    filled = jnp.sum(pc, axis=0)
    region = ((filled + rb - 1) // rb) * rb
    region_off = jnp.sum(region[None, :] * e_before, axis=1)
    gbase = region_off[None, :] + within
    blk_end = (region_off + region) // rb
    nused = blk_end[-1:]
    blk = jnp.minimum(jnp.arange(nb_max, dtype=jnp.int32), nused - 1)
    blk_expert = jnp.minimum(jnp.sum((blk_end[None, :] <= blk[:, None]).astype(jnp.int32), axis=1),
                             N_EXPERTS - 1)
    n16 = pc // SEG
    end16 = (loc + pc) // SEG
    piece = jnp.arange(pieces_per_tile, dtype=jnp.int32)
    seg_of_piece = jnp.minimum(jnp.sum((end16[:, None, :] <= piece[None, :, None]).astype(jnp.int32), axis=2),
                               N_EXPERTS - 1)
    onehot = (seg_of_piece[:, :, None] == jnp.arange(N_EXPERTS)[None, None, :]).astype(jnp.int32)
    piece_row = jnp.sum(onehot * (gbase - loc)[:, None, :], axis=2) + piece[None, :] * SEG
    i32 = lambda a: a.reshape(-1).astype(jnp.int32)
    return (loc, i32(piece_row), i32(jnp.sum(n16, axis=1)), i32(blk_expert), i32(nused),
            i32(region_off + filled), i32((region - filled) // SEG))


def _piece_copies(local_ref, global_ref, row_s, tile, count, sem, outbound, pieces_per_tile):
    def one(p, carry):
        loc_rows = local_ref.at[pl.ds(pl.multiple_of(p * SEG, SEG), SEG)]
        glob_rows = global_ref.at[pl.ds(pl.multiple_of(row_s[tile * pieces_per_tile + p], SEG), SEG)]
        if outbound:
            pltpu.make_async_copy(loc_rows, glob_rows, sem).start()
        else:
            pltpu.make_async_copy(glob_rows, loc_rows, sem).start()
        return carry

    lax.fori_loop(0, count, one, 0)


def _wait_copies(src_ref, dst_ref, count, sem):
    def one(b, carry):
        pltpu.make_async_copy(src_ref.at[pl.ds(0, SEG)], dst_ref.at[pl.ds(0, SEG)], sem).wait()
        return carry

    lax.fori_loop(0, count, one, 0)


def _dispatch_kernel(row_s, npiece_s, tail_s, tailn_s, nused_s, h_ref, rt_ref, locc_ref, hs_ref,
                     sorted_ref, zero_ref, sems, *, nb_max, rb):
    tile = pl.program_id(0)
    ntiles = pl.num_programs(0)
    slot = tile % 2
    sorted_ref = sorted_ref.at[slot]
    sem = sems.at[slot]
    pieces_per_tile = sorted_ref.shape[0] // SEG

    @pl.when(tile >= 2)
    def _():
        _wait_copies(sorted_ref, hs_ref, npiece_s[tile - 2], sem)

    tm = h_ref.shape[0]
    rt = rt_ref[...]
    e1 = rt[0:1].astype(jnp.int32)
    e2 = rt[1:2].astype(jnp.int32)
    rows_e = lax.broadcasted_iota(jnp.int32, (N_EXPERTS, tm), 0)
    ind1 = rows_e == e1
    ind2 = rows_e == e2
    ind = jnp.where(ind1 | ind2, 1.0, 0.0).astype(BF16)
    earlier = jnp.where(lax.broadcasted_iota(jnp.int32, (tm, tm), 0) < lax.broadcasted_iota(jnp.int32, (tm, tm), 1),
                        1.0, 0.0).astype(BF16)
    pos = locc_ref[0] + _dot(ind, earlier)
    pos1 = jnp.sum(jnp.where(ind1, pos, 0.0), axis=0, keepdims=True).astype(jnp.int32)
    pos2 = jnp.sum(jnp.where(ind2, pos, 0.0), axis=0, keepdims=True).astype(jnp.int32)
    row_id = lax.broadcasted_iota(jnp.int32, (sorted_ref.shape[0], tm), 0)
    onehot = jnp.where((row_id == pos1) | (row_id == pos2), 1.0, 0.0).astype(BF16)
    sorted_ref[...] = _dot(onehot, h_ref[...]).astype(BF16)
    count = npiece_s[tile]
    _piece_copies(sorted_ref, hs_ref, row_s, tile, count, sem, True, pieces_per_tile)

    @pl.when(tile == ntiles - 1)
    def _():
        zero_ref[...] = jnp.zeros_like(zero_ref)

        def tail(e, total):
            def piece(b, carry):
                pltpu.make_async_copy(
                    zero_ref.at[pl.ds(0, SEG)],
                    hs_ref.at[pl.ds(pl.multiple_of(tail_s[e] + b * SEG, SEG), SEG)], sems.at[2]).start()
                return carry

            lax.fori_loop(0, tailn_s[e], piece, 0)
            return total + tailn_s[e]

        n_tail = lax.fori_loop(0, N_EXPERTS, tail, 0)

        def block(b, carry):
            pltpu.make_async_copy(zero_ref, hs_ref.at[pl.ds(pl.multiple_of(b * rb, rb), rb)], sems.at[3]).start()
            return carry

        lax.fori_loop(nused_s[0], nb_max, block, 0)
        _wait_copies(zero_ref, hs_ref, n_tail, sems.at[2])

        def block_wait(b, carry):
            pltpu.make_async_copy(zero_ref, hs_ref.at[pl.ds(0, rb)], sems.at[3]).wait()
            return carry

        lax.fori_loop(nused_s[0], nb_max, block_wait, 0)
        _wait_copies(sorted_ref, hs_ref, count, sem)

        @pl.when(tile >= 1)
        def _():
            _wait_copies(sorted_ref, hs_ref, npiece_s[tile - 1], sems.at[1 - slot])


def _ffn_kernel(be_s, nused_s, hs_ref, wg_ref, wu_ref, wd_ref, o_ref, wgb_ref, wub_ref, wdb_ref):
    b = pl.program_id(0)
    used = b < nused_s[0]

    @pl.when(used & ((b == 0) | (be_s[b] != be_s[jnp.maximum(b - 1, 0)])))
    def _():
        wgb_ref[...] = wg_ref[0].astype(BF16)
        wub_ref[...] = wu_ref[0].astype(BF16)
        wdb_ref[...] = wd_ref[0].astype(BF16)

    @pl.when(used)
    def _():
        x = hs_ref[...]
        gate = _dot(x, wgb_ref[...])
        up = _dot(x, wub_ref[...])
        he = (gate * jax.nn.sigmoid(gate) * up).astype(BF16)
        o_ref[...] = _dot(he, wdb_ref[...]).astype(BF16)

    @pl.when(jnp.logical_not(used))
    def _():
        o_ref[...] = jnp.zeros_like(o_ref)


def _combine_kernel(row_s, npiece_s, x1_ref, rc_ref, locr_ref, ho_ref, out_ref, obuf_ref, sems):
    tile = pl.program_id(0)
    ntiles = pl.num_programs(0)
    slot = tile % 2
    tm = x1_ref.shape[0]
    pieces_per_tile = obuf_ref.shape[1] // SEG

    @pl.when(tile == 0)
    def _():
        obuf_ref[...] = jnp.zeros_like(obuf_ref)
        _piece_copies(obuf_ref.at[0], ho_ref, row_s, 0, npiece_s[0], sems.at[0], False, pieces_per_tile)

    @pl.when(tile + 1 < ntiles)
    def _():
        _piece_copies(obuf_ref.at[1 - slot], ho_ref, row_s, tile + 1, npiece_s[tile + 1], sems.at[1 - slot],
                      False, pieces_per_tile)

    obuf_ref = obuf_ref.at[slot]
    rc = rc_ref[...]
    e1 = rc[:, 0:1].astype(jnp.int32)
    e2 = rc[:, 1:2].astype(jnp.int32)
    lane_e = lax.broadcasted_iota(jnp.int32, (tm, LANES), 1)
    ind1 = lane_e == e1
    ind2 = lane_e == e2
    ind = jnp.where(ind1 | ind2, 1.0, 0.0).astype(BF16)
    earlier = jnp.where(lax.broadcasted_iota(jnp.int32, (tm, tm), 1) < lax.broadcasted_iota(jnp.int32, (tm, tm), 0),
                        1.0, 0.0).astype(BF16)
    pos = locr_ref[0] + _dot(earlier, ind)
    pos1 = jnp.sum(jnp.where(ind1, pos, 0.0), axis=1, keepdims=True).astype(jnp.int32)
    pos2 = jnp.sum(jnp.where(ind2, pos, 0.0), axis=1, keepdims=True).astype(jnp.int32)
    row_id = lax.broadcasted_iota(jnp.int32, (tm, obuf_ref.shape[0]), 1)
    pick = jnp.where(row_id == pos1, rc[:, 2:3], jnp.where(row_id == pos2, rc[:, 3:4], 0.0)).astype(BF16)
    _wait_copies(ho_ref, obuf_ref, npiece_s[tile], sems.at[slot])
    out_ref[...] = x1_ref[...] + _dot(pick, obuf_ref[...])


def _moe(h2, x1, rt, cnt, w_gate_e, w_up_e, w_down_e, tm, rb):
    n = h2.shape[0]
    ntiles = n // tm
    j_rows = _slot_rows(tm)
    nb_max = -(-(2 * n + ntiles * N_EXPERTS * (SEG - 1) + N_EXPERTS * (rb - 1)) // rb)
    rmax = nb_max * rb
    loc, row_s, npiece_s, blk_expert, nused, tail_s, tailn_s = _moe_plan(
        cnt[:, :, 0].astype(jnp.int32), rb, nb_max, j_rows // SEG)
    loc_col = loc.astype(F32).reshape(ntiles, N_EXPERTS, 1)
    loc_row = jnp.pad(loc.astype(F32), ((0, 0), (0, LANES - N_EXPERTS))).reshape(ntiles, 1, LANES)
    any_spec = pl.BlockSpec(memory_space=pl.ANY)
    params = pltpu.CompilerParams(dimension_semantics=("arbitrary",), vmem_limit_bytes=VMEM_LIMIT_BYTES)

    hs = pl.pallas_call(
        functools.partial(_dispatch_kernel, nb_max=nb_max, rb=rb),
        grid_spec=pltpu.PrefetchScalarGridSpec(
            num_scalar_prefetch=5, grid=(ntiles,),
            in_specs=[pl.BlockSpec((tm, D_MODEL), lambda i, *_: (i, 0)),
                      pl.BlockSpec((ROUTE_REC, tm), lambda i, *_: (0, i)),
                      pl.BlockSpec((1, N_EXPERTS, 1), lambda i, *_: (i, 0, 0))],
            out_specs=any_spec,
            scratch_shapes=[pltpu.VMEM((2, j_rows, D_MODEL), BF16), pltpu.VMEM((rb, D_MODEL), BF16),
                            pltpu.SemaphoreType.DMA((4,))]),
        out_shape=jax.ShapeDtypeStruct((rmax, D_MODEL), BF16),
        compiler_params=params,
        name="moe_dispatch",
    )(row_s, npiece_s, tail_s, tailn_s, nused, h2, rt, loc_col)

    blk_row = lambda b, be, nu: (jnp.minimum(b, nu[0] - 1), 0)
    ho = pl.pallas_call(
        _ffn_kernel,
        grid_spec=pltpu.PrefetchScalarGridSpec(
            num_scalar_prefetch=2, grid=(nb_max,),
            in_specs=[pl.BlockSpec((rb, D_MODEL), blk_row),
                      pl.BlockSpec((1, D_MODEL, D_EXPERT), lambda b, be, nu: (be[b], 0, 0)),
                      pl.BlockSpec((1, D_MODEL, D_EXPERT), lambda b, be, nu: (be[b], 0, 0)),
                      pl.BlockSpec((1, D_EXPERT, D_MODEL), lambda b, be, nu: (be[b], 0, 0))],
            out_specs=pl.BlockSpec((rb, D_MODEL), lambda b, be, nu: (b, 0)),
            scratch_shapes=[pltpu.VMEM((D_MODEL, D_EXPERT), BF16), pltpu.VMEM((D_MODEL, D_EXPERT), BF16),
                            pltpu.VMEM((D_EXPERT, D_MODEL), BF16)]),
        out_shape=jax.ShapeDtypeStruct((rmax, D_MODEL), BF16),
        compiler_params=params,
        name="moe_ffn",
    )(blk_expert, nused, hs, w_gate_e, w_up_e, w_down_e)

    return pl.pallas_call(
        _combine_kernel,
        grid_spec=pltpu.PrefetchScalarGridSpec(
            num_scalar_prefetch=2, grid=(ntiles,),
            in_specs=[pl.BlockSpec((tm, D_MODEL), lambda i, *_: (i, 0)),
                      pl.BlockSpec((tm, ROUTE_REC), lambda i, *_: (i, 0)),
                      pl.BlockSpec((1, 1, LANES), lambda i, *_: (i, 0, 0)),
                      any_spec],
            out_specs=pl.BlockSpec((tm, D_MODEL), lambda i, *_: (i, 0)),
            scratch_shapes=[pltpu.VMEM((2, j_rows, D_MODEL), BF16), pltpu.SemaphoreType.DMA((2,))]),
        out_shape=jax.ShapeDtypeStruct((n, D_MODEL), F32),
        compiler_params=params,
        name="moe_combine",
    )(row_s, npiece_s, x1, rt.T, loc_row, ho)


def _row_tile(n, want):
    t = min(want, n)
    while n % t:
        t //= 2
    return t


def kernel(x, norm1_g, w_in, mu_prev, mu_next, w0_f, w_up_f, w0_b, w_up_b, a0_f, a_up_f, a0_b, a_up_b, g_up, k_k, k_a, r_k, lnx_g, lnx_b, q_gain, k_gain, rpb, b_gate, w_o_rwkv, w_o_na, w_out, norm2_g, w_router_group, b_router_group, w_router_expert, b_router_expert, w_gate_e, w_up_e, w_down_e):
    batch, seq, d = x.shape
    assert d == D_MODEL and seq % CHUNK == 0 and seq % GRID_W == 0
    n = batch * seq
    for l in range(norm1_g.shape[0]):
        xf = x.reshape(n, d)
        prw, q, k, v, gate = _in_proj(xf, norm1_g[l], w_in[l].astype(BF16), q_gain[l], k_gain[l],
                                      _row_tile(n, 512))
        yf, bonf, g, yb, bonb = _rwkv(prw, batch, seq, mu_prev[l], mu_next[l], w0_f[l], w_up_f[l],
                                      w0_b[l], w_up_b[l], a0_f[l], a_up_f[l], a0_b[l], a_up_b[l],
                                      g_up[l], k_k[l], k_a[l], r_k[l].reshape(-1))
        yna = _natten(q, k, v, rpb[l], batch, seq)
        tm = _row_tile(n, MOE_TM)
        x1, h2, rt, cnt = _out_proj(xf, yf, yb, bonf, bonb, g, yna, gate, lnx_g[l], lnx_b[l], b_gate[l],
                                    w_o_rwkv[l], w_o_na[l], w_out[l], norm2_g[l],
                                    w_router_group[l], b_router_group[l],
                                    w_router_expert[l], b_router_expert[l], tm)
        out = _moe(h2, x1, rt, cnt, w_gate_e[l], w_up_e[l], w_down_e[l], tm, MOE_RB)
        x = out.reshape(batch, seq, d)
    return x
```

```python
import functools

import numpy as np
import jax
import jax.numpy as jnp
from jax import lax
from jax.experimental import pallas as pl
from jax.experimental.pallas import tpu as pltpu

D_MODEL = 1024
HEAD_DIM = 64
RWKV_HEADS = 8
RWKV_WIDTH = RWKV_HEADS * HEAD_DIM
DECAY_RANK = 64
ICLR_RANK = 64
GATE_RANK = 128
NA_HEADS = 8
NA_WIDTH = NA_HEADS * HEAD_DIM
GRID_W = 64
NA_MAX_WIN_ROWS = 8
NA_WIN_COLS = 16
N_GROUPS = 4
EXPERTS_PER_GROUP = 8
N_EXPERTS = N_GROUPS * EXPERTS_PER_GROUP
D_EXPERT = 256
RMS_EPS = 1e-6
GN_EPS = 64e-5
NEG_INF = -1e30
RWKV_COLS = 3 * RWKV_WIDTH + 2 * DECAY_RANK + 2 * ICLR_RANK + GATE_RANK
NA_COLS = 3 * NA_WIDTH
GATE_COLS = 2 * D_MODEL
IN_COLS = RWKV_COLS + NA_COLS + GATE_COLS

LANES = 128
BF16_SUBLANES = 16
VMEM_LIMIT_BYTES = 56 * 1024 * 1024

CHUNK = 64
RWKV_CHUNKS_PER_STEP = 4
PAIR = LANES // HEAD_DIM
MXU_DIM = 256
RWKV_GW = LANES
RWKV_HG = RWKV_GW // HEAD_DIM
RWKV_NG = RWKV_WIDTH // RWKV_GW
EXP_M05 = float(np.exp(-0.5))
NA_ROWS_PER_STEP = 16
SUBLANES = 8
ROUTER_ROWS = SUBLANES + N_EXPERTS
ROUTE_REC = SUBLANES
MOE_TM = 512
MOE_RB = 1024
SEG = BF16_SUBLANES

F32 = jnp.float32
BF16 = jnp.bfloat16


def _dot(a, b):
    return jnp.dot(a, b, preferred_element_type=F32)


def _dot_nt(a, b):
    return lax.dot_general(a, b, (((1,), (1,)), ((), ())), preferred_element_type=F32)


def _dot_tn(a, b):
    return lax.dot_general(a, b, (((0,), (0,)), ((), ())), preferred_element_type=F32)


def _split2(z):
    hi = z.astype(BF16)
    lo = (z - hi.astype(F32)).astype(BF16)
    return hi, lo


def _grouped_dot(zb, bd):
    g = bd.shape[0]
    return jnp.concatenate([_dot(zb[:, j:j + g], bd) for j in range(0, zb.shape[1], g)], axis=1)


def _seg_dot(z, bd):
    hi, lo = _split2(z)
    return _grouped_dot(hi, bd) + _grouped_dot(lo, bd)


def _block_diag_heads(value):
    idx = np.arange(MXU_DIM) // HEAD_DIM
    return jnp.asarray((idx[:, None] == idx[None, :]).astype(np.float32) * value, dtype=BF16)


def _inproj_kernel(x_ref, g1_ref, w_ref, qg_ref, kg_ref, bd_ref,
                   prw_ref, q_ref, k_ref, v_ref, gate_ref):
    x = x_ref[...]
    ms = jnp.mean(x * x, axis=-1, keepdims=True)
    h = (x * lax.rsqrt(ms + RMS_EPS) * g1_ref[...]).astype(BF16)

    def proj(lo, hi):
        return _dot(h, w_ref[:, lo:hi])

    col = 0
    while col < RWKV_COLS:
        nxt = min(col + 512, RWKV_COLS)
        prw_ref[:, col:nxt] = proj(col, nxt).astype(BF16)
        col = nxt

    def head_rms(t, gain):
        msq = _grouped_dot((t * t).astype(BF16), bd_ref[...])
        return t * lax.rsqrt(msq + RMS_EPS) * gain

    c0 = RWKV_COLS
    q_ref[...] = head_rms(proj(c0, c0 + NA_WIDTH), qg_ref[...]).astype(BF16)
    k_ref[...] = head_rms(proj(c0 + NA_WIDTH, c0 + 2 * NA_WIDTH), kg_ref[...]).astype(BF16)
    v_ref[...] = proj(c0 + 2 * NA_WIDTH, c0 + 3 * NA_WIDTH).astype(BF16)
    c1 = c0 + NA_COLS
    for j in range(GATE_COLS // 512):
        gate_ref[:, j * 512:(j + 1) * 512] = proj(c1 + j * 512, c1 + (j + 1) * 512).astype(BF16)


def _in_proj(xf, g1, w_in_b, q_gain, k_gain, tm):
    n = xf.shape[0]
    qg = (jnp.tile(q_gain, NA_HEADS) * (HEAD_DIM ** -0.5)).reshape(1, NA_WIDTH)
    kg = jnp.tile(k_gain, NA_HEADS).reshape(1, NA_WIDTH)
    bd = _block_diag_heads(1.0 / HEAD_DIM)
    const = lambda i: (0, 0)
    row = lambda i: (i, 0)
    return pl.pallas_call(
        _inproj_kernel,
        grid=(n // tm,),
        in_specs=[
            pl.BlockSpec((tm, D_MODEL), row),
            pl.BlockSpec((1, D_MODEL), const),
            pl.BlockSpec((D_MODEL, IN_COLS), const),
            pl.BlockSpec((1, NA_WIDTH), const),
            pl.BlockSpec((1, NA_WIDTH), const),
            pl.BlockSpec((MXU_DIM, MXU_DIM), const),
        ],
        out_specs=[
            pl.BlockSpec((tm, RWKV_COLS), row),
            pl.BlockSpec((tm, NA_WIDTH), row),
            pl.BlockSpec((tm, NA_WIDTH), row),
            pl.BlockSpec((tm, NA_WIDTH), row),
            pl.BlockSpec((tm, GATE_COLS), row),
        ],
        out_shape=[
            jax.ShapeDtypeStruct((n, RWKV_COLS), BF16),
            jax.ShapeDtypeStruct((n, NA_WIDTH), BF16),
            jax.ShapeDtypeStruct((n, NA_WIDTH), BF16),
            jax.ShapeDtypeStruct((n, NA_WIDTH), BF16),
            jax.ShapeDtypeStruct((n, GATE_COLS), BF16),
        ],
        compiler_params=pltpu.CompilerParams(
            dimension_semantics=("parallel",), vmem_limit_bytes=VMEM_LIMIT_BYTES),
        name="in_proj",
    )(xf, g1.reshape(1, D_MODEL), w_in_b, qg, kg, bd)


def _expand(z, lane_head):
    return jnp.concatenate([jnp.where(lane_head == h, z, 0.0) for h in range(RWKV_HG)], axis=0).astype(BF16)


def _level_mask(t_idx, s_idx, b, reverse):
    same = (t_idx // (2 * b)) == (s_idx // (2 * b))
    t_hi = (t_idx % (2 * b)) >= b
    s_hi = (s_idx % (2 * b)) >= b
    if reverse:
        return same & jnp.logical_not(t_hi) & s_hi
    return same & t_hi & jnp.logical_not(s_hi)


def _rwkv_prep(ps, d, prm, with_gate):
    reverse = d == 1
    w = RWKV_WIDTH
    r = ps[:, 0:w]
    k = ps[:, w:2 * w]
    v = ps[:, 2 * w:3 * w]
    c_dec = 3 * w
    c_icl = c_dec + 2 * DECAY_RANK
    c_gate = c_icl + 2 * ICLR_RANK
    bd1 = prm["bd1"][...]

    w_raw = prm["w0"][d:d + 1, :] + _dot(jnp.tanh(ps[:, c_dec:c_icl]).astype(BF16), prm["wup"][d])
    a = jax.nn.sigmoid(prm["a0"][d:d + 1, :] + _dot(ps[:, c_icl:c_gate].astype(BF16), prm["aup"][d]))
    lw = -EXP_M05 * jax.nn.sigmoid(w_raw)
    kk = k * prm["k_k"][...]
    kk = kk * lax.rsqrt(_grouped_dot((kk * kk).astype(BF16), bd1) + 1e-12)
    kd = k * (1.0 + (a - 1.0) * prm["k_a"][...])
    bonus = _seg_dot(r * kd * prm["r_k"][...], bd1) * v

    nrow = ps.shape[0]
    rowc = lax.broadcasted_iota(jnp.int32, (CHUNK, CHUNK), 0)
    colc = lax.broadcasted_iota(jnp.int32, (CHUNK, CHUNK), 1)
    tri = jnp.where((colc >= rowc) if reverse else (colc <= rowc), 1.0, 0.0).astype(BF16)
    lw_hi, lw_lo = _split2(lw)
    cum = jnp.concatenate([_dot(tri, lw_hi[c:c + CHUNK]) + _dot(tri, lw_lo[c:c + CHUNK])
                           for c in range(0, nrow, CHUNK)], axis=0)
    tots =[jnp.sum(lw[c * CHUNK:(c + 1) * CHUNK], axis=0, keepdims=True) for c in range(nrow // CHUNK)]
    tot = jnp.concatenate([jnp.broadcast_to(t, (CHUNK, w)) for t in tots], axis=0)
    e_rest = jnp.exp(tot - cum)
    e_neg = jnp.exp(-cum)
    beta = kk * a
    out = dict(rt=r * jnp.exp(cum), at=-kk * jnp.exp(cum - lw), kh=kd * e_neg, bh=beta * e_neg,
               kw=kd * e_rest, bw=beta * e_rest, v=v, wc=[jnp.exp(t) for t in tots], bonus=bonus)
    if with_gate:
        out["gate"] = _dot(jax.nn.sigmoid(ps[:, c_gate:c_gate + GATE_RANK]).astype(BF16), prm["gup"][...])
    return out


def _rwkv_chains(preps, dirs, s_ref):
    gw = RWKV_GW
    m0 = lax.broadcasted_iota(jnp.int32, (1, gw), 1) // HEAD_DIM
    t_idx = lax.broadcasted_iota(jnp.int32, (CHUNK, gw), 0)
    s_idx = lax.broadcasted_iota(jnp.int32, (CHUNK, gw), 1) % CHUNK
    eye = jnp.where(s_idx == t_idx, 1.0, 0.0)
    strict = [s_idx < t_idx, s_idx > t_idx]
    incl = [s_idx <= t_idx, s_idx >= t_idx]
    bd_r = lax.broadcasted_iota(jnp.int32, (gw, gw), 0) // HEAD_DIM
    bd_c = lax.broadcasted_iota(jnp.int32, (gw, gw), 1) // HEAD_DIM
    bdmask = bd_r == bd_c

    ndir = len(preps)
    nck = preps[0]["v"].shape[0] // CHUNK
    chains = [(d, ck, pr) for ck in range(nck) for pr in range(RWKV_NG) for d in range(ndir)]

    def part(c, name):
        d, ck, pr = c
        return preps[d][name][ck * CHUNK:(ck + 1) * CHUNK, pr * gw:(pr + 1) * gw]

    ab, ak = {}, {}
    for c in chains:
        x = jnp.concatenate([part(c, "at"), part(c, "rt")], axis=0).astype(BF16)
        abk = _dot_nt(x, jnp.concatenate([_expand(part(c, "bh"), m0), _expand(part(c, "kh"), m0)], axis=0))
        ab[c], ak[c] = abk[:, :RWKV_HG * CHUNK], abk[:, RWKV_HG * CHUNK:]
    nmat, nb, a_rb, av, tinv = {}, {}, {}, {}, {}
    for c in chains:
        d = dirs[c[0]]
        nmat[c] = jnp.where(strict[d], ab[c][:CHUNK], 0.0)
        nb[c] = nmat[c].astype(BF16)
        a_rb[c] = jnp.where(incl[d], ab[c][CHUNK:], 0.0).astype(BF16)
        a_k = jnp.concatenate([jnp.where(strict[d], ak[c][:CHUNK], 0.0),
                               jnp.where(incl[d], ak[c][CHUNK:], 0.0)], axis=0).astype(BF16)
        av[c] = _dot(a_k, _expand(part(c, "v"), m0))
        tinv[c] = eye + jnp.where(_level_mask(t_idx, s_idx, 1, d == 1), nmat[c], 0.0)
    b = 2
    while b < CHUNK:
        e = {c: _dot(nb[c], _expand(tinv[c], m0)) for c in chains}
        for c in chains:
            f = _dot(tinv[c].astype(BF16), _expand(e[c], m0))
            tinv[c] = tinv[c] + jnp.where(_level_mask(t_idx, s_idx, b, dirs[c[0]] == 1), f, 0.0)
        b *= 2
    tu = {c: _dot(tinv[c].astype(BF16),
                  jnp.concatenate([_expand(part(c, "at"), m0), _expand(av[c][:CHUNK], m0)], axis=1))
          for c in chains}
    qy = {c: _dot(a_rb[c], jnp.concatenate([_expand(tu[c][:, :gw], m0), _expand(tu[c][:, gw:], m0)], axis=1))
          for c in chains}
    mt, bt = {}, {}
    for c in chains:
        bw = part(c, "bw").astype(BF16)
        ut_bw = _dot_tn(tu[c][:, :gw].astype(BF16), bw)
        ulv = jnp.concatenate([tu[c][:, gw:], part(c, "v")], axis=0).astype(BF16)
        ulv_bk = _dot_tn(ulv, jnp.concatenate([bw, part(c, "kw").astype(BF16)], axis=0))
        mt[c] = jnp.where(bdmask, ut_bw, 0.0).astype(BF16)
        bt[c] = jnp.where(bdmask, ulv_bk, 0.0)
    state = {(d, pr): s_ref[dirs[d], pr] for d in range(ndir) for pr in range(RWKV_NG)}
    ys = {}
    for step in range(nck):
        for pr in range(RWKV_NG):
            for d in range(ndir):
                c = (d, step if dirs[d] == 0 else nck - 1 - step, pr)
                s0 = state[(d, pr)]
                s0b = s0.astype(BF16)
                q = part(c, "rt") + qy[c][:, :gw]
                ys[c] = _dot_nt(q.astype(BF16), s0b) + qy[c][:, gw:] + av[c][CHUNK:]
                wc = preps[d]["wc"][c[1]][:, pr * gw:(pr + 1) * gw]
                state[(d, pr)] = s0 * wc + _dot(s0b, mt[c]) + bt[c]
    for (d, pr), s_new in state.items():
        s_ref[dirs[d], pr] = s_new
    return [jnp.concatenate([jnp.concatenate([ys[(d, ck, pr)] for pr in range(RWKV_NG)], axis=1)
                             for ck in range(nck)], axis=0) for d in range(ndir)]


def _rwkv_kernel(pf_ref, pfp_ref, pfn_ref, pb_ref, pbp_ref, pbn_ref,
                 mup_ref, mun_ref, w0_ref, wup_ref, a0_ref, aup_ref, gup_ref,
                 kk_ref, ka_ref, rk_ref, bd1_ref,
                 yf_ref, bonf_ref, g_ref, yb_ref, bonb_ref, s_ref, *, nc):
    i = pl.program_id(1)

    @pl.when(i == 0)
    def _():
        s_ref[...] = jnp.zeros_like(s_ref)

    prm = dict(w0=w0_ref, wup=wup_ref, a0=a0_ref, aup=aup_ref, gup=gup_ref,
               k_k=kk_ref, k_a=ka_ref, r_k=rk_ref, bd1=bd1_ref)
    nrow = pf_ref.shape[0]
    span = min(CHUNK, nrow)
    rows = lax.broadcasted_iota(jnp.int32, (span, span), 0)
    cols = lax.broadcasted_iota(jnp.int32, (span, span), 1)
    take_prev = jnp.where(cols == rows - 1, 1.0, 0.0).astype(BF16)
    take_next = jnp.where(cols == rows + 1, 1.0, 0.0).astype(BF16)
    sub = lax.broadcasted_iota(jnp.int32, (SUBLANES, 1), 0)

    def shifted(main_ref, prev_ref, next_ref, c):
        pm = main_ref[...]
        p = pm.astype(F32)
        edge_prev = prev_ref[...].astype(F32)[BF16_SUBLANES - 1:, :] * jnp.where(c > 0, 1.0, 0.0)
        edge_next = next_ref[...].astype(F32)[:1, :] * jnp.where(c < nc - 1, 1.0, 0.0)
        prevs, nexts = [], []
        for lo in range(0, nrow, span):
            hi = lo + span
            before = edge_prev if lo == 0 else p[lo - 1:lo]
            after = edge_next if hi == nrow else p[hi:hi + 1]
            pp = _dot(take_prev, pm[lo:hi])
            pn = _dot(take_next, pm[lo:hi])
            prevs += [jnp.where(sub == 0, before, pp[:SUBLANES]), pp[SUBLANES:]]
            nexts += [pn[:span - SUBLANES], jnp.where(sub == SUBLANES - 1, after, pn[span - SUBLANES:])]
        p_prev = jnp.concatenate(prevs, axis=0)
        p_next = jnp.concatenate(nexts, axis=0)
        return p + mup_ref[...] * (p_prev - p) + mun_ref[...] * (p_next - p)

    preps = [_rwkv_prep(shifted(pf_ref, pfp_ref, pfn_ref, i), 0, prm, True),
             _rwkv_prep(shifted(pb_ref, pbp_ref, pbn_ref, nc - 1 - i), 1, prm, False)]
    y_f, y_b = _rwkv_chains(preps, [0, 1], s_ref)
    yf_ref[...] = y_f
    bonf_ref[...] = preps[0]["bonus"].astype(BF16)
    g_ref[...] = preps[0]["gate"].astype(BF16)
    yb_ref[...] = y_b
    bonb_ref[...] = preps[1]["bonus"].astype(BF16)


def _rwkv(prw, batch, seq, mu_prev, mu_next, w0_f, w_up_f, w0_b, w_up_b, a0_f, a_up_f, a0_b, a_up_b,
          g_up, k_k, k_a, r_k):
    n = batch * seq
    cps = RWKV_CHUNKS_PER_STEP if seq % (CHUNK * RWKV_CHUNKS_PER_STEP) == 0 else 1
    nrow = cps * CHUNK
    nc = seq // nrow
    sub = nrow // BF16_SUBLANES
    nsub = n // BF16_SUBLANES
    w = RWKV_WIDTH
    zd = jnp.zeros((DECAY_RANK, w), F32)
    zi = jnp.zeros((ICLR_RANK, w), F32)
    wup = jnp.stack([jnp.concatenate([w_up_f, zd], 0), jnp.concatenate([zd, w_up_b], 0)]).astype(BF16)
    aup = jnp.stack([jnp.concatenate([a_up_f, zi], 0), jnp.concatenate([zi, a_up_b], 0)]).astype(BF16)
    w0 = jnp.stack([w0_f, w0_b])
    a0 = jnp.stack([a0_f, a0_b])
    bd1 = _block_diag_heads(1.0)

    def fwd_c(b, i):
        return i

    def bwd_c(b, i):
        return nc - 1 - i

    def main(cf):
        return lambda b, i: (b * nc + cf(b, i), 0)

    def prev(cf):
        return lambda b, i: (jnp.maximum((b * nc + cf(b, i)) * sub - 1, 0), 0)

    def nxt(cf):
        return lambda b, i: (jnp.minimum((b * nc + cf(b, i)) * sub + sub, nsub - 1), 0)

    const2 = lambda b, i: (0, 0)
    const3 = lambda b, i: (0, 0, 0)
    pspec = lambda f: pl.BlockSpec((nrow, RWKV_COLS), f)
    nspec = lambda f: pl.BlockSpec((BF16_SUBLANES, RWKV_COLS), f)
    ospec = lambda f: pl.BlockSpec((nrow, w), f)
    vec = pl.BlockSpec((1, w), const2)
    out_sds = jax.ShapeDtypeStruct((n, w), F32)
    out_bf = jax.ShapeDtypeStruct((n, w), BF16)
    return pl.pallas_call(
        functools.partial(_rwkv_kernel, nc=nc),
        grid=(batch, nc),
        in_specs=[
            pspec(main(fwd_c)), nspec(prev(fwd_c)), nspec(nxt(fwd_c)),
            pspec(main(bwd_c)), nspec(prev(bwd_c)), nspec(nxt(bwd_c)),
            pl.BlockSpec((1, RWKV_COLS), const2), pl.BlockSpec((1, RWKV_COLS), const2),
            pl.BlockSpec((2, w), const2), pl.BlockSpec((2, 2 * DECAY_RANK, w), const3),
            pl.BlockSpec((2, w), const2), pl.BlockSpec((2, 2 * ICLR_RANK, w), const3),
            pl.BlockSpec((GATE_RANK, w), const2),
            vec, vec, vec,
            pl.BlockSpec((MXU_DIM, MXU_DIM), const2),
        ],
        out_specs=[ospec(main(fwd_c)), ospec(main(fwd_c)), ospec(main(fwd_c)),
                   ospec(main(bwd_c)), ospec(main(bwd_c))],
        out_shape=[out_sds, out_bf, out_bf, out_sds, out_bf],
        scratch_shapes=[pltpu.VMEM((2, RWKV_NG, RWKV_GW, RWKV_GW), F32)],
        compiler_params=pltpu.CompilerParams(
            dimension_semantics=("arbitrary", "arbitrary"), vmem_limit_bytes=VMEM_LIMIT_BYTES),
        name="rwkv7",
    )(prw, prw, prw, prw, prw, prw,
      mu_prev.reshape(1, RWKV_COLS), mu_next.reshape(1, RWKV_COLS), w0, wup, a0, aup,
      g_up.astype(BF16), k_k.reshape(1, w), k_a.reshape(1, w), r_k.reshape(1, w), bd1)


def _na_kernel(q_ref, k_ref, v_ref, bias_ref, o_ref, *, rows, kh, rblk):
    j = pl.program_id(1)
    lane = lax.broadcasted_iota(jnp.int32, (1, LANES), 1)
    m0 = lane < HEAD_DIM
    zero = jnp.zeros((), BF16)
    npairs = NA_HEADS // PAIR
    chains = [(r, pr) for r in range(rblk) for pr in range(npairs)]
    starts, deltas = [], []
    for r in range(rblk):
        i = j * rblk + r
        start = jnp.clip(i - kh // 2, 0, rows - kh)
        starts.append(pl.multiple_of(start * GRID_W, GRID_W))
        deltas.append(i - start)
    s = {}
    for (r, pr) in chains:
        sl = slice(pr * LANES, (pr + 1) * LANES)
        qp = q_ref[r * GRID_W:(r + 1) * GRID_W, sl]
        qs = jnp.concatenate([jnp.where(m0, qp, zero), jnp.where(m0, zero, qp)], axis=0)
        kb = k_ref[pl.ds(starts[r], kh * GRID_W), sl]
        s[(r, pr)] = _dot_nt(qs, kb) + bias_ref[deltas[r], pr]
    e, l = {}, {}
    for c in chains:
        m = jnp.max(s[c], axis=-1, keepdims=True)
        p = jnp.exp(s[c] - m)
        l[c] = jnp.sum(p, axis=-1, keepdims=True)
        e[c] = p.astype(BF16)
    for r in range(rblk):
        outs = []
        for pr in range(npairs):
            sl = slice(pr * LANES, (pr + 1) * LANES)
            vb = v_ref[pl.ds(starts[r], kh * GRID_W), sl]
            pv = _dot(e[(r, pr)], vb) / l[(r, pr)]
            outs.append(jnp.where(m0, pv[:GRID_W], pv[GRID_W:]))
        o_ref[r * GRID_W:(r + 1) * GRID_W, :] = jnp.concatenate(outs, axis=1).astype(BF16)


def _na_bias_table(rpb, kh):
    cols = np.arange(GRID_W)
    col_start = np.clip(cols - NA_WIN_COLS // 2, 0, GRID_W - NA_WIN_COLS)
    col_mask = (cols[None, :] >= col_start[:, None]) & (cols[None, :] < col_start[:, None] + NA_WIN_COLS)
    col_off = np.clip(cols[None, :] - cols[:, None] + NA_WIN_COLS - 1, 0, 2 * NA_WIN_COLS - 2)
    delta = np.arange(kh)
    row_off = np.arange(kh)[None, :] - delta[:, None] + NA_MAX_WIN_ROWS - 1
    col_sel = (col_off[None] == np.arange(2 * NA_WIN_COLS - 1)[:, None, None]).astype(np.float32)
    row_sel = (row_off[:, :, None] == np.arange(2 * NA_MAX_WIN_ROWS - 1)).astype(np.float32)
    hi = lax.Precision.HIGHEST
    t = jnp.einsum("hoc,cqk->hoqk", rpb.astype(F32), col_sel, precision=hi)
    t = jnp.einsum("dro,hoqk->dhqrk", row_sel, t, precision=hi)
    t = jnp.where(col_mask[None, None, :, None, :], t, NEG_INF)
    return t.reshape(kh, NA_HEADS // PAIR, PAIR * GRID_W, kh * GRID_W)


def _natten(q, k, v, rpb, batch, seq):
    n = batch * seq
    rows = seq // GRID_W
    kh = min(NA_MAX_WIN_ROWS, rows)
    bias = _na_bias_table(rpb, kh)
    rblk = NA_ROWS_PER_STEP if rows % NA_ROWS_PER_STEP == 0 else 1
    nblk = rows // rblk

    return pl.pallas_call(
        functools.partial(_na_kernel, rows=rows, kh=kh, rblk=rblk),
        grid=(batch, nblk),
        in_specs=[
            pl.BlockSpec((rblk * GRID_W, NA_WIDTH), lambda b, i: (b * nblk + i, 0)),
            pl.BlockSpec((seq, NA_WIDTH), lambda b, i: (b, 0)),
            pl.BlockSpec((seq, NA_WIDTH), lambda b, i: (b, 0)),
            pl.BlockSpec((kh, NA_HEADS // PAIR, PAIR * GRID_W, kh * GRID_W), lambda b, i: (0, 0, 0, 0),
                         pipeline_mode=pl.Buffered(1)),
        ],
        out_specs=pl.BlockSpec((rblk * GRID_W, NA_WIDTH), lambda b, i: (b * nblk + i, 0)),
        out_shape=jax.ShapeDtypeStruct((n, NA_WIDTH), BF16),
        compiler_params=pltpu.CompilerParams(
            dimension_semantics=("parallel", "arbitrary"), vmem_limit_bytes=VMEM_LIMIT_BYTES),
        name="natten2d",
    )(q, k, v, bias)


def _route_t(logits_t):
    tm = logits_t.shape[1]
    sub = lax.broadcasted_iota(jnp.int32, (SUBLANES, tm), 0)
    ninf = -jnp.inf
    gvalid = sub < N_GROUPS
    gl = jnp.where(gvalid, logits_t[0:SUBLANES], ninf)
    gmax = jnp.max(gl, axis=0, keepdims=True)
    g_sel = jnp.min(jnp.where(gl == gmax, sub, SUBLANES), axis=0, keepdims=True)
    p_group = 1.0 / jnp.sum(jnp.where(gvalid, jnp.exp(gl - gmax), 0.0), axis=0, keepdims=True)
    ev = logits_t[SUBLANES:2 * SUBLANES]
    for g in range(1, N_GROUPS):
        lo = SUBLANES + g * EXPERTS_PER_GROUP
        ev = jnp.where(g_sel == g, logits_t[lo:lo + EXPERTS_PER_GROUP], ev)
    m1 = jnp.max(ev, axis=0, keepdims=True)
    i1 = jnp.min(jnp.where(ev == m1, sub, SUBLANES), axis=0, keepdims=True)
    ev2 = jnp.where(sub == i1, ninf, ev)
    m2 = jnp.max(ev2, axis=0, keepdims=True)
    i2 = jnp.min(jnp.where(ev2 == m2, sub, SUBLANES), axis=0, keepdims=True)
    t = jnp.exp(m2 - m1)
    w1 = p_group / (1.0 + t)
    w2 = p_group * t / (1.0 + t)
    e1 = g_sel * EXPERTS_PER_GROUP + i1
    e2 = g_sel * EXPERTS_PER_GROUP + i2
    rec = jnp.where(sub == 0, e1.astype(F32),
                    jnp.where(sub == 1, e2.astype(F32),
                              jnp.where(sub == 2, w1, jnp.where(sub == 3, w2, 0.0))))
    return rec, e1, e2


def _out_kernel(x_ref, yf_ref, yb_ref, bonf_ref, bonb_ref, g_ref, yna_ref, gate_ref,
                lng_ref, lnb_ref, bgate_ref, worw_ref, wona_ref, wout_ref, n2g_ref, wrh_ref, wrl_ref, br_ref,
                bd_ref, x1_ref, h2_ref, rt_ref, cnt_ref):
    bd = bd_ref[...]
    y = yf_ref[...] + yb_ref[...]
    yc = y - _seg_dot(y, bd)
    var = _seg_dot(yc * yc, bd)
    yn = yc * lax.rsqrt(var + GN_EPS) * lng_ref[...] + lnb_ref[...]
    ya = (yn + bonf_ref[...].astype(F32) + bonb_ref[...].astype(F32)) * g_ref[...].astype(F32)
    ya_o = _dot(ya.astype(BF16), worw_ref[...])
    yb_o = _dot(yna_ref[...], wona_ref[...])
    gl = gate_ref[...].astype(F32) + bgate_ref[...]
    mix = jax.nn.sigmoid(gl[:, :D_MODEL]) * ya_o + jax.nn.sigmoid(gl[:, D_MODEL:]) * yb_o
    x1 = x_ref[...] + _dot(mix.astype(BF16), wout_ref[...])
    x1_ref[...] = x1
    ms = jnp.mean(x1 * x1, axis=-1, keepdims=True)
    h2 = x1 * lax.rsqrt(ms + RMS_EPS) * n2g_ref[...]
    h2_hi, h2_lo = _split2(h2)
    h2_ref[...] = h2_hi
    logits_t = (_dot_nt(wrh_ref[...], h2_hi) + _dot_nt(wrh_ref[...], h2_lo)
                + _dot_nt(wrl_ref[...], h2_hi) + br_ref[...])
    rec, e1, e2 = _route_t(logits_t)
    rt_ref[...] = rec
    rows_e = lax.broadcasted_iota(jnp.int32, (N_EXPERTS, rec.shape[1]), 0)
    chosen = jnp.where((rows_e == e1) | (rows_e == e2), 1.0, 0.0)
    cnt_ref[0] = jnp.broadcast_to(jnp.sum(chosen, axis=1, keepdims=True), (N_EXPERTS, LANES))


def _out_proj(xf, yf, yb, bonf, bonb, g, yna, gate, lnx_g, lnx_b, b_gate, w_o_rwkv, w_o_na, w_out,
              norm2_g, w_rg, b_rg, w_re, b_re, tm):
    n = xf.shape[0]
    w = RWKV_WIDTH
    assert EXPERTS_PER_GROUP == SUBLANES and N_GROUPS <= SUBLANES
    gpad = SUBLANES - N_GROUPS
    wr_t = jnp.concatenate([w_rg.T, jnp.zeros((gpad, D_MODEL), F32), w_re.T], axis=0)
    wr_hi = wr_t.astype(BF16)
    wr_lo = (wr_t - wr_hi.astype(F32)).astype(BF16)
    br = jnp.concatenate([b_rg, jnp.zeros((gpad,), F32), b_re]).reshape(ROUTER_ROWS, 1)
    bd = _block_diag_heads(1.0 / HEAD_DIM)
    const = lambda i: (0, 0)
    row = lambda i: (i, 0)
    rs = lambda c: pl.BlockSpec((tm, c), row)
    cs = lambda r, c: pl.BlockSpec((r, c), const)
    return pl.pallas_call(
        _out_kernel,
        grid=(n // tm,),
        in_specs=[rs(D_MODEL), rs(w), rs(w), rs(w), rs(w), rs(w), rs(NA_WIDTH), rs(GATE_COLS),
                  cs(1, w), cs(1, w), cs(1, GATE_COLS), cs(w, D_MODEL), cs(NA_WIDTH, D_MODEL),
                  cs(D_MODEL, D_MODEL), cs(1, D_MODEL), cs(ROUTER_ROWS, D_MODEL), cs(ROUTER_ROWS, D_MODEL),
                  cs(ROUTER_ROWS, 1), cs(MXU_DIM, MXU_DIM)],
        out_specs=[rs(D_MODEL), rs(D_MODEL), pl.BlockSpec((ROUTE_REC, tm), lambda i: (0, i)),
                   pl.BlockSpec((1, N_EXPERTS, LANES), lambda i: (i, 0, 0))],
        out_shape=[jax.ShapeDtypeStruct((n, D_MODEL), F32),
                   jax.ShapeDtypeStruct((n, D_MODEL), BF16),
                   jax.ShapeDtypeStruct((ROUTE_REC, n), F32),
                   jax.ShapeDtypeStruct((n // tm, N_EXPERTS, LANES), F32)],
        compiler_params=pltpu.CompilerParams(
            dimension_semantics=("parallel",), vmem_limit_bytes=VMEM_LIMIT_BYTES),
        name="out_proj",
    )(xf, yf, yb, bonf, bonb, g, yna, gate,
      lnx_g.reshape(1, w), lnx_b.reshape(1, w), b_gate.reshape(1, GATE_COLS),
      w_o_rwkv.astype(BF16), w_o_na.astype(BF16), w_out.astype(BF16),
      norm2_g.reshape(1, D_MODEL), wr_hi, wr_lo, br, bd)


def _slot_rows(tm):
    rows = 2 * tm + N_EXPERTS * (SEG - 1)
    return -(-rows // LANES) * LANES


def _moe_plan(cnt, rb, nb_max, pieces_per_tile):
    ntiles = cnt.shape[0]
    e_before = jnp.asarray(np.tril(np.ones((N_EXPERTS, N_EXPERTS), np.int32), -1))
    t_before = jnp.asarray(np.tril(np.ones((ntiles, ntiles), np.int32), -1))
    pc = ((cnt + SEG - 1) // SEG) * SEG
    loc = jnp.sum(pc[:, None, :] * e_before[None], axis=2)
    within = jnp.sum(t_before[:, :, None] * pc[None], axis=1)
    filled = jnp.sum(pc, axis=0)
    region = ((filled + rb - 1) // rb) * rb
    region_off = jnp.sum(region[None, :] * e_before, axis=1)
    gbase = region_off[None, :] + within
    blk_end = (region_off + region) // rb
    nused = blk_end[-1:]
    blk = jnp.minimum(jnp.arange(nb_max, dtype=jnp.int32), nused - 1)
    blk_expert = jnp.minimum(jnp.sum((blk_end[None, :] <= blk[:, None]).astype(jnp.int32), axis=1),
                             N_EXPERTS - 1)
    n16 = pc // SEG
    end16 = (loc + pc) // SEG
    piece = jnp.arange(pieces_per_tile, dtype=jnp.int32)
    seg_of_piece = jnp.minimum(jnp.sum((end16[:, None, :] <= piece[None, :, None]).astype(jnp.int32), axis=2),
                               N_EXPERTS - 1)
    onehot = (seg_of_piece[:, :, None] == jnp.arange(N_EXPERTS)[None, None, :]).astype(jnp.int32)
    piece_row = jnp.sum(onehot * (gbase - loc)[:, None, :], axis=2) + piece[None, :] * SEG
    i32 = lambda a: a.reshape(-1).astype(jnp.int32)
    return (loc, i32(piece_row), i32(jnp.sum(n16, axis=1)), i32(blk_expert), i32(nused),
            i32(region_off + filled), i32((region - filled) // SEG))


def _piece_copies(local_ref, global_ref, row_s, tile, count, sem, outbound, pieces_per_tile):
    def one(p, carry):
        loc_rows = local_ref.at[pl.ds(pl.multiple_of(p * SEG, SEG), SEG)]
        glob_rows = global_ref.at[pl.ds(pl.multiple_of(row_s[tile * pieces_per_tile + p], SEG), SEG)]
        if outbound:
            pltpu.make_async_copy(loc_rows, glob_rows, sem).start()
        else:
            pltpu.make_async_copy(glob_rows, loc_rows, sem).start()
        return carry

    lax.fori_loop(0, count, one, 0)


def _wait_copies(src_ref, dst_ref, count, sem):
    def one(b, carry):
        pltpu.make_async_copy(src_ref.at[pl.ds(0, SEG)], dst_ref.at[pl.ds(0, SEG)], sem).wait()
        return carry

    lax.fori_loop(0, count, one, 0)


def _dispatch_kernel(row_s, npiece_s, tail_s, tailn_s, nused_s, h_ref, rt_ref, locc_ref, hs_ref,
                     sorted_ref, zero_ref, sems, *, nb_max, rb):
    tile = pl.program_id(0)
    ntiles = pl.num_programs(0)
    slot = tile % 2
    sorted_ref = sorted_ref.at[slot]
    sem = sems.at[slot]
    pieces_per_tile = sorted_ref.shape[0] // SEG

    @pl.when(tile >= 2)
    def _():
        _wait_copies(sorted_ref, hs_ref, npiece_s[tile - 2], sem)

    tm = h_ref.shape[0]
    rt = rt_ref[...]
    e1 = rt[0:1].astype(jnp.int32)
    e2 = rt[1:2].astype(jnp.int32)
    rows_e = lax.broadcasted_iota(jnp.int32, (N_EXPERTS, tm), 0)
    ind1 = rows_e == e1
    ind2 = rows_e == e2
    ind = jnp.where(ind1 | ind2, 1.0, 0.0).astype(BF16)
    earlier = jnp.where(lax.broadcasted_iota(jnp.int32, (tm, tm), 0) < lax.broadcasted_iota(jnp.int32, (tm, tm), 1),
                        1.0, 0.0).astype(BF16)
    pos = locc_ref[0] + _dot(ind, earlier)
    pos1 = jnp.sum(jnp.where(ind1, pos, 0.0), axis=0, keepdims=True).astype(jnp.int32)
    pos2 = jnp.sum(jnp.where(ind2, pos, 0.0), axis=0, keepdims=True).astype(jnp.int32)
    row_id = lax.broadcasted_iota(jnp.int32, (sorted_ref.shape[0], tm), 0)
    onehot = jnp.where((row_id == pos1) | (row_id == pos2), 1.0, 0.0).astype(BF16)
    sorted_ref[...] = _dot(onehot, h_ref[...]).astype(BF16)
    count = npiece_s[tile]
    _piece_copies(sorted_ref, hs_ref, row_s, tile, count, sem, True, pieces_per_tile)

    @pl.when(tile == ntiles - 1)
    def _():
        zero_ref[...] = jnp.zeros_like(zero_ref)

        def tail(e, total):
            def piece(b, carry):
                pltpu.make_async_copy(
                    zero_ref.at[pl.ds(0, SEG)],
                    hs_ref.at[pl.ds(pl.multiple_of(tail_s[e] + b * SEG, SEG), SEG)], sems.at[2]).start()
                return carry

            lax.fori_loop(0, tailn_s[e], piece, 0)
            return total + tailn_s[e]

        n_tail = lax.fori_loop(0, N_EXPERTS, tail, 0)

        def block(b, carry):
            pltpu.make_async_copy(zero_ref, hs_ref.at[pl.ds(pl.multiple_of(b * rb, rb), rb)], sems.at[3]).start()
            return carry

        lax.fori_loop(nused_s[0], nb_max, block, 0)
        _wait_copies(zero_ref, hs_ref, n_tail, sems.at[2])

        def block_wait(b, carry):
            pltpu.make_async_copy(zero_ref, hs_ref.at[pl.ds(0, rb)], sems.at[3]).wait()
            return carry

        lax.fori_loop(nused_s[0], nb_max, block_wait, 0)
        _wait_copies(sorted_ref, hs_ref, count, sem)

        @pl.when(tile >= 1)
        def _():
            _wait_copies(sorted_ref, hs_ref, npiece_s[tile - 1], sems.at[1 - slot])


def _ffn_kernel(be_s, nused_s, hs_ref, wg_ref, wu_ref, wd_ref, o_ref, wgb_ref, wub_ref, wdb_ref):
    b = pl.program_id(0)
    used = b < nused_s[0]

    @pl.when(used & ((b == 0) | (be_s[b] != be_s[jnp.maximum(b - 1, 0)])))
    def _():
        wgb_ref[...] = wg_ref[0].astype(BF16)
        wub_ref[...] = wu_ref[0].astype(BF16)
        wdb_ref[...] = wd_ref[0].astype(BF16)

    @pl.when(used)
    def _():
        x = hs_ref[...]
        gate = _dot(x, wgb_ref[...])
        up = _dot(x, wub_ref[...])
        he = (gate * jax.nn.sigmoid(gate) * up).astype(BF16)
        o_ref[...] = _dot(he, wdb_ref[...]).astype(BF16)

    @pl.when(jnp.logical_not(used))
    def _():
        o_ref[...] = jnp.zeros_like(o_ref)


def _combine_kernel(row_s, npiece_s, x1_ref, rc_ref, locr_ref, ho_ref, out_ref, obuf_ref, sems):
    tile = pl.program_id(0)
    ntiles = pl.num_programs(0)
    slot = tile % 2
    tm = x1_ref.shape[0]
    pieces_per_tile = obuf_ref.shape[1] // SEG

    @pl.when(tile == 0)
    def _():
        obuf_ref[...] = jnp.zeros_like(obuf_ref)
        _piece_copies(obuf_ref.at[0], ho_ref, row_s, 0, npiece_s[0], sems.at[0], False, pieces_per_tile)

    @pl.when(tile + 1 < ntiles)
    def _():
        _piece_copies(obuf_ref.at[1 - slot], ho_ref, row_s, tile + 1, npiece_s[tile + 1], sems.at[1 - slot],
                      False, pieces_per_tile)

    obuf_ref = obuf_ref.at[slot]
    rc = rc_ref[...]
    e1 = rc[:, 0:1].astype(jnp.int32)
    e2 = rc[:, 1:2].astype(jnp.int32)
    lane_e = lax.broadcasted_iota(jnp.int32, (tm, LANES), 1)
    ind1 = lane_e == e1
    ind2 = lane_e == e2
    ind = jnp.where(ind1 | ind2, 1.0, 0.0).astype(BF16)
    earlier = jnp.where(lax.broadcasted_iota(jnp.int32, (tm, tm), 1) < lax.broadcasted_iota(jnp.int32, (tm, tm), 0),
                        1.0, 0.0).astype(BF16)
    pos = locr_ref[0] + _dot(earlier, ind)
    pos1 = jnp.sum(jnp.where(ind1, pos, 0.0), axis=1, keepdims=True).astype(jnp.int32)
    pos2 = jnp.sum(jnp.where(ind2, pos, 0.0), axis=1, keepdims=True).astype(jnp.int32)
    row_id = lax.broadcasted_iota(jnp.int32, (tm, obuf_ref.shape[0]), 1)
    pick = jnp.where(row_id == pos1, rc[:, 2:3], jnp.where(row_id == pos2, rc[:, 3:4], 0.0)).astype(BF16)
    _wait_copies(ho_ref, obuf_ref, npiece_s[tile], sems.at[slot])
    out_ref[...] = x1_ref[...] + _dot(pick, obuf_ref[...])


def _moe(h2, x1, rt, cnt, w_gate_e, w_up_e, w_down_e, tm, rb):
    n = h2.shape[0]
    ntiles = n // tm
    j_rows = _slot_rows(tm)
    nb_max = -(-(2 * n + ntiles * N_EXPERTS * (SEG - 1) + N_EXPERTS * (rb - 1)) // rb)
    rmax = nb_max * rb
    loc, row_s, npiece_s, blk_expert, nused, tail_s, tailn_s = _moe_plan(
        cnt[:, :, 0].astype(jnp.int32), rb, nb_max, j_rows // SEG)
    loc_col = loc.astype(F32).reshape(ntiles, N_EXPERTS, 1)
    loc_row = jnp.pad(loc.astype(F32), ((0, 0), (0, LANES - N_EXPERTS))).reshape(ntiles, 1, LANES)
    any_spec = pl.BlockSpec(memory_space=pl.ANY)
    params = pltpu.CompilerParams(dimension_semantics=("arbitrary",), vmem_limit_bytes=VMEM_LIMIT_BYTES)

    hs = pl.pallas_call(
        functools.partial(_dispatch_kernel, nb_max=nb_max, rb=rb),
        grid_spec=pltpu.PrefetchScalarGridSpec(
            num_scalar_prefetch=5, grid=(ntiles,),
            in_specs=[pl.BlockSpec((tm, D_MODEL), lambda i, *_: (i, 0)),
                      pl.BlockSpec((ROUTE_REC, tm), lambda i, *_: (0, i)),
                      pl.BlockSpec((1, N_EXPERTS, 1), lambda i, *_: (i, 0, 0))],
            out_specs=any_spec,
            scratch_shapes=[pltpu.VMEM((2, j_rows, D_MODEL), BF16), pltpu.VMEM((rb, D_MODEL), BF16),
                            pltpu.SemaphoreType.DMA((4,))]),
        out_shape=jax.ShapeDtypeStruct((rmax, D_MODEL), BF16),
        compiler_params=params,
        name="moe_dispatch",
    )(row_s, npiece_s, tail_s, tailn_s, nused, h2, rt, loc_col)

    blk_row = lambda b, be, nu: (jnp.minimum(b, nu[0] - 1), 0)
    ho = pl.pallas_call(
        _ffn_kernel,
        grid_spec=pltpu.PrefetchScalarGridSpec(
            num_scalar_prefetch=2, grid=(nb_max,),
            in_specs=[pl.BlockSpec((rb, D_MODEL), blk_row),
                      pl.BlockSpec((1, D_MODEL, D_EXPERT), lambda b, be, nu: (be[b], 0, 0)),
                      pl.BlockSpec((1, D_MODEL, D_EXPERT), lambda b, be, nu: (be[b], 0, 0)),
                      pl.BlockSpec((1, D_EXPERT, D_MODEL), lambda b, be, nu: (be[b], 0, 0))],
            out_specs=pl.BlockSpec((rb, D_MODEL), lambda b, be, nu: (b, 0)),
            scratch_shapes=[pltpu.VMEM((D_MODEL, D_EXPERT), BF16), pltpu.VMEM((D_MODEL, D_EXPERT), BF16),
                            pltpu.VMEM((D_EXPERT, D_MODEL), BF16)]),
        out_shape=jax.ShapeDtypeStruct((rmax, D_MODEL), BF16),
        compiler_params=params,
        name="moe_ffn",
    )(blk_expert, nused, hs, w_gate_e, w_up_e, w_down_e)

    return pl.pallas_call(
        _combine_kernel,
        grid_spec=pltpu.PrefetchScalarGridSpec(
            num_scalar_prefetch=2, grid=(ntiles,),
            in_specs=[pl.BlockSpec((tm, D_MODEL), lambda i, *_: (i, 0)),
                      pl.BlockSpec((tm, ROUTE_REC), lambda i, *_: (i, 0)),
                      pl.BlockSpec((1, 1, LANES), lambda i, *_: (i, 0, 0)),
                      any_spec],
            out_specs=pl.BlockSpec((tm, D_MODEL), lambda i, *_: (i, 0)),
            scratch_shapes=[pltpu.VMEM((2, j_rows, D_MODEL), BF16), pltpu.SemaphoreType.DMA((2,))]),
        out_shape=jax.ShapeDtypeStruct((n, D_MODEL), F32),
        compiler_params=params,
        name="moe_combine",
    )(row_s, npiece_s, x1, rt.T, loc_row, ho)


def _row_tile(n, want):
    t = min(want, n)
    while n % t:
        t //= 2
    return t


def kernel(x, norm1_g, w_in, mu_prev, mu_next, w0_f, w_up_f, w0_b, w_up_b, a0_f, a_up_f, a0_b, a_up_b, g_up, k_k, k_a, r_k, lnx_g, lnx_b, q_gain, k_gain, rpb, b_gate, w_o_rwkv, w_o_na, w_out, norm2_g, w_router_group, b_router_group, w_router_expert, b_router_expert, w_gate_e, w_up_e, w_down_e):
    batch, seq, d = x.shape
    assert d == D_MODEL and seq % CHUNK == 0 and seq % GRID_W == 0
    n = batch * seq
    for l in range(norm1_g.shape[0]):
        xf = x.reshape(n, d)
        prw, q, k, v, gate = _in_proj(xf, norm1_g[l], w_in[l].astype(BF16), q_gain[l], k_gain[l],
                                      _row_tile(n, 512))
        yf, bonf, g, yb, bonb = _rwkv(prw, batch, seq, mu_prev[l], mu_next[l], w0_f[l], w_up_f[l],
                                      w0_b[l], w_up_b[l], a0_f[l], a_up_f[l], a0_b[l], a_up_b[l],
                                      g_up[l], k_k[l], k_a[l], r_k[l].reshape(-1))
        yna = _natten(q, k, v, rpb[l], batch, seq)
        tm = _row_tile(n, MOE_TM)
        x1, h2, rt, cnt = _out_proj(xf, yf, yb, bonf, bonb, g, yna, gate, lnx_g[l], lnx_b[l], b_gate[l],
                                    w_o_rwkv[l], w_o_na[l], w_out[l], norm2_g[l],
                                    w_router_group[l], b_router_group[l],
                                    w_router_expert[l], b_router_expert[l], tm)
        out = _moe(h2, x1, rt, cnt, w_gate_e[l], w_up_e[l], w_down_e[l], tm, MOE_RB)
        x = out.reshape(batch, seq, d)
    return x
```

```python
import functools

import numpy as np
import jax
import jax.numpy as jnp
from jax import lax
from jax.experimental import pallas as pl
from jax.experimental.pallas import tpu as pltpu

D_MODEL = 1024
HEAD_DIM = 64
RWKV_HEADS = 8
RWKV_WIDTH = RWKV_HEADS * HEAD_DIM
DECAY_RANK = 64
ICLR_RANK = 64
GATE_RANK = 128
NA_HEADS = 8
NA_WIDTH = NA_HEADS * HEAD_DIM
GRID_W = 64
NA_MAX_WIN_ROWS = 8
NA_WIN_COLS = 16
N_GROUPS = 4
EXPERTS_PER_GROUP = 8
N_EXPERTS = N_GROUPS * EXPERTS_PER_GROUP
D_EXPERT = 256
RMS_EPS = 1e-6
GN_EPS = 64e-5
NEG_INF = -1e30
RWKV_COLS = 3 * RWKV_WIDTH + 2 * DECAY_RANK + 2 * ICLR_RANK + GATE_RANK
NA_COLS = 3 * NA_WIDTH
GATE_COLS = 2 * D_MODEL
IN_COLS = RWKV_COLS + NA_COLS + GATE_COLS

LANES = 128
BF16_SUBLANES = 16
VMEM_LIMIT_BYTES = 56 * 1024 * 1024

CHUNK = 64
RWKV_CHUNKS_PER_STEP = 4
PAIR = LANES // HEAD_DIM
MXU_DIM = 256
RWKV_GW = LANES
RWKV_HG = RWKV_GW // HEAD_DIM
RWKV_NG = RWKV_WIDTH // RWKV_GW
EXP_M05 = float(np.exp(-0.5))
NA_ROWS_PER_STEP = 16
SUBLANES = 8
ROUTER_ROWS = SUBLANES + N_EXPERTS
ROUTE_REC = SUBLANES
MOE_TM = 512
MOE_RB = 1024
SEG = BF16_SUBLANES

F32 = jnp.float32
BF16 = jnp.bfloat16


def _dot(a, b):
    return jnp.dot(a, b, preferred_element_type=F32)


def _dot_nt(a, b):
    return lax.dot_general(a, b, (((1,), (1,)), ((), ())), preferred_element_type=F32)


def _dot_tn(a, b):
    return lax.dot_general(a, b, (((0,), (0,)), ((), ())), preferred_element_type=F32)


def _split2(z):
    hi = z.astype(BF16)
    lo = (z - hi.astype(F32)).astype(BF16)
    return hi, lo


def _grouped_dot(zb, bd):
    g = bd.shape[0]
    return jnp.concatenate([_dot(zb[:, j:j + g], bd) for j in range(0, zb.shape[1], g)], axis=1)


def _seg_dot(z, bd):
    hi, lo = _split2(z)
    return _grouped_dot(hi, bd) + _grouped_dot(lo, bd)


def _block_diag_heads(value):
    idx = np.arange(MXU_DIM) // HEAD_DIM
    return jnp.asarray((idx[:, None] == idx[None, :]).astype(np.float32) * value, dtype=BF16)


def _inproj_kernel(x_ref, g1_ref, w_ref, qg_ref, kg_ref, bd_ref,
                   prw_ref, q_ref, k_ref, v_ref, gate_ref):
    x = x_ref[...]
    ms = jnp.mean(x * x, axis=-1, keepdims=True)
    h = (x * lax.rsqrt(ms + RMS_EPS) * g1_ref[...]).astype(BF16)

    def proj(lo, hi):
        return _dot(h, w_ref[:, lo:hi])

    col = 0
    while col < RWKV_COLS:
        nxt = min(col + 512, RWKV_COLS)
        prw_ref[:, col:nxt] = proj(col, nxt).astype(BF16)
        col = nxt

    def head_rms(t, gain):
        msq = _grouped_dot((t * t).astype(BF16), bd_ref[...])
        return t * lax.rsqrt(msq + RMS_EPS) * gain

    c0 = RWKV_COLS
    q_ref[...] = head_rms(proj(c0, c0 + NA_WIDTH), qg_ref[...]).astype(BF16)
    k_ref[...] = head_rms(proj(c0 + NA_WIDTH, c0 + 2 * NA_WIDTH), kg_ref[...]).astype(BF16)
    v_ref[...] = proj(c0 + 2 * NA_WIDTH, c0 + 3 * NA_WIDTH).astype(BF16)
    c1 = c0 + NA_COLS
    for j in range(GATE_COLS // 512):
        gate_ref[:, j * 512:(j + 1) * 512] = proj(c1 + j * 512, c1 + (j + 1) * 512).astype(BF16)


def _in_proj(xf, g1, w_in_b, q_gain, k_gain, tm):
    n = xf.shape[0]
    qg = (jnp.tile(q_gain, NA_HEADS) * (HEAD_DIM ** -0.5)).reshape(1, NA_WIDTH)
    kg = jnp.tile(k_gain, NA_HEADS).reshape(1, NA_WIDTH)
    bd = _block_diag_heads(1.0 / HEAD_DIM)
    const = lambda i: (0, 0)
    row = lambda i: (i, 0)
    return pl.pallas_call(
        _inproj_kernel,
        grid=(n // tm,),
        in_specs=[
            pl.BlockSpec((tm, D_MODEL), row),
            pl.BlockSpec((1, D_MODEL), const),
            pl.BlockSpec((D_MODEL, IN_COLS), const),
            pl.BlockSpec((1, NA_WIDTH), const),
            pl.BlockSpec((1, NA_WIDTH), const),
            pl.BlockSpec((MXU_DIM, MXU_DIM), const),
        ],
        out_specs=[
            pl.BlockSpec((tm, RWKV_COLS), row),
            pl.BlockSpec((tm, NA_WIDTH), row),
            pl.BlockSpec((tm, NA_WIDTH), row),
            pl.BlockSpec((tm, NA_WIDTH), row),
            pl.BlockSpec((tm, GATE_COLS), row),
        ],
        out_shape=[
            jax.ShapeDtypeStruct((n, RWKV_COLS), BF16),
            jax.ShapeDtypeStruct((n, NA_WIDTH), BF16),
            jax.ShapeDtypeStruct((n, NA_WIDTH), BF16),
            jax.ShapeDtypeStruct((n, NA_WIDTH), BF16),
            jax.ShapeDtypeStruct((n, GATE_COLS), BF16),
        ],
        compiler_params=pltpu.CompilerParams(
            dimension_semantics=("parallel",), vmem_limit_bytes=VMEM_LIMIT_BYTES),
        name="in_proj",
    )(xf, g1.reshape(1, D_MODEL), w_in_b, qg, kg, bd)


def _expand(z, lane_head):
    return jnp.concatenate([jnp.where(lane_head == h, z, 0.0) for h in range(RWKV_HG)], axis=0).astype(BF16)


def _level_mask(t_idx, s_idx, b, reverse):
    same = (t_idx // (2 * b)) == (s_idx // (2 * b))
    t_hi = (t_idx % (2 * b)) >= b
    s_hi = (s_idx % (2 * b)) >= b
    if reverse:
        return same & jnp.logical_not(t_hi) & s_hi
    return same & t_hi & jnp.logical_not(s_hi)


def _rwkv_prep(ps, d, prm, with_gate):
    reverse = d == 1
    w = RWKV_WIDTH
    r = ps[:, 0:w]
    k = ps[:, w:2 * w]
    v = ps[:, 2 * w:3 * w]
    c_dec = 3 * w
    c_icl = c_dec + 2 * DECAY_RANK
    c_gate = c_icl + 2 * ICLR_RANK
    bd1 = prm["bd1"][...]

    w_raw = prm["w0"][d:d + 1, :] + _dot(jnp.tanh(ps[:, c_dec:c_icl]).astype(BF16), prm["wup"][d])
    a = jax.nn.sigmoid(prm["a0"][d:d + 1, :] + _dot(ps[:, c_icl:c_gate].astype(BF16), prm["aup"][d]))
    lw = -EXP_M05 * jax.nn.sigmoid(w_raw)
    kk = k * prm["k_k"][...]
    kk = kk * lax.rsqrt(_grouped_dot((kk * kk).astype(BF16), bd1) + 1e-12)
    kd = k * (1.0 + (a - 1.0) * prm["k_a"][...])
    bonus = _seg_dot(r * kd * prm["r_k"][...], bd1) * v

    nrow = ps.shape[0]
    rowc = lax.broadcasted_iota(jnp.int32, (CHUNK, CHUNK), 0)
    colc = lax.broadcasted_iota(jnp.int32, (CHUNK, CHUNK), 1)
    tri = jnp.where((colc >= rowc) if reverse else (colc <= rowc), 1.0, 0.0).astype(BF16)
    lw_hi, lw_lo = _split2(lw)
    cum = jnp.concatenate([_dot(tri, lw_hi[c:c + CHUNK]) + _dot(tri, lw_lo[c:c + CHUNK])
                           for c in range(0, nrow, CHUNK)], axis=0)
    tots =[jnp.sum(lw[c * CHUNK:(c + 1) * CHUNK], axis=0, keepdims=True) for c in range(nrow // CHUNK)]
    tot = jnp.concatenate([jnp.broadcast_to(t, (CHUNK, w)) for t in tots], axis=0)
    e_rest = jnp.exp(tot - cum)
    e_neg = jnp.exp(-cum)
    beta = kk * a
    out = dict(rt=r * jnp.exp(cum), at=-kk * jnp.exp(cum - lw), kh=kd * e_neg, bh=beta * e_neg,
               kw=kd * e_rest, bw=beta * e_rest, v=v, wc=[jnp.exp(t) for t in tots], bonus=bonus)
    if with_gate:
        out["gate"] = _dot(jax.nn.sigmoid(ps[:, c_gate:c_gate + GATE_RANK]).astype(BF16), prm["gup"][...])
    return out


def _rwkv_chains(preps, dirs, s_ref):
    gw = RWKV_GW
    m0 = lax.broadcasted_iota(jnp.int32, (1, gw), 1) // HEAD_DIM
    t_idx = lax.broadcasted_iota(jnp.int32, (CHUNK, gw), 0)
    s_idx = lax.broadcasted_iota(jnp.int32, (CHUNK, gw), 1) % CHUNK
    eye = jnp.where(s_idx == t_idx, 1.0, 0.0)
    strict = [s_idx < t_idx, s_idx > t_idx]
    incl = [s_idx <= t_idx, s_idx >= t_idx]
    bd_r = lax.broadcasted_iota(jnp.int32, (gw, gw), 0) // HEAD_DIM
    bd_c = lax.broadcasted_iota(jnp.int32, (gw, gw), 1) // HEAD_DIM
    bdmask = bd_r == bd_c

    ndir = len(preps)
    nck = preps[0]["v"].shape[0] // CHUNK
    chains = [(d, ck, pr) for ck in range(nck) for pr in range(RWKV_NG) for d in range(ndir)]

    def part(c, name):
        d, ck, pr = c
        return preps[d][name][ck * CHUNK:(ck + 1) * CHUNK, pr * gw:(pr + 1) * gw]

    ab, ak = {}, {}
    for c in chains:
        x = jnp.concatenate([part(c, "at"), part(c, "rt")], axis=0).astype(BF16)
        abk = _dot_nt(x, jnp.concatenate([_expand(part(c, "bh"), m0), _expand(part(c, "kh"), m0)], axis=0))
        ab[c], ak[c] = abk[:, :RWKV_HG * CHUNK], abk[:, RWKV_HG * CHUNK:]
    nmat, nb, a_rb, av, tinv = {}, {}, {}, {}, {}
    for c in chains:
        d = dirs[c[0]]
        nmat[c] = jnp.where(strict[d], ab[c][:CHUNK], 0.0)
        nb[c] = nmat[c].astype(BF16)
        a_rb[c] = jnp.where(incl[d], ab[c][CHUNK:], 0.0).astype(BF16)
        a_k = jnp.concatenate([jnp.where(strict[d], ak[c][:CHUNK], 0.0),
                               jnp.where(incl[d], ak[c][CHUNK:], 0.0)], axis=0).astype(BF16)
        av[c] = _dot(a_k, _expand(part(c, "v"), m0))
        tinv[c] = eye + jnp.where(_level_mask(t_idx, s_idx, 1, d == 1), nmat[c], 0.0)
    b = 2
    while b < CHUNK:
        e = {c: _dot(nb[c], _expand(tinv[c], m0)) for c in chains}
        for c in chains:
            f = _dot(tinv[c].astype(BF16), _expand(e[c], m0))
            tinv[c] = tinv[c] + jnp.where(_level_mask(t_idx, s_idx, b, dirs[c[0]] == 1), f, 0.0)
        b *= 2
    tu = {c: _dot(tinv[c].astype(BF16),
                  jnp.concatenate([_expand(part(c, "at"), m0), _expand(av[c][:CHUNK], m0)], axis=1))
          for c in chains}
    qy = {c: _dot(a_rb[c], jnp.concatenate([_expand(tu[c][:, :gw], m0), _expand(tu[c][:, gw:], m0)], axis=1))
          for c in chains}
    mt, bt = {}, {}
    for c in chains:
        bw = part(c, "bw").astype(BF16)
        ut_bw = _dot_tn(tu[c][:, :gw].astype(BF16), bw)
        ulv = jnp.concatenate([tu[c][:, gw:], part(c, "v")], axis=0).astype(BF16)
        ulv_bk = _dot_tn(ulv, jnp.concatenate([bw, part(c, "kw").astype(BF16)], axis=0))
        mt[c] = jnp.where(bdmask, ut_bw, 0.0).astype(BF16)
        bt[c] = jnp.where(bdmask, ulv_bk, 0.0)
    state = {(d, pr): s_ref[dirs[d], pr] for d in range(ndir) for pr in range(RWKV_NG)}
    ys = {}
    for step in range(nck):
        for pr in range(RWKV_NG):
            for d in range(ndir):
                c = (d, step if dirs[d] == 0 else nck - 1 - step, pr)
                s0 = state[(d, pr)]
                s0b = s0.astype(BF16)
                q = part(c, "rt") + qy[c][:, :gw]
                ys[c] = _dot_nt(q.astype(BF16), s0b) + qy[c][:, gw:] + av[c][CHUNK:]
                wc = preps[d]["wc"][c[1]][:, pr * gw:(pr + 1) * gw]
                state[(d, pr)] = s0 * wc + _dot(s0b, mt[c]) + bt[c]
    for (d, pr), s_new in state.items():
        s_ref[dirs[d], pr] = s_new
    return [jnp.concatenate([jnp.concatenate([ys[(d, ck, pr)] for pr in range(RWKV_NG)], axis=1)
                             for ck in range(nck)], axis=0) for d in range(ndir)]


def _rwkv_kernel(pf_ref, pfp_ref, pfn_ref, pb_ref, pbp_ref, pbn_ref,
                 mup_ref, mun_ref, w0_ref, wup_ref, a0_ref, aup_ref, gup_ref,
                 kk_ref, ka_ref, rk_ref, bd1_ref,
                 yf_ref, bonf_ref, g_ref, yb_ref, bonb_ref, s_ref, *, nc):
    i = pl.program_id(1)

    @pl.when(i == 0)
    def _():
        s_ref[...] = jnp.zeros_like(s_ref)

    prm = dict(w0=w0_ref, wup=wup_ref, a0=a0_ref, aup=aup_ref, gup=gup_ref,
               k_k=kk_ref, k_a=ka_ref, r_k=rk_ref, bd1=bd1_ref)
    nrow = pf_ref.shape[0]
    span = min(CHUNK, nrow)
    rows = lax.broadcasted_iota(jnp.int32, (span, span), 0)
    cols = lax.broadcasted_iota(jnp.int32, (span, span), 1)
    take_prev = jnp.where(cols == rows - 1, 1.0, 0.0).astype(BF16)
    take_next = jnp.where(cols == rows + 1, 1.0, 0.0).astype(BF16)
    sub = lax.broadcasted_iota(jnp.int32, (SUBLANES, 1), 0)

    def shifted(main_ref, prev_ref, next_ref, c):
        pm = main_ref[...]
        p = pm.astype(F32)
        edge_prev = prev_ref[...].astype(F32)[BF16_SUBLANES - 1:, :] * jnp.where(c > 0, 1.0, 0.0)
        edge_next = next_ref[...].astype(F32)[:1, :] * jnp.where(c < nc - 1, 1.0, 0.0)
        prevs, nexts = [], []
        for lo in range(0, nrow, span):
            hi = lo + span
            before = edge_prev if lo == 0 else p[lo - 1:lo]
            after = edge_next if hi == nrow else p[hi:hi + 1]
            pp = _dot(take_prev, pm[lo:hi])
            pn = _dot(take_next, pm[lo:hi])
            prevs += [jnp.where(sub == 0, before, pp[:SUBLANES]), pp[SUBLANES:]]
            nexts += [pn[:span - SUBLANES], jnp.where(sub == SUBLANES - 1, after, pn[span - SUBLANES:])]
        p_prev = jnp.concatenate(prevs, axis=0)
        p_next = jnp.concatenate(nexts, axis=0)
        return p + mup_ref[...] * (p_prev - p) + mun_ref[...] * (p_next - p)

    preps = [_rwkv_prep(shifted(pf_ref, pfp_ref, pfn_ref, i), 0, prm, True),
             _rwkv_prep(shifted(pb_ref, pbp_ref, pbn_ref, nc - 1 - i), 1, prm, False)]
    y_f, y_b = _rwkv_chains(preps, [0, 1], s_ref)
    yf_ref[...] = y_f
    bonf_ref[...] = preps[0]["bonus"].astype(BF16)
    g_ref[...] = preps[0]["gate"].astype(BF16)
    yb_ref[...] = y_b
    bonb_ref[...] = preps[1]["bonus"].astype(BF16)


def _rwkv(prw, batch, seq, mu_prev, mu_next, w0_f, w_up_f, w0_b, w_up_b, a0_f, a_up_f, a0_b, a_up_b,
          g_up, k_k, k_a, r_k):
    n = batch * seq
    cps = RWKV_CHUNKS_PER_STEP if seq % (CHUNK * RWKV_CHUNKS_PER_STEP) == 0 else 1
    nrow = cps * CHUNK
    nc = seq // nrow
    sub = nrow // BF16_SUBLANES
    nsub = n // BF16_SUBLANES
    w = RWKV_WIDTH
    zd = jnp.zeros((DECAY_RANK, w), F32)
    zi = jnp.zeros((ICLR_RANK, w), F32)
    wup = jnp.stack([jnp.concatenate([w_up_f, zd], 0), jnp.concatenate([zd, w_up_b], 0)]).astype(BF16)
    aup = jnp.stack([jnp.concatenate([a_up_f, zi], 0), jnp.concatenate([zi, a_up_b], 0)]).astype(BF16)
    w0 = jnp.stack([w0_f, w0_b])
    a0 = jnp.stack([a0_f, a0_b])
    bd1 = _block_diag_heads(1.0)

    def fwd_c(b, i):
        return i

    def bwd_c(b, i):
        return nc - 1 - i

    def main(cf):
        return lambda b, i: (b * nc + cf(b, i), 0)

    def prev(cf):
        return lambda b, i: (jnp.maximum((b * nc + cf(b, i)) * sub - 1, 0), 0)

    def nxt(cf):
        return lambda b, i: (jnp.minimum((b * nc + cf(b, i)) * sub + sub, nsub - 1), 0)

    const2 = lambda b, i: (0, 0)
    const3 = lambda b, i: (0, 0, 0)
    pspec = lambda f: pl.BlockSpec((nrow, RWKV_COLS), f)
    nspec = lambda f: pl.BlockSpec((BF16_SUBLANES, RWKV_COLS), f)
    ospec = lambda f: pl.BlockSpec((nrow, w), f)
    vec = pl.BlockSpec((1, w), const2)
    out_sds = jax.ShapeDtypeStruct((n, w), F32)
    out_bf = jax.ShapeDtypeStruct((n, w), BF16)
    return pl.pallas_call(
        functools.partial(_rwkv_kernel, nc=nc),
        grid=(batch, nc),
        in_specs=[
            pspec(main(fwd_c)), nspec(prev(fwd_c)), nspec(nxt(fwd_c)),
            pspec(main(bwd_c)), nspec(prev(bwd_c)), nspec(nxt(bwd_c)),
            pl.BlockSpec((1, RWKV_COLS), const2), pl.BlockSpec((1, RWKV_COLS), const2),
            pl.BlockSpec((2, w), const2), pl.BlockSpec((2, 2 * DECAY_RANK, w), const3),
            pl.BlockSpec((2, w), const2), pl.BlockSpec((2, 2 * ICLR_RANK, w), const3),
            pl.BlockSpec((GATE_RANK, w), const2),
            vec, vec, vec,
            pl.BlockSpec((MXU_DIM, MXU_DIM), const2),
        ],
        out_specs=[ospec(main(fwd_c)), ospec(main(fwd_c)), ospec(main(fwd_c)),
                   ospec(main(bwd_c)), ospec(main(bwd_c))],
        out_shape=[out_sds, out_bf, out_bf, out_sds, out_bf],
        scratch_shapes=[pltpu.VMEM((2, RWKV_NG, RWKV_GW, RWKV_GW), F32)],
        compiler_params=pltpu.CompilerParams(
            dimension_semantics=("arbitrary", "arbitrary"), vmem_limit_bytes=VMEM_LIMIT_BYTES),
        name="rwkv7",
    )(prw, prw, prw, prw, prw, prw,
      mu_prev.reshape(1, RWKV_COLS), mu_next.reshape(1, RWKV_COLS), w0, wup, a0, aup,
      g_up.astype(BF16), k_k.reshape(1, w), k_a.reshape(1, w), r_k.reshape(1, w), bd1)


def _na_kernel(q_ref, k_ref, v_ref, bias_ref, o_ref, *, rows, kh, rblk):
    j = pl.program_id(1)
    lane = lax.broadcasted_iota(jnp.int32, (1, LANES), 1)
    m0 = lane < HEAD_DIM
    zero = jnp.zeros((), BF16)
    npairs = NA_HEADS // PAIR
    chains = [(r, pr) for r in range(rblk) for pr in range(npairs)]
    starts, deltas = [], []
    for r in range(rblk):
        i = j * rblk + r
        start = jnp.clip(i - kh // 2, 0, rows - kh)
        starts.append(pl.multiple_of(start * GRID_W, GRID_W))
        deltas.append(i - start)
    s = {}
    for (r, pr) in chains:
        sl = slice(pr * LANES, (pr + 1) * LANES)
        qp = q_ref[r * GRID_W:(r + 1) * GRID_W, sl]
        qs = jnp.concatenate([jnp.where(m0, qp, zero), jnp.where(m0, zero, qp)], axis=0)
        kb = k_ref[pl.ds(starts[r], kh * GRID_W), sl]
        s[(r, pr)] = _dot_nt(qs, kb) + bias_ref[deltas[r], pr]
    e, l = {}, {}
    for c in chains:
        m = jnp.max(s[c], axis=-1, keepdims=True)
        p = jnp.exp(s[c] - m)
        l[c] = jnp.sum(p, axis=-1, keepdims=True)
        e[c] = p.astype(BF16)
    for r in range(rblk):
        outs = []
        for pr in range(npairs):
            sl = slice(pr * LANES, (pr + 1) * LANES)
            vb = v_ref[pl.ds(starts[r], kh * GRID_W), sl]
            pv = _dot(e[(r, pr)], vb) / l[(r, pr)]
            outs.append(jnp.where(m0, pv[:GRID_W], pv[GRID_W:]))
        o_ref[r * GRID_W:(r + 1) * GRID_W, :] = jnp.concatenate(outs, axis=1).astype(BF16)


def _na_bias_table(rpb, kh):
    cols = np.arange(GRID_W)
    col_start = np.clip(cols - NA_WIN_COLS // 2, 0, GRID_W - NA_WIN_COLS)
    col_mask = (cols[None, :] >= col_start[:, None]) & (cols[None, :] < col_start[:, None] + NA_WIN_COLS)
    col_off = np.clip(cols[None, :] - cols[:, None] + NA_WIN_COLS - 1, 0, 2 * NA_WIN_COLS - 2)
    delta = np.arange(kh)
    row_off = np.arange(kh)[None, :] - delta[:, None] + NA_MAX_WIN_ROWS - 1
    col_sel = (col_off[None] == np.arange(2 * NA_WIN_COLS - 1)[:, None, None]).astype(np.float32)
    row_sel = (row_off[:, :, None] == np.arange(2 * NA_MAX_WIN_ROWS - 1)).astype(np.float32)
    hi = lax.Precision.HIGHEST
    t = jnp.einsum("hoc,cqk->hoqk", rpb.astype(F32), col_sel, precision=hi)
    t = jnp.einsum("dro,hoqk->dhqrk", row_sel, t, precision=hi)
    t = jnp.where(col_mask[None, None, :, None, :], t, NEG_INF)
    return t.reshape(kh, NA_HEADS // PAIR, PAIR * GRID_W, kh * GRID_W)


def _natten(q, k, v, rpb, batch, seq):
    n = batch * seq
    rows = seq // GRID_W
    kh = min(NA_MAX_WIN_ROWS, rows)
    bias = _na_bias_table(rpb, kh)
    rblk = NA_ROWS_PER_STEP if rows % NA_ROWS_PER_STEP == 0 else 1
    nblk = rows // rblk

    return pl.pallas_call(
        functools.partial(_na_kernel, rows=rows, kh=kh, rblk=rblk),
        grid=(batch, nblk),
        in_specs=[
            pl.BlockSpec((rblk * GRID_W, NA_WIDTH), lambda b, i: (b * nblk + i, 0)),
            pl.BlockSpec((seq, NA_WIDTH), lambda b, i: (b, 0)),
            pl.BlockSpec((seq, NA_WIDTH), lambda b, i: (b, 0)),
            pl.BlockSpec((kh, NA_HEADS // PAIR, PAIR * GRID_W, kh * GRID_W), lambda b, i: (0, 0, 0, 0),
                         pipeline_mode=pl.Buffered(1)),
        ],
        out_specs=pl.BlockSpec((rblk * GRID_W, NA_WIDTH), lambda b, i: (b * nblk + i, 0)),
        out_shape=jax.ShapeDtypeStruct((n, NA_WIDTH), BF16),
        compiler_params=pltpu.CompilerParams(
            dimension_semantics=("parallel", "arbitrary"), vmem_limit_bytes=VMEM_LIMIT_BYTES),
        name="natten2d",
    )(q, k, v, bias)


def _route_t(logits_t):
    tm = logits_t.shape[1]
    sub = lax.broadcasted_iota(jnp.int32, (SUBLANES, tm), 0)
    ninf = -jnp.inf
    gvalid = sub < N_GROUPS
    gl = jnp.where(gvalid, logits_t[0:SUBLANES], ninf)
    gmax = jnp.max(gl, axis=0, keepdims=True)
    g_sel = jnp.min(jnp.where(gl == gmax, sub, SUBLANES), axis=0, keepdims=True)
    p_group = 1.0 / jnp.sum(jnp.where(gvalid, jnp.exp(gl - gmax), 0.0), axis=0, keepdims=True)
    ev = logits_t[SUBLANES:2 * SUBLANES]
    for g in range(1, N_GROUPS):
        lo = SUBLANES + g * EXPERTS_PER_GROUP
        ev = jnp.where(g_sel == g, logits_t[lo:lo + EXPERTS_PER_GROUP], ev)
    m1 = jnp.max(ev, axis=0, keepdims=True)
    i1 = jnp.min(jnp.where(ev == m1, sub, SUBLANES), axis=0, keepdims=True)
    ev2 = jnp.where(sub == i1, ninf, ev)
    m2 = jnp.max(ev2, axis=0, keepdims=True)
    i2 = jnp.min(jnp.where(ev2 == m2, sub, SUBLANES), axis=0, keepdims=True)
    t = jnp.exp(m2 - m1)
    w1 = p_group / (1.0 + t)
    w2 = p_group * t / (1.0 + t)
    e1 = g_sel * EXPERTS_PER_GROUP + i1
    e2 = g_sel * EXPERTS_PER_GROUP + i2
    rec = jnp.where(sub == 0, e1.astype(F32),
                    jnp.where(sub == 1, e2.astype(F32),
                              jnp.where(sub == 2, w1, jnp.where(sub == 3, w2, 0.0))))
    return rec, e1, e2


def _out_kernel(x_ref, yf_ref, yb_ref, bonf_ref, bonb_ref, g_ref, yna_ref, gate_ref,
                lng_ref, lnb_ref, bgate_ref, worw_ref, wona_ref, wout_ref, n2g_ref, wrh_ref, wrl_ref, br_ref,
                bd_ref, x1_ref, h2_ref, rt_ref, cnt_ref):
    bd = bd_ref[...]
    y = yf_ref[...] + yb_ref[...]
    yc = y - _seg_dot(y, bd)
    var = _grouped_dot((yc * yc).astype(BF16), bd)
    yn = yc * lax.rsqrt(var + GN_EPS) * lng_ref[...] + lnb_ref[...]
    ya = (yn + bonf_ref[...].astype(F32) + bonb_ref[...].astype(F32)) * g_ref[...].astype(F32)
    ya_o = _dot(ya.astype(BF16), worw_ref[...])
    yb_o = _dot(yna_ref[...], wona_ref[...])
    gl = gate_ref[...].astype(F32) + bgate_ref[...]
    mix = jax.nn.sigmoid(gl[:, :D_MODEL]) * ya_o + jax.nn.sigmoid(gl[:, D_MODEL:]) * yb_o
    x1 = x_ref[...] + _dot(mix.astype(BF16), wout_ref[...])
    x1_ref[...] = x1
    ms = jnp.mean(x1 * x1, axis=-1, keepdims=True)
    h2 = x1 * lax.rsqrt(ms + RMS_EPS) * n2g_ref[...]
    h2_hi, h2_lo = _split2(h2)
    h2_ref[...] = h2_hi
    logits_t = (_dot_nt(wrh_ref[...], h2_hi) + _dot_nt(wrh_ref[...], h2_lo)
                + _dot_nt(wrl_ref[...], h2_hi) + br_ref[...])
    rec, e1, e2 = _route_t(logits_t)
    rt_ref[...] = rec
    rows_e = lax.broadcasted_iota(jnp.int32, (N_EXPERTS, rec.shape[1]), 0)
    chosen = jnp.where((rows_e == e1) | (rows_e == e2), 1.0, 0.0)
    cnt_ref[0] = jnp.broadcast_to(jnp.sum(chosen, axis=1, keepdims=True), (N_EXPERTS, LANES))


def _out_proj(xf, yf, yb, bonf, bonb, g, yna, gate, lnx_g, lnx_b, b_gate, w_o_rwkv, w_o_na, w_out,
              norm2_g, w_rg, b_rg, w_re, b_re, tm):
    n = xf.shape[0]
    w = RWKV_WIDTH
    assert EXPERTS_PER_GROUP == SUBLANES and N_GROUPS <= SUBLANES
    gpad = SUBLANES - N_GROUPS
    wr_t = jnp.concatenate([w_rg.T, jnp.zeros((gpad, D_MODEL), F32), w_re.T], axis=0)
    wr_hi = wr_t.astype(BF16)
    wr_lo = (wr_t - wr_hi.astype(F32)).astype(BF16)
    br = jnp.concatenate([b_rg, jnp.zeros((gpad,), F32), b_re]).reshape(ROUTER_ROWS, 1)
    bd = _block_diag_heads(1.0 / HEAD_DIM)
    const = lambda i: (0, 0)
    row = lambda i: (i, 0)
    rs = lambda c: pl.BlockSpec((tm, c), row)
    cs = lambda r, c: pl.BlockSpec((r, c), const)
    return pl.pallas_call(
        _out_kernel,
        grid=(n // tm,),
        in_specs=[rs(D_MODEL), rs(w), rs(w), rs(w), rs(w), rs(w), rs(NA_WIDTH), rs(GATE_COLS),
                  cs(1, w), cs(1, w), cs(1, GATE_COLS), cs(w, D_MODEL), cs(NA_WIDTH, D_MODEL),
                  cs(D_MODEL, D_MODEL), cs(1, D_MODEL), cs(ROUTER_ROWS, D_MODEL), cs(ROUTER_ROWS, D_MODEL),
                  cs(ROUTER_ROWS, 1), cs(MXU_DIM, MXU_DIM)],
        out_specs=[rs(D_MODEL), rs(D_MODEL), pl.BlockSpec((ROUTE_REC, tm), lambda i: (0, i)),
                   pl.BlockSpec((1, N_EXPERTS, LANES), lambda i: (i, 0, 0))],
        out_shape=[jax.ShapeDtypeStruct((n, D_MODEL), F32),
                   jax.ShapeDtypeStruct((n, D_MODEL), BF16),
                   jax.ShapeDtypeStruct((ROUTE_REC, n), F32),
                   jax.ShapeDtypeStruct((n // tm, N_EXPERTS, LANES), F32)],
        compiler_params=pltpu.CompilerParams(
            dimension_semantics=("parallel",), vmem_limit_bytes=VMEM_LIMIT_BYTES),
        name="out_proj",
    )(xf, yf, yb, bonf, bonb, g, yna, gate,
      lnx_g.reshape(1, w), lnx_b.reshape(1, w), b_gate.reshape(1, GATE_COLS),
      w_o_rwkv.astype(BF16), w_o_na.astype(BF16), w_out.astype(BF16),
      norm2_g.reshape(1, D_MODEL), wr_hi, wr_lo, br, bd)


def _slot_rows(tm):
    rows = 2 * tm + N_EXPERTS * (SEG - 1)
    return -(-rows // LANES) * LANES


def _moe_plan(cnt, rb, nb_max, pieces_per_tile):
    ntiles = cnt.shape[0]
    e_before = jnp.asarray(np.tril(np.ones((N_EXPERTS, N_EXPERTS), np.int32), -1))
    t_before = jnp.asarray(np.tril(np.ones((ntiles, ntiles), np.int32), -1))
    pc = ((cnt + SEG - 1) // SEG) * SEG
    loc = jnp.sum(pc[:, None, :] * e_before[None], axis=2)
    within = jnp.sum(t_before[:, :, None] * pc[None], axis=1)
    filled = jnp.sum(pc, axis=0)
    region = ((filled + rb - 1) // rb) * rb
    region_off = jnp.sum(region[None, :] * e_before, axis=1)
    gbase = region_off[None, :] + within
    blk_end = (region_off + region) // rb
    nused = blk_end[-1:]
    blk = jnp.minimum(jnp.arange(nb_max, dtype=jnp.int32), nused - 1)
    blk_expert = jnp.minimum(jnp.sum((blk_end[None, :] <= blk[:, None]).astype(jnp.int32), axis=1),
                             N_EXPERTS - 1)
    n16 = pc // SEG
    end16 = (loc + pc) // SEG
    piece = jnp.arange(pieces_per_tile, dtype=jnp.int32)
    seg_of_piece = jnp.minimum(jnp.sum((end16[:, None, :] <= piece[None, :, None]).astype(jnp.int32), axis=2),
                               N_EXPERTS - 1)
    onehot = (seg_of_piece[:, :, None] == jnp.arange(N_EXPERTS)[None, None, :]).astype(jnp.int32)
    piece_row = jnp.sum(onehot * (gbase - loc)[:, None, :], axis=2) + piece[None, :] * SEG
    i32 = lambda a: a.reshape(-1).astype(jnp.int32)
    return (loc, i32(piece_row), i32(jnp.sum(n16, axis=1)), i32(blk_expert), i32(nused),
            i32(region_off + filled), i32((region - filled) // SEG))


def _piece_copies(local_ref, global_ref, row_s, tile, count, sem, outbound, pieces_per_tile):
    def one(p, carry):
        loc_rows = local_ref.at[pl.ds(pl.multiple_of(p * SEG, SEG), SEG)]
        glob_rows = global_ref.at[pl.ds(pl.multiple_of(row_s[tile * pieces_per_tile + p], SEG), SEG)]
        if outbound:
            pltpu.make_async_copy(loc_rows, glob_rows, sem).start()
        else:
            pltpu.make_async_copy(glob_rows, loc_rows, sem).start()
        return carry

    lax.fori_loop(0, count, one, 0)


def _wait_copies(src_ref, dst_ref, count, sem):
    def one(b, carry):
        pltpu.make_async_copy(src_ref.at[pl.ds(0, SEG)], dst_ref.at[pl.ds(0, SEG)], sem).wait()
        return carry

    lax.fori_loop(0, count, one, 0)


def _dispatch_kernel(row_s, npiece_s, tail_s, tailn_s, nused_s, h_ref, rt_ref, locc_ref, hs_ref,
                     sorted_ref, zero_ref, sems, *, nb_max, rb):
    tile = pl.program_id(0)
    ntiles = pl.num_programs(0)
    slot = tile % 2
    sorted_ref = sorted_ref.at[slot]
    sem = sems.at[slot]
    pieces_per_tile = sorted_ref.shape[0] // SEG

    @pl.when(tile >= 2)
    def _():
        _wait_copies(sorted_ref, hs_ref, npiece_s[tile - 2], sem)

    tm = h_ref.shape[0]
    rt = rt_ref[...]
    e1 = rt[0:1].astype(jnp.int32)
    e2 = rt[1:2].astype(jnp.int32)
    rows_e = lax.broadcasted_iota(jnp.int32, (N_EXPERTS, tm), 0)
    ind1 = rows_e == e1
    ind2 = rows_e == e2
    ind = jnp.where(ind1 | ind2, 1.0, 0.0).astype(BF16)
    earlier = jnp.where(lax.broadcasted_iota(jnp.int32, (tm, tm), 0) < lax.broadcasted_iota(jnp.int32, (tm, tm), 1),
                        1.0, 0.0).astype(BF16)
    pos = locc_ref[0] + _dot(ind, earlier)
    pos1 = jnp.sum(jnp.where(ind1, pos, 0.0), axis=0, keepdims=True).astype(jnp.int32)
    pos2 = jnp.sum(jnp.where(ind2, pos, 0.0), axis=0, keepdims=True).astype(jnp.int32)
    row_id = lax.broadcasted_iota(jnp.int32, (sorted_ref.shape[0], tm), 0)
    onehot = jnp.where((row_id == pos1) | (row_id == pos2), 1.0, 0.0).astype(BF16)
    sorted_ref[...] = _dot(onehot, h_ref[...]).astype(BF16)
    count = npiece_s[tile]
    _piece_copies(sorted_ref, hs_ref, row_s, tile, count, sem, True, pieces_per_tile)

    @pl.when(tile == ntiles - 1)
    def _():
        zero_ref[...] = jnp.zeros_like(zero_ref)

        def tail(e, total):
            def piece(b, carry):
                pltpu.make_async_copy(
                    zero_ref.at[pl.ds(0, SEG)],
                    hs_ref.at[pl.ds(pl.multiple_of(tail_s[e] + b * SEG, SEG), SEG)], sems.at[2]).start()
                return carry

            lax.fori_loop(0, tailn_s[e], piece, 0)
            return total + tailn_s[e]

        n_tail = lax.fori_loop(0, N_EXPERTS, tail, 0)

        def block(b, carry):
            pltpu.make_async_copy(zero_ref, hs_ref.at[pl.ds(pl.multiple_of(b * rb, rb), rb)], sems.at[3]).start()
            return carry

        lax.fori_loop(nused_s[0], nb_max, block, 0)
        _wait_copies(zero_ref, hs_ref, n_tail, sems.at[2])

        def block_wait(b, carry):
            pltpu.make_async_copy(zero_ref, hs_ref.at[pl.ds(0, rb)], sems.at[3]).wait()
            return carry

        lax.fori_loop(nused_s[0], nb_max, block_wait, 0)
        _wait_copies(sorted_ref, hs_ref, count, sem)

        @pl.when(tile >= 1)
        def _():
            _wait_copies(sorted_ref, hs_ref, npiece_s[tile - 1], sems.at[1 - slot])


def _ffn_kernel(be_s, nused_s, hs_ref, wg_ref, wu_ref, wd_ref, o_ref, wgb_ref, wub_ref, wdb_ref):
    b = pl.program_id(0)
    used = b < nused_s[0]

    @pl.when(used & ((b == 0) | (be_s[b] != be_s[jnp.maximum(b - 1, 0)])))
    def _():
        wgb_ref[...] = wg_ref[0].astype(BF16)
        wub_ref[...] = wu_ref[0].astype(BF16)
        wdb_ref[...] = wd_ref[0].astype(BF16)

    @pl.when(used)
    def _():
        x = hs_ref[...]
        gate = _dot(x, wgb_ref[...])
        up = _dot(x, wub_ref[...])
        he = (gate * jax.nn.sigmoid(gate) * up).astype(BF16)
        o_ref[...] = _dot(he, wdb_ref[...]).astype(BF16)

    @pl.when(jnp.logical_not(used))
    def _():
        o_ref[...] = jnp.zeros_like(o_ref)


def _combine_kernel(row_s, npiece_s, x1_ref, rc_ref, locr_ref, ho_ref, out_ref, obuf_ref, sems):
    tile = pl.program_id(0)
    ntiles = pl.num_programs(0)
    slot = tile % 2
    tm = x1_ref.shape[0]
    pieces_per_tile = obuf_ref.shape[1] // SEG

    @pl.when(tile == 0)
    def _():
        obuf_ref[...] = jnp.zeros_like(obuf_ref)
        _piece_copies(obuf_ref.at[0], ho_ref, row_s, 0, npiece_s[0], sems.at[0], False, pieces_per_tile)

    @pl.when(tile + 1 < ntiles)
    def _():
        _piece_copies(obuf_ref.at[1 - slot], ho_ref, row_s, tile + 1, npiece_s[tile + 1], sems.at[1 - slot],
                      False, pieces_per_tile)

    obuf_ref = obuf_ref.at[slot]
    rc = rc_ref[...]
    e1 = rc[:, 0:1].astype(jnp.int32)
    e2 = rc[:, 1:2].astype(jnp.int32)
    lane_e = lax.broadcasted_iota(jnp.int32, (tm, LANES), 1)
    ind1 = lane_e == e1
    ind2 = lane_e == e2
    ind = jnp.where(ind1 | ind2, 1.0, 0.0).astype(BF16)
    earlier = jnp.where(lax.broadcasted_iota(jnp.int32, (tm, tm), 1) < lax.broadcasted_iota(jnp.int32, (tm, tm), 0),
                        1.0, 0.0).astype(BF16)
    pos = locr_ref[0] + _dot(earlier, ind)
    pos1 = jnp.sum(jnp.where(ind1, pos, 0.0), axis=1, keepdims=True).astype(jnp.int32)
    pos2 = jnp.sum(jnp.where(ind2, pos, 0.0), axis=1, keepdims=True).astype(jnp.int32)
    row_id = lax.broadcasted_iota(jnp.int32, (tm, obuf_ref.shape[0]), 1)
    pick = jnp.where(row_id == pos1, rc[:, 2:3], jnp.where(row_id == pos2, rc[:, 3:4], 0.0)).astype(BF16)
    _wait_copies(ho_ref, obuf_ref, npiece_s[tile], sems.at[slot])
    out_ref[...] = x1_ref[...] + _dot(pick, obuf_ref[...])


def _moe(h2, x1, rt, cnt, w_gate_e, w_up_e, w_down_e, tm, rb):
    n = h2.shape[0]
    ntiles = n // tm
    j_rows = _slot_rows(tm)
    nb_max = -(-(2 * n + ntiles * N_EXPERTS * (SEG - 1) + N_EXPERTS * (rb - 1)) // rb)
    rmax = nb_max * rb
    loc, row_s, npiece_s, blk_expert, nused, tail_s, tailn_s = _moe_plan(
        cnt[:, :, 0].astype(jnp.int32), rb, nb_max, j_rows // SEG)
    loc_col = loc.astype(F32).reshape(ntiles, N_EXPERTS, 1)
    loc_row = jnp.pad(loc.astype(F32), ((0, 0), (0, LANES - N_EXPERTS))).reshape(ntiles, 1, LANES)
    any_spec = pl.BlockSpec(memory_space=pl.ANY)
    params = pltpu.CompilerParams(dimension_semantics=("arbitrary",), vmem_limit_bytes=VMEM_LIMIT_BYTES)

    hs = pl.pallas_call(
        functools.partial(_dispatch_kernel, nb_max=nb_max, rb=rb),
        grid_spec=pltpu.PrefetchScalarGridSpec(
            num_scalar_prefetch=5, grid=(ntiles,),
            in_specs=[pl.BlockSpec((tm, D_MODEL), lambda i, *_: (i, 0)),
                      pl.BlockSpec((ROUTE_REC, tm), lambda i, *_: (0, i)),
                      pl.BlockSpec((1, N_EXPERTS, 1), lambda i, *_: (i, 0, 0))],
            out_specs=any_spec,
            scratch_shapes=[pltpu.VMEM((2, j_rows, D_MODEL), BF16), pltpu.VMEM((rb, D_MODEL), BF16),
                            pltpu.SemaphoreType.DMA((4,))]),
        out_shape=jax.ShapeDtypeStruct((rmax, D_MODEL), BF16),
        compiler_params=params,
        name="moe_dispatch",
    )(row_s, npiece_s, tail_s, tailn_s, nused, h2, rt, loc_col)

    blk_row = lambda b, be, nu: (jnp.minimum(b, nu[0] - 1), 0)
    ho = pl.pallas_call(
        _ffn_kernel,
        grid_spec=pltpu.PrefetchScalarGridSpec(
            num_scalar_prefetch=2, grid=(nb_max,),
            in_specs=[pl.BlockSpec((rb, D_MODEL), blk_row),
                      pl.BlockSpec((1, D_MODEL, D_EXPERT), lambda b, be, nu: (be[b], 0, 0)),
                      pl.BlockSpec((1, D_MODEL, D_EXPERT), lambda b, be, nu: (be[b], 0, 0)),
                      pl.BlockSpec((1, D_EXPERT, D_MODEL), lambda b, be, nu: (be[b], 0, 0))],
            out_specs=pl.BlockSpec((rb, D_MODEL), lambda b, be, nu: (b, 0)),
            scratch_shapes=[pltpu.VMEM((D_MODEL, D_EXPERT), BF16), pltpu.VMEM((D_MODEL, D_EXPERT), BF16),
                            pltpu.VMEM((D_EXPERT, D_MODEL), BF16)]),
        out_shape=jax.ShapeDtypeStruct((rmax, D_MODEL), BF16),
        compiler_params=params,
        name="moe_ffn",
    )(blk_expert, nused, hs, w_gate_e, w_up_e, w_down_e)

    return pl.pallas_call(
        _combine_kernel,
        grid_spec=pltpu.PrefetchScalarGridSpec(
            num_scalar_prefetch=2, grid=(ntiles,),
            in_specs=[pl.BlockSpec((tm, D_MODEL), lambda i, *_: (i, 0)),
                      pl.BlockSpec((tm, ROUTE_REC), lambda i, *_: (i, 0)),
                      pl.BlockSpec((1, 1, LANES), lambda i, *_: (i, 0, 0)),
                      any_spec],
            out_specs=pl.BlockSpec((tm, D_MODEL), lambda i, *_: (i, 0)),
            scratch_shapes=[pltpu.VMEM((2, j_rows, D_MODEL), BF16), pltpu.SemaphoreType.DMA((2,))]),
        out_shape=jax.ShapeDtypeStruct((n, D_MODEL), F32),
        compiler_params=params,
        name="moe_combine",
    )(row_s, npiece_s, x1, rt.T, loc_row, ho)


def _row_tile(n, want):
    t = min(want, n)
    while n % t:
        t //= 2
    return t


def kernel(x, norm1_g, w_in, mu_prev, mu_next, w0_f, w_up_f, w0_b, w_up_b, a0_f, a_up_f, a0_b, a_up_b, g_up, k_k, k_a, r_k, lnx_g, lnx_b, q_gain, k_gain, rpb, b_gate, w_o_rwkv, w_o_na, w_out, norm2_g, w_router_group, b_router_group, w_router_expert, b_router_expert, w_gate_e, w_up_e, w_down_e):
    batch, seq, d = x.shape
    assert d == D_MODEL and seq % CHUNK == 0 and seq % GRID_W == 0
    n = batch * seq
    for l in range(norm1_g.shape[0]):
        xf = x.reshape(n, d)
        prw, q, k, v, gate = _in_proj(xf, norm1_g[l], w_in[l].astype(BF16), q_gain[l], k_gain[l],
                                      _row_tile(n, 512))
        yf, bonf, g, yb, bonb = _rwkv(prw, batch, seq, mu_prev[l], mu_next[l], w0_f[l], w_up_f[l],
                                      w0_b[l], w_up_b[l], a0_f[l], a_up_f[l], a0_b[l], a_up_b[l],
                                      g_up[l], k_k[l], k_a[l], r_k[l].reshape(-1))
        yna = _natten(q, k, v, rpb[l], batch, seq)
        tm = _row_tile(n, MOE_TM)
        x1, h2, rt, cnt = _out_proj(xf, yf, yb, bonf, bonb, g, yna, gate, lnx_g[l], lnx_b[l], b_gate[l],
                                    w_o_rwkv[l], w_o_na[l], w_out[l], norm2_g[l],
                                    w_router_group[l], b_router_group[l],
                                    w_router_expert[l], b_router_expert[l], tm)
        out = _moe(h2, x1, rt, cnt, w_gate_e[l], w_up_e[l], w_down_e[l], tm, MOE_RB)
        x = out.reshape(batch, seq, d)
    return x
```

```python
import functools

import numpy as np
import jax
import jax.numpy as jnp
from jax import lax
from jax.experimental import pallas as pl
from jax.experimental.pallas import tpu as pltpu

D_MODEL = 1024
HEAD_DIM = 64
RWKV_HEADS = 8
RWKV_WIDTH = RWKV_HEADS * HEAD_DIM
DECAY_RANK = 64
ICLR_RANK = 64
GATE_RANK = 128
NA_HEADS = 8
NA_WIDTH = NA_HEADS * HEAD_DIM
GRID_W = 64
NA_MAX_WIN_ROWS = 8
NA_WIN_COLS = 16
N_GROUPS = 4
EXPERTS_PER_GROUP = 8
N_EXPERTS = N_GROUPS * EXPERTS_PER_GROUP
D_EXPERT = 256
RMS_EPS = 1e-6
GN_EPS = 64e-5
NEG_INF = -1e30
RWKV_COLS = 3 * RWKV_WIDTH + 2 * DECAY_RANK + 2 * ICLR_RANK + GATE_RANK
NA_COLS = 3 * NA_WIDTH
GATE_COLS = 2 * D_MODEL
IN_COLS = RWKV_COLS + NA_COLS + GATE_COLS

LANES = 128
BF16_SUBLANES = 16
VMEM_LIMIT_BYTES = 56 * 1024 * 1024

CHUNK = 64
RWKV_CHUNKS_PER_STEP = 4
PAIR = LANES // HEAD_DIM
MXU_DIM = 256
RWKV_GW = LANES
RWKV_HG = RWKV_GW // HEAD_DIM
RWKV_NG = RWKV_WIDTH // RWKV_GW
EXP_M05 = float(np.exp(-0.5))
NA_ROWS_PER_STEP = 16
SUBLANES = 8
ROUTER_ROWS = SUBLANES + N_EXPERTS
ROUTE_REC = SUBLANES
MOE_TM = 512
MOE_RB = 1024
SEG = BF16_SUBLANES

F32 = jnp.float32
BF16 = jnp.bfloat16


def _dot(a, b):
    return jnp.dot(a, b, preferred_element_type=F32)


def _dot_nt(a, b):
    return lax.dot_general(a, b, (((1,), (1,)), ((), ())), preferred_element_type=F32)


def _dot_tn(a, b):
    return lax.dot_general(a, b, (((0,), (0,)), ((), ())), preferred_element_type=F32)


def _split2(z):
    hi = z.astype(BF16)
    lo = (z - hi.astype(F32)).astype(BF16)
    return hi, lo


def _grouped_dot(zb, bd):
    g = bd.shape[0]
    return jnp.concatenate([_dot(zb[:, j:j + g], bd) for j in range(0, zb.shape[1], g)], axis=1)


def _seg_dot(z, bd):
    hi, lo = _split2(z)
    return _grouped_dot(hi, bd) + _grouped_dot(lo, bd)


def _block_diag_heads(value):
    idx = np.arange(MXU_DIM) // HEAD_DIM
    return jnp.asarray((idx[:, None] == idx[None, :]).astype(np.float32) * value, dtype=BF16)


def _inproj_kernel(x_ref, g1_ref, w_ref, qg_ref, kg_ref, bd_ref,
                   prw_ref, q_ref, k_ref, v_ref, gate_ref):
    x = x_ref[...]
    ms = jnp.mean(x * x, axis=-1, keepdims=True)
    h = (x * lax.rsqrt(ms + RMS_EPS) * g1_ref[...]).astype(BF16)

    def proj(lo, hi):
        return _dot(h, w_ref[:, lo:hi])

    col = 0
    while col < RWKV_COLS:
        nxt = min(col + 512, RWKV_COLS)
        prw_ref[:, col:nxt] = proj(col, nxt).astype(BF16)
        col = nxt

    def head_rms(t, gain):
        msq = _grouped_dot((t * t).astype(BF16), bd_ref[...])
        return t * lax.rsqrt(msq + RMS_EPS) * gain

    c0 = RWKV_COLS
    q_ref[...] = head_rms(proj(c0, c0 + NA_WIDTH), qg_ref[...]).astype(BF16)
    k_ref[...] = head_rms(proj(c0 + NA_WIDTH, c0 + 2 * NA_WIDTH), kg_ref[...]).astype(BF16)
    v_ref[...] = proj(c0 + 2 * NA_WIDTH, c0 + 3 * NA_WIDTH).astype(BF16)
    c1 = c0 + NA_COLS
    for j in range(GATE_COLS // 512):
        gate_ref[:, j * 512:(j + 1) * 512] = proj(c1 + j * 512, c1 + (j + 1) * 512).astype(BF16)


def _in_proj(xf, g1, w_in_b, q_gain, k_gain, tm):
    n = xf.shape[0]
    qg = (jnp.tile(q_gain, NA_HEADS) * (HEAD_DIM ** -0.5)).reshape(1, NA_WIDTH)
    kg = jnp.tile(k_gain, NA_HEADS).reshape(1, NA_WIDTH)
    bd = _block_diag_heads(1.0 / HEAD_DIM)
    const = lambda i: (0, 0)
    row = lambda i: (i, 0)
    return pl.pallas_call(
        _inproj_kernel,
        grid=(n // tm,),
        in_specs=[
            pl.BlockSpec((tm, D_MODEL), row),
            pl.BlockSpec((1, D_MODEL), const),
            pl.BlockSpec((D_MODEL, IN_COLS), const),
            pl.BlockSpec((1, NA_WIDTH), const),
            pl.BlockSpec((1, NA_WIDTH), const),
            pl.BlockSpec((MXU_DIM, MXU_DIM), const),
        ],
        out_specs=[
            pl.BlockSpec((tm, RWKV_COLS), row),
            pl.BlockSpec((tm, NA_WIDTH), row),
            pl.BlockSpec((tm, NA_WIDTH), row),
            pl.BlockSpec((tm, NA_WIDTH), row),
            pl.BlockSpec((tm, GATE_COLS), row),
        ],
        out_shape=[
            jax.ShapeDtypeStruct((n, RWKV_COLS), BF16),
            jax.ShapeDtypeStruct((n, NA_WIDTH), BF16),
            jax.ShapeDtypeStruct((n, NA_WIDTH), BF16),
            jax.ShapeDtypeStruct((n, NA_WIDTH), BF16),
            jax.ShapeDtypeStruct((n, GATE_COLS), BF16),
        ],
        compiler_params=pltpu.CompilerParams(
            dimension_semantics=("parallel",), vmem_limit_bytes=VMEM_LIMIT_BYTES),
        name="in_proj",
    )(xf, g1.reshape(1, D_MODEL), w_in_b, qg, kg, bd)


def _expand(z, lane_head):
    return jnp.concatenate([jnp.where(lane_head == h, z, 0.0) for h in range(RWKV_HG)], axis=0).astype(BF16)


def _level_mask(t_idx, s_idx, b, reverse):
    same = (t_idx // (2 * b)) == (s_idx // (2 * b))
    t_hi = (t_idx % (2 * b)) >= b
    s_hi = (s_idx % (2 * b)) >= b
    if reverse:
        return same & jnp.logical_not(t_hi) & s_hi
    return same & t_hi & jnp.logical_not(s_hi)


def _rwkv_prep(ps, d, prm, with_gate):
    reverse = d == 1
    w = RWKV_WIDTH
    r = ps[:, 0:w]
    k = ps[:, w:2 * w]
    v = ps[:, 2 * w:3 * w]
    c_dec = 3 * w
    c_icl = c_dec + 2 * DECAY_RANK
    c_gate = c_icl + 2 * ICLR_RANK
    bd1 = prm["bd1"][...]

    w_raw = prm["w0"][d:d + 1, :] + _dot(jnp.tanh(ps[:, c_dec:c_icl]).astype(BF16), prm["wup"][d])
    a = jax.nn.sigmoid(prm["a0"][d:d + 1, :] + _dot(ps[:, c_icl:c_gate].astype(BF16), prm["aup"][d]))
    lw = -EXP_M05 * jax.nn.sigmoid(w_raw)
    kk = k * prm["k_k"][...]
    kk = kk * lax.rsqrt(_grouped_dot((kk * kk).astype(BF16), bd1) + 1e-12)
    kd = k * (1.0 + (a - 1.0) * prm["k_a"][...])
    bonus = _seg_dot(r * kd * prm["r_k"][...], bd1) * v

    nrow = ps.shape[0]
    rowc = lax.broadcasted_iota(jnp.int32, (CHUNK, CHUNK), 0)
    colc = lax.broadcasted_iota(jnp.int32, (CHUNK, CHUNK), 1)
    tri = jnp.where((colc >= rowc) if reverse else (colc <= rowc), 1.0, 0.0).astype(BF16)
    lw_hi, lw_lo = _split2(lw)
    cum = jnp.concatenate([_dot(tri, lw_hi[c:c + CHUNK]) + _dot(tri, lw_lo[c:c + CHUNK])
                           for c in range(0, nrow, CHUNK)], axis=0)
    tots =[jnp.sum(lw[c * CHUNK:(c + 1) * CHUNK], axis=0, keepdims=True) for c in range(nrow // CHUNK)]
    tot = jnp.concatenate([jnp.broadcast_to(t, (CHUNK, w)) for t in tots], axis=0)
    e_rest = jnp.exp(tot - cum)
    e_neg = jnp.exp(-cum)
    beta = kk * a
    out = dict(rt=r * jnp.exp(cum), at=-kk * jnp.exp(cum - lw), kh=kd * e_neg, bh=beta * e_neg,
               kw=kd * e_rest, bw=beta * e_rest, v=v, wc=[jnp.exp(t) for t in tots], bonus=bonus)
    if with_gate:
        out["gate"] = _dot(jax.nn.sigmoid(ps[:, c_gate:c_gate + GATE_RANK]).astype(BF16), prm["gup"][...])
    return out


def _rwkv_chains(preps, dirs, s_ref):
    gw = RWKV_GW
    m0 = lax.broadcasted_iota(jnp.int32, (1, gw), 1) // HEAD_DIM
    t_idx = lax.broadcasted_iota(jnp.int32, (CHUNK, gw), 0)
    s_idx = lax.broadcasted_iota(jnp.int32, (CHUNK, gw), 1) % CHUNK
    eye = jnp.where(s_idx == t_idx, 1.0, 0.0)
    strict = [s_idx < t_idx, s_idx > t_idx]
    incl = [s_idx <= t_idx, s_idx >= t_idx]
    bd_r = lax.broadcasted_iota(jnp.int32, (gw, gw), 0) // HEAD_DIM
    bd_c = lax.broadcasted_iota(jnp.int32, (gw, gw), 1) // HEAD_DIM
    bdmask = bd_r == bd_c

    ndir = len(preps)
    nck = preps[0]["v"].shape[0] // CHUNK
    chains = [(d, ck, pr) for ck in range(nck) for pr in range(RWKV_NG) for d in range(ndir)]

    def part(c, name):
        d, ck, pr = c
        return preps[d][name][ck * CHUNK:(ck + 1) * CHUNK, pr * gw:(pr + 1) * gw]

    ab, ak = {}, {}
    for c in chains:
        x = jnp.concatenate([part(c, "at"), part(c, "rt")], axis=0).astype(BF16)
        abk = _dot_nt(x, jnp.concatenate([_expand(part(c, "bh"), m0), _expand(part(c, "kh"), m0)], axis=0))
        ab[c], ak[c] = abk[:, :RWKV_HG * CHUNK], abk[:, RWKV_HG * CHUNK:]
    nmat, nb, a_rb, av, tinv = {}, {}, {}, {}, {}
    for c in chains:
        d = dirs[c[0]]
        nmat[c] = jnp.where(strict[d], ab[c][:CHUNK], 0.0)
        nb[c] = nmat[c].astype(BF16)
        a_rb[c] = jnp.where(incl[d], ab[c][CHUNK:], 0.0).astype(BF16)
        a_k = jnp.concatenate([jnp.where(strict[d], ak[c][:CHUNK], 0.0),
                               jnp.where(incl[d], ak[c][CHUNK:], 0.0)], axis=0).astype(BF16)
        av[c] = _dot(a_k, _expand(part(c, "v"), m0))
        tinv[c] = eye + jnp.where(_level_mask(t_idx, s_idx, 1, d == 1), nmat[c], 0.0)
    b = 2
    while b < CHUNK:
        e = {c: _dot(nb[c], _expand(tinv[c], m0)) for c in chains}
        for c in chains:
            f = _dot(tinv[c].astype(BF16), _expand(e[c], m0))
            tinv[c] = tinv[c] + jnp.where(_level_mask(t_idx, s_idx, b, dirs[c[0]] == 1), f, 0.0)
        b *= 2
    tu = {c: _dot(tinv[c].astype(BF16),
                  jnp.concatenate([_expand(part(c, "at"), m0), _expand(av[c][:CHUNK], m0)], axis=1))
          for c in chains}
    qy = {c: _dot(a_rb[c], jnp.concatenate([_expand(tu[c][:, :gw], m0), _expand(tu[c][:, gw:], m0)], axis=1))
          for c in chains}
    mt, bt = {}, {}
    for c in chains:
        bw = part(c, "bw").astype(BF16)
        ut_bw = _dot_tn(tu[c][:, :gw].astype(BF16), bw)
        ulv = jnp.concatenate([tu[c][:, gw:], part(c, "v")], axis=0).astype(BF16)
        ulv_bk = _dot_tn(ulv, jnp.concatenate([bw, part(c, "kw").astype(BF16)], axis=0))
        mt[c] = jnp.where(bdmask, ut_bw, 0.0).astype(BF16)
        bt[c] = jnp.where(bdmask, ulv_bk, 0.0)
    state = {(d, pr): s_ref[dirs[d], pr] for d in range(ndir) for pr in range(RWKV_NG)}
    ys = {}
    for step in range(nck):
        for pr in range(RWKV_NG):
            for d in range(ndir):
                c = (d, step if dirs[d] == 0 else nck - 1 - step, pr)
                s0 = state[(d, pr)]
                s0b = s0.astype(BF16)
                q = part(c, "rt") + qy[c][:, :gw]
                ys[c] = _dot_nt(q.astype(BF16), s0b) + qy[c][:, gw:] + av[c][CHUNK:]
                wc = preps[d]["wc"][c[1]][:, pr * gw:(pr + 1) * gw]
                state[(d, pr)] = s0 * wc + _dot(s0b, mt[c]) + bt[c]
    for (d, pr), s_new in state.items():
        s_ref[dirs[d], pr] = s_new
    return [jnp.concatenate([jnp.concatenate([ys[(d, ck, pr)] for pr in range(RWKV_NG)], axis=1)
                             for ck in range(nck)], axis=0) for d in range(ndir)]


def _rwkv_kernel(pf_ref, pfp_ref, pfn_ref, pb_ref, pbp_ref, pbn_ref,
                 mup_ref, mun_ref, w0_ref, wup_ref, a0_ref, aup_ref, gup_ref,
                 kk_ref, ka_ref, rk_ref, bd1_ref,
                 yf_ref, bonf_ref, g_ref, yb_ref, bonb_ref, s_ref, *, nc):
    i = pl.program_id(1)

    @pl.when(i == 0)
    def _():
        s_ref[...] = jnp.zeros_like(s_ref)

    prm = dict(w0=w0_ref, wup=wup_ref, a0=a0_ref, aup=aup_ref, gup=gup_ref,
               k_k=kk_ref, k_a=ka_ref, r_k=rk_ref, bd1=bd1_ref)
    nrow = pf_ref.shape[0]
    span = min(CHUNK, nrow)
    rows = lax.broadcasted_iota(jnp.int32, (span, span), 0)
    cols = lax.broadcasted_iota(jnp.int32, (span, span), 1)
    take_prev = jnp.where(cols == rows - 1, 1.0, 0.0).astype(BF16)
    take_next = jnp.where(cols == rows + 1, 1.0, 0.0).astype(BF16)
    sub = lax.broadcasted_iota(jnp.int32, (SUBLANES, 1), 0)

    def shifted(main_ref, prev_ref, next_ref, c):
        pm = main_ref[...]
        p = pm.astype(F32)
        edge_prev = prev_ref[...].astype(F32)[BF16_SUBLANES - 1:, :] * jnp.where(c > 0, 1.0, 0.0)
        edge_next = next_ref[...].astype(F32)[:1, :] * jnp.where(c < nc - 1, 1.0, 0.0)
        prevs, nexts = [], []
        for lo in range(0, nrow, span):
            hi = lo + span
            before = edge_prev if lo == 0 else p[lo - 1:lo]
            after = edge_next if hi == nrow else p[hi:hi + 1]
            pp = _dot(take_prev, pm[lo:hi])
            pn = _dot(take_next, pm[lo:hi])
            prevs += [jnp.where(sub == 0, before, pp[:SUBLANES]), pp[SUBLANES:]]
            nexts += [pn[:span - SUBLANES], jnp.where(sub == SUBLANES - 1, after, pn[span - SUBLANES:])]
        p_prev = jnp.concatenate(prevs, axis=0)
        p_next = jnp.concatenate(nexts, axis=0)
        return p + mup_ref[...] * (p_prev - p) + mun_ref[...] * (p_next - p)

    preps = [_rwkv_prep(shifted(pf_ref, pfp_ref, pfn_ref, i), 0, prm, True),
             _rwkv_prep(shifted(pb_ref, pbp_ref, pbn_ref, nc - 1 - i), 1, prm, False)]
    y_f, y_b = _rwkv_chains(preps, [0, 1], s_ref)
    yf_ref[...] = y_f
    bonf_ref[...] = preps[0]["bonus"].astype(BF16)
    g_ref[...] = preps[0]["gate"].astype(BF16)
    yb_ref[...] = y_b
    bonb_ref[...] = preps[1]["bonus"].astype(BF16)


def _rwkv(prw, batch, seq, mu_prev, mu_next, w0_f, w_up_f, w0_b, w_up_b, a0_f, a_up_f, a0_b, a_up_b,
          g_up, k_k, k_a, r_k):
    n = batch * seq
    cps = RWKV_CHUNKS_PER_STEP if seq % (CHUNK * RWKV_CHUNKS_PER_STEP) == 0 else 1
    nrow = cps * CHUNK
    nc = seq // nrow
    sub = nrow // BF16_SUBLANES
    nsub = n // BF16_SUBLANES
    w = RWKV_WIDTH
    zd = jnp.zeros((DECAY_RANK, w), F32)
    zi = jnp.zeros((ICLR_RANK, w), F32)
    wup = jnp.stack([jnp.concatenate([w_up_f, zd], 0), jnp.concatenate([zd, w_up_b], 0)]).astype(BF16)
    aup = jnp.stack([jnp.concatenate([a_up_f, zi], 0), jnp.concatenate([zi, a_up_b], 0)]).astype(BF16)
    w0 = jnp.stack([w0_f, w0_b])
    a0 = jnp.stack([a0_f, a0_b])
    bd1 = _block_diag_heads(1.0)

    def fwd_c(b, i):
        return i

    def bwd_c(b, i):
        return nc - 1 - i

    def main(cf):
        return lambda b, i: (b * nc + cf(b, i), 0)

    def prev(cf):
        return lambda b, i: (jnp.maximum((b * nc + cf(b, i)) * sub - 1, 0), 0)

    def nxt(cf):
        return lambda b, i: (jnp.minimum((b * nc + cf(b, i)) * sub + sub, nsub - 1), 0)

    const2 = lambda b, i: (0, 0)
    const3 = lambda b, i: (0, 0, 0)
    pspec = lambda f: pl.BlockSpec((nrow, RWKV_COLS), f)
    nspec = lambda f: pl.BlockSpec((BF16_SUBLANES, RWKV_COLS), f)
    ospec = lambda f: pl.BlockSpec((nrow, w), f)
    vec = pl.BlockSpec((1, w), const2)
    out_sds = jax.ShapeDtypeStruct((n, w), F32)
    out_bf = jax.ShapeDtypeStruct((n, w), BF16)
    return pl.pallas_call(
        functools.partial(_rwkv_kernel, nc=nc),
        grid=(batch, nc),
        in_specs=[
            pspec(main(fwd_c)), nspec(prev(fwd_c)), nspec(nxt(fwd_c)),
            pspec(main(bwd_c)), nspec(prev(bwd_c)), nspec(nxt(bwd_c)),
            pl.BlockSpec((1, RWKV_COLS), const2), pl.BlockSpec((1, RWKV_COLS), const2),
            pl.BlockSpec((2, w), const2), pl.BlockSpec((2, 2 * DECAY_RANK, w), const3),
            pl.BlockSpec((2, w), const2), pl.BlockSpec((2, 2 * ICLR_RANK, w), const3),
            pl.BlockSpec((GATE_RANK, w), const2),
            vec, vec, vec,
            pl.BlockSpec((MXU_DIM, MXU_DIM), const2),
        ],
        out_specs=[ospec(main(fwd_c)), ospec(main(fwd_c)), ospec(main(fwd_c)),
                   ospec(main(bwd_c)), ospec(main(bwd_c))],
        out_shape=[out_sds, out_bf, out_bf, out_sds, out_bf],
        scratch_shapes=[pltpu.VMEM((2, RWKV_NG, RWKV_GW, RWKV_GW), F32)],
        compiler_params=pltpu.CompilerParams(
            dimension_semantics=("arbitrary", "arbitrary"), vmem_limit_bytes=VMEM_LIMIT_BYTES),
        name="rwkv7",
    )(prw, prw, prw, prw, prw, prw,
      mu_prev.reshape(1, RWKV_COLS), mu_next.reshape(1, RWKV_COLS), w0, wup, a0, aup,
      g_up.astype(BF16), k_k.reshape(1, w), k_a.reshape(1, w), r_k.reshape(1, w), bd1)


def _na_kernel(q_ref, k_ref, v_ref, bias_ref, o_ref, *, rows, kh, rblk):
    j = pl.program_id(1)
    lane = lax.broadcasted_iota(jnp.int32, (1, LANES), 1)
    m0 = lane < HEAD_DIM
    zero = jnp.zeros((), BF16)
    npairs = NA_HEADS // PAIR
    chains = [(r, pr) for r in range(rblk) for pr in range(npairs)]
    starts, deltas = [], []
    for r in range(rblk):
        i = j * rblk + r
        start = jnp.clip(i - kh // 2, 0, rows - kh)
        starts.append(pl.multiple_of(start * GRID_W, GRID_W))
        deltas.append(i - start)
    s = {}
    for (r, pr) in chains:
        sl = slice(pr * LANES, (pr + 1) * LANES)
        qp = q_ref[r * GRID_W:(r + 1) * GRID_W, sl]
        qs = jnp.concatenate([jnp.where(m0, qp, zero), jnp.where(m0, zero, qp)], axis=0)
        kb = k_ref[pl.ds(starts[r], kh * GRID_W), sl]
        s[(r, pr)] = _dot_nt(qs, kb) + bias_ref[deltas[r], pr]
    e, l = {}, {}
    for c in chains:
        m = jnp.max(s[c], axis=-1, keepdims=True)
        p = jnp.exp(s[c] - m)
        l[c] = jnp.sum(p, axis=-1, keepdims=True)
        e[c] = p.astype(BF16)
    for r in range(rblk):
        outs = []
        for pr in range(npairs):
            sl = slice(pr * LANES, (pr + 1) * LANES)
            vb = v_ref[pl.ds(starts[r], kh * GRID_W), sl]
            pv = _dot(e[(r, pr)], vb) / l[(r, pr)]
            outs.append(jnp.where(m0, pv[:GRID_W], pv[GRID_W:]))
        o_ref[r * GRID_W:(r + 1) * GRID_W, :] = jnp.concatenate(outs, axis=1).astype(BF16)


def _na_bias_table(rpb, kh):
    cols = np.arange(GRID_W)
    col_start = np.clip(cols - NA_WIN_COLS // 2, 0, GRID_W - NA_WIN_COLS)
    col_mask = (cols[None, :] >= col_start[:, None]) & (cols[None, :] < col_start[:, None] + NA_WIN_COLS)
    col_off = np.clip(cols[None, :] - cols[:, None] + NA_WIN_COLS - 1, 0, 2 * NA_WIN_COLS - 2)
    delta = np.arange(kh)
    row_off = np.arange(kh)[None, :] - delta[:, None] + NA_MAX_WIN_ROWS - 1
    col_sel = (col_off[None] == np.arange(2 * NA_WIN_COLS - 1)[:, None, None]).astype(np.float32)
    row_sel = (row_off[:, :, None] == np.arange(2 * NA_MAX_WIN_ROWS - 1)).astype(np.float32)
    hi = lax.Precision.HIGHEST
    t = jnp.einsum("hoc,cqk->hoqk", rpb.astype(F32), col_sel, precision=hi)
    t = jnp.einsum("dro,hoqk->dhqrk", row_sel, t, precision=hi)
    t = jnp.where(col_mask[None, None, :, None, :], t, NEG_INF)
    return t.reshape(kh, NA_HEADS // PAIR, PAIR * GRID_W, kh * GRID_W)


def _natten(q, k, v, rpb, batch, seq):
    n = batch * seq
    rows = seq // GRID_W
    kh = min(NA_MAX_WIN_ROWS, rows)
    bias = _na_bias_table(rpb, kh)
    rblk = NA_ROWS_PER_STEP if rows % NA_ROWS_PER_STEP == 0 else 1
    nblk = rows // rblk

    return pl.pallas_call(
        functools.partial(_na_kernel, rows=rows, kh=kh, rblk=rblk),
        grid=(batch, nblk),
        in_specs=[
            pl.BlockSpec((rblk * GRID_W, NA_WIDTH), lambda b, i: (b * nblk + i, 0)),
            pl.BlockSpec((seq, NA_WIDTH), lambda b, i: (b, 0)),
            pl.BlockSpec((seq, NA_WIDTH), lambda b, i: (b, 0)),
            pl.BlockSpec((kh, NA_HEADS // PAIR, PAIR * GRID_W, kh * GRID_W), lambda b, i: (0, 0, 0, 0),
                         pipeline_mode=pl.Buffered(1)),
        ],
        out_specs=pl.BlockSpec((rblk * GRID_W, NA_WIDTH), lambda b, i: (b * nblk + i, 0)),
        out_shape=jax.ShapeDtypeStruct((n, NA_WIDTH), BF16),
        compiler_params=pltpu.CompilerParams(
            dimension_semantics=("parallel", "arbitrary"), vmem_limit_bytes=VMEM_LIMIT_BYTES),
        name="natten2d",
    )(q, k, v, bias)


def _route_t(logits_t):
    tm = logits_t.shape[1]
    sub = lax.broadcasted_iota(jnp.int32, (SUBLANES, tm), 0)
    ninf = -jnp.inf
    gvalid = sub < N_GROUPS
    gl = jnp.where(gvalid, logits_t[0:SUBLANES], ninf)
    gmax = jnp.max(gl, axis=0, keepdims=True)
    g_sel = jnp.min(jnp.where(gl == gmax, sub, SUBLANES), axis=0, keepdims=True)
    p_group = 1.0 / jnp.sum(jnp.where(gvalid, jnp.exp(gl - gmax), 0.0), axis=0, keepdims=True)
    ev = logits_t[SUBLANES:2 * SUBLANES]
    for g in range(1, N_GROUPS):
        lo = SUBLANES + g * EXPERTS_PER_GROUP
        ev = jnp.where(g_sel == g, logits_t[lo:lo + EXPERTS_PER_GROUP], ev)
    m1 = jnp.max(ev, axis=0, keepdims=True)
    i1 = jnp.min(jnp.where(ev == m1, sub, SUBLANES), axis=0, keepdims=True)
    ev2 = jnp.where(sub == i1, ninf, ev)
    m2 = jnp.max(ev2, axis=0, keepdims=True)
    i2 = jnp.min(jnp.where(ev2 == m2, sub, SUBLANES), axis=0, keepdims=True)
    t = jnp.exp(m2 - m1)
    w1 = p_group / (1.0 + t)
    w2 = p_group * t / (1.0 + t)
    e1 = g_sel * EXPERTS_PER_GROUP + i1
    e2 = g_sel * EXPERTS_PER_GROUP + i2
    rec = jnp.where(sub == 0, e1.astype(F32),
                    jnp.where(sub == 1, e2.astype(F32),
                              jnp.where(sub == 2, w1, jnp.where(sub == 3, w2, 0.0))))
    return rec, e1, e2


def _out_kernel(x_ref, yf_ref, yb_ref, bonf_ref, bonb_ref, g_ref, yna_ref, gate_ref,
                lng_ref, lnb_ref, bgate_ref, worw_ref, wona_ref, wout_ref, n2g_ref, wrh_ref, wrl_ref, br_ref,
                bd_ref, x1_ref, h2_ref, rt_ref, cnt_ref):
    bd = bd_ref[...]
    y = yf_ref[...] + yb_ref[...]
    yc = y - _seg_dot(y, bd)
    var = _grouped_dot((yc * yc).astype(BF16), bd)
    yn = yc * lax.rsqrt(var + GN_EPS) * lng_ref[...] + lnb_ref[...]
    ya = (yn + bonf_ref[...].astype(F32) + bonb_ref[...].astype(F32)) * g_ref[...].astype(F32)
    ya_o = _dot(ya.astype(BF16), worw_ref[...])
    yb_o = _dot(yna_ref[...], wona_ref[...])
    gl = gate_ref[...].astype(F32) + bgate_ref[...]
    mix = jax.nn.sigmoid(gl[:, :D_MODEL]) * ya_o + jax.nn.sigmoid(gl[:, D_MODEL:]) * yb_o
    x1 = x_ref[...] + _dot(mix.astype(BF16), wout_ref[...])
    x1_ref[...] = x1
    ms = jnp.mean(x1 * x1, axis=-1, keepdims=True)
    h2 = x1 * lax.rsqrt(ms + RMS_EPS) * n2g_ref[...]
    h2_hi, h2_lo = _split2(h2)
    h2_ref[...] = h2_hi
    logits_t = (_dot_nt(wrh_ref[...], h2_hi) + _dot_nt(wrh_ref[...], h2_lo)
                + _dot_nt(wrl_ref[...], h2_hi) + br_ref[...])
    rec, e1, e2 = _route_t(logits_t)
    rt_ref[...] = rec
    rows_e = lax.broadcasted_iota(jnp.int32, (N_EXPERTS, rec.shape[1]), 0)
    chosen = jnp.where((rows_e == e1) | (rows_e == e2), 1.0, 0.0)
    cnt_ref[0] = jnp.broadcast_to(jnp.sum(chosen, axis=1, keepdims=True), (N_EXPERTS, LANES))


def _out_proj(xf, yf, yb, bonf, bonb, g, yna, gate, lnx_g, lnx_b, b_gate, w_o_rwkv, w_o_na, w_out,
              norm2_g, w_rg, b_rg, w_re, b_re, tm):
    n = xf.shape[0]
    w = RWKV_WIDTH
    assert EXPERTS_PER_GROUP == SUBLANES and N_GROUPS <= SUBLANES
    gpad = SUBLANES - N_GROUPS
    wr_t = jnp.concatenate([w_rg.T, jnp.zeros((gpad, D_MODEL), F32), w_re.T], axis=0)
    wr_hi = wr_t.astype(BF16)
    wr_lo = (wr_t - wr_hi.astype(F32)).astype(BF16)
    br = jnp.concatenate([b_rg, jnp.zeros((gpad,), F32), b_re]).reshape(ROUTER_ROWS, 1)
    bd = _block_diag_heads(1.0 / HEAD_DIM)
    const = lambda i: (0, 0)
    row = lambda i: (i, 0)
    rs = lambda c: pl.BlockSpec((tm, c), row)
    cs = lambda r, c: pl.BlockSpec((r, c), const)
    return pl.pallas_call(
        _out_kernel,
        grid=(n // tm,),
        in_specs=[rs(D_MODEL), rs(w), rs(w), rs(w), rs(w), rs(w), rs(NA_WIDTH), rs(GATE_COLS),
                  cs(1, w), cs(1, w), cs(1, GATE_COLS), cs(w, D_MODEL), cs(NA_WIDTH, D_MODEL),
                  cs(D_MODEL, D_MODEL), cs(1, D_MODEL), cs(ROUTER_ROWS, D_MODEL), cs(ROUTER_ROWS, D_MODEL),
                  cs(ROUTER_ROWS, 1), cs(MXU_DIM, MXU_DIM)],
        out_specs=[rs(D_MODEL), rs(D_MODEL), pl.BlockSpec((ROUTE_REC, tm), lambda i: (0, i)),
                   pl.BlockSpec((1, N_EXPERTS, LANES), lambda i: (i, 0, 0))],
        out_shape=[jax.ShapeDtypeStruct((n, D_MODEL), F32),
                   jax.ShapeDtypeStruct((n, D_MODEL), BF16),
                   jax.ShapeDtypeStruct((ROUTE_REC, n), F32),
                   jax.ShapeDtypeStruct((n // tm, N_EXPERTS, LANES), F32)],
        compiler_params=pltpu.CompilerParams(
            dimension_semantics=("parallel",), vmem_limit_bytes=VMEM_LIMIT_BYTES),
        name="out_proj",
    )(xf, yf, yb, bonf, bonb, g, yna, gate,
      lnx_g.reshape(1, w), lnx_b.reshape(1, w), b_gate.reshape(1, GATE_COLS),
      w_o_rwkv.astype(BF16), w_o_na.astype(BF16), w_out.astype(BF16),
      norm2_g.reshape(1, D_MODEL), wr_hi, wr_lo, br, bd)


def _slot_rows(tm):
    rows = 2 * tm + N_EXPERTS * (SEG - 1)
    return -(-rows // LANES) * LANES


def _moe_plan(cnt, rb, nb_max, pieces_per_tile):
    ntiles = cnt.shape[0]
    e_before = jnp.asarray(np.tril(np.ones((N_EXPERTS, N_EXPERTS), np.int32), -1))
    t_before = jnp.asarray(np.tril(np.ones((ntiles, ntiles), np.int32), -1))
    pc = ((cnt + SEG - 1) // SEG) * SEG
    loc = jnp.sum(pc[:, None, :] * e_before[None], axis=2)
    within = jnp.sum(t_before[:, :, None] * pc[None], axis=1)
    filled = jnp.sum(pc, axis=0)
    region = ((filled + rb - 1) // rb) * rb
    region_off = jnp.sum(region[None, :] * e_before, axis=1)
    gbase = region_off[None, :] + within
    blk_end = (region_off + region) // rb
    nused = blk_end[-1:]
    blk = jnp.minimum(jnp.arange(nb_max, dtype=jnp.int32), nused - 1)
    blk_expert = jnp.minimum(jnp.sum((blk_end[None, :] <= blk[:, None]).astype(jnp.int32), axis=1),
                             N_EXPERTS - 1)
    n16 = pc // SEG
    end16 = (loc + pc) // SEG
    piece = jnp.arange(pieces_per_tile, dtype=jnp.int32)
    seg_of_piece = jnp.minimum(jnp.sum((end16[:, None, :] <= piece[None, :, None]).astype(jnp.int32), axis=2),
                               N_EXPERTS - 1)
    onehot = (seg_of_piece[:, :, None] == jnp.arange(N_EXPERTS)[None, None, :]).astype(jnp.int32)
    piece_row = jnp.sum(onehot * (gbase - loc)[:, None, :], axis=2) + piece[None, :] * SEG
    i32 = lambda a: a.reshape(-1).astype(jnp.int32)
    return (loc, i32(piece_row), i32(jnp.sum(n16, axis=1)), i32(blk_expert), i32(nused),
            i32(region_off + filled), i32((region - filled) // SEG))


def _piece_copies(local_ref, global_ref, row_s, tile, count, sem, outbound, pieces_per_tile):
    def start(p, priority):
        loc_rows = local_ref.at[pl.ds(pl.multiple_of(p * SEG, SEG), SEG)]
        glob_rows = global_ref.at[pl.ds(pl.multiple_of(row_s[tile * pieces_per_tile + p], SEG), SEG)]
        if outbound:
            pltpu.make_async_copy(loc_rows, glob_rows, sem).start(priority=priority)
        else:
            pltpu.make_async_copy(glob_rows, loc_rows, sem).start(priority=priority)

    def pair(q, carry):
        start(2 * q, 0)
        start(2 * q + 1, 1)
        return carry

    lax.fori_loop(0, count // 2, pair, 0)

    @pl.when(count % 2 == 1)
    def _():
        start(count - 1, 0)


def _wait_copies(src_ref, dst_ref, count, sem):
    def one(b, carry):
        pltpu.make_async_copy(src_ref.at[pl.ds(0, SEG)], dst_ref.at[pl.ds(0, SEG)], sem).wait()
        return carry

    lax.fori_loop(0, count, one, 0)


def _dispatch_kernel(row_s, npiece_s, tail_s, tailn_s, nused_s, h_ref, rt_ref, locc_ref, hs_ref,
                     sorted_ref, zero_ref, sems, *, nb_max, rb):
    tile = pl.program_id(0)
    ntiles = pl.num_programs(0)
    slot = tile % 2
    sorted_ref = sorted_ref.at[slot]
    sem = sems.at[slot]
    pieces_per_tile = sorted_ref.shape[0] // SEG

    @pl.when(tile >= 2)
    def _():
        _wait_copies(sorted_ref, hs_ref, npiece_s[tile - 2], sem)

    tm = h_ref.shape[0]
    rt = rt_ref[...]
    e1 = rt[0:1].astype(jnp.int32)
    e2 = rt[1:2].astype(jnp.int32)
    rows_e = lax.broadcasted_iota(jnp.int32, (N_EXPERTS, tm), 0)
    ind1 = rows_e == e1
    ind2 = rows_e == e2
    ind = jnp.where(ind1 | ind2, 1.0, 0.0).astype(BF16)
    earlier = jnp.where(lax.broadcasted_iota(jnp.int32, (tm, tm), 0) < lax.broadcasted_iota(jnp.int32, (tm, tm), 1),
                        1.0, 0.0).astype(BF16)
    pos = locc_ref[0] + _dot(ind, earlier)
    pos1 = jnp.sum(jnp.where(ind1, pos, 0.0), axis=0, keepdims=True).astype(jnp.int32)
    pos2 = jnp.sum(jnp.where(ind2, pos, 0.0), axis=0, keepdims=True).astype(jnp.int32)
    row_id = lax.broadcasted_iota(jnp.int32, (sorted_ref.shape[0], tm), 0)
    onehot = jnp.where((row_id == pos1) | (row_id == pos2), 1.0, 0.0).astype(BF16)
    sorted_ref[...] = _dot(onehot, h_ref[...]).astype(BF16)
    count = npiece_s[tile]
    _piece_copies(sorted_ref, hs_ref, row_s, tile, count, sem, True, pieces_per_tile)

    @pl.when(tile == ntiles - 1)
    def _():
        zero_ref[...] = jnp.zeros_like(zero_ref)

        def tail(e, total):
            def piece(b, carry):
                pltpu.make_async_copy(
                    zero_ref.at[pl.ds(0, SEG)],
                    hs_ref.at[pl.ds(pl.multiple_of(tail_s[e] + b * SEG, SEG), SEG)], sems.at[2]).start()
                return carry

            lax.fori_loop(0, tailn_s[e], piece, 0)
            return total + tailn_s[e]

        n_tail = lax.fori_loop(0, N_EXPERTS, tail, 0)

        def block(b, carry):
            pltpu.make_async_copy(zero_ref, hs_ref.at[pl.ds(pl.multiple_of(b * rb, rb), rb)], sems.at[3]).start()
            return carry

        lax.fori_loop(nused_s[0], nb_max, block, 0)
        _wait_copies(zero_ref, hs_ref, n_tail, sems.at[2])

        def block_wait(b, carry):
            pltpu.make_async_copy(zero_ref, hs_ref.at[pl.ds(0, rb)], sems.at[3]).wait()
            return carry

        lax.fori_loop(nused_s[0], nb_max, block_wait, 0)
        _wait_copies(sorted_ref, hs_ref, count, sem)

        @pl.when(tile >= 1)
        def _():
            _wait_copies(sorted_ref, hs_ref, npiece_s[tile - 1], sems.at[1 - slot])


def _ffn_kernel(be_s, nused_s, hs_ref, wg_ref, wu_ref, wd_ref, o_ref, wgb_ref, wub_ref, wdb_ref):
    b = pl.program_id(0)
    used = b < nused_s[0]

    @pl.when(used & ((b == 0) | (be_s[b] != be_s[jnp.maximum(b - 1, 0)])))
    def _():
        wgb_ref[...] = wg_ref[0].astype(BF16)
        wub_ref[...] = wu_ref[0].astype(BF16)
        wdb_ref[...] = wd_ref[0].astype(BF16)

    @pl.when(used)
    def _():
        x = hs_ref[...]
        gate = _dot(x, wgb_ref[...])
        up = _dot(x, wub_ref[...])
        he = (gate * jax.nn.sigmoid(gate) * up).astype(BF16)
        o_ref[...] = _dot(he, wdb_ref[...]).astype(BF16)

    @pl.when(jnp.logical_not(used))
    def _():
        o_ref[...] = jnp.zeros_like(o_ref)


def _combine_kernel(row_s, npiece_s, x1_ref, rc_ref, locr_ref, ho_ref, out_ref, obuf_ref, sems):
    tile = pl.program_id(0)
    ntiles = pl.num_programs(0)
    slot = tile % 2
    tm = x1_ref.shape[0]
    pieces_per_tile = obuf_ref.shape[1] // SEG

    @pl.when(tile == 0)
    def _():
        obuf_ref[...] = jnp.zeros_like(obuf_ref)
        _piece_copies(obuf_ref.at[0], ho_ref, row_s, 0, npiece_s[0], sems.at[0], False, pieces_per_tile)

    @pl.when(tile + 1 < ntiles)
    def _():
        _piece_copies(obuf_ref.at[1 - slot], ho_ref, row_s, tile + 1, npiece_s[tile + 1], sems.at[1 - slot],
                      False, pieces_per_tile)

    obuf_ref = obuf_ref.at[slot]
    rc = rc_ref[...]
    e1 = rc[:, 0:1].astype(jnp.int32)
    e2 = rc[:, 1:2].astype(jnp.int32)
    lane_e = lax.broadcasted_iota(jnp.int32, (tm, LANES), 1)
    ind1 = lane_e == e1
    ind2 = lane_e == e2
    ind = jnp.where(ind1 | ind2, 1.0, 0.0).astype(BF16)
    earlier = jnp.where(lax.broadcasted_iota(jnp.int32, (tm, tm), 1) < lax.broadcasted_iota(jnp.int32, (tm, tm), 0),
                        1.0, 0.0).astype(BF16)
    pos = locr_ref[0] + _dot(earlier, ind)
    pos1 = jnp.sum(jnp.where(ind1, pos, 0.0), axis=1, keepdims=True).astype(jnp.int32)
    pos2 = jnp.sum(jnp.where(ind2, pos, 0.0), axis=1, keepdims=True).astype(jnp.int32)
    row_id = lax.broadcasted_iota(jnp.int32, (tm, obuf_ref.shape[0]), 1)
    pick = jnp.where(row_id == pos1, rc[:, 2:3], jnp.where(row_id == pos2, rc[:, 3:4], 0.0)).astype(BF16)
    _wait_copies(ho_ref, obuf_ref, npiece_s[tile], sems.at[slot])
    out_ref[...] = x1_ref[...] + _dot(pick, obuf_ref[...])


def _moe(h2, x1, rt, cnt, w_gate_e, w_up_e, w_down_e, tm, rb):
    n = h2.shape[0]
    ntiles = n // tm
    j_rows = _slot_rows(tm)
    nb_max = -(-(2 * n + ntiles * N_EXPERTS * (SEG - 1) + N_EXPERTS * (rb - 1)) // rb)
    rmax = nb_max * rb
    loc, row_s, npiece_s, blk_expert, nused, tail_s, tailn_s = _moe_plan(
        cnt[:, :, 0].astype(jnp.int32), rb, nb_max, j_rows // SEG)
    loc_col = loc.astype(F32).reshape(ntiles, N_EXPERTS, 1)
    loc_row = jnp.pad(loc.astype(F32), ((0, 0), (0, LANES - N_EXPERTS))).reshape(ntiles, 1, LANES)
    any_spec = pl.BlockSpec(memory_space=pl.ANY)
    params = pltpu.CompilerParams(dimension_semantics=("arbitrary",), vmem_limit_bytes=VMEM_LIMIT_BYTES)

    hs = pl.pallas_call(
        functools.partial(_dispatch_kernel, nb_max=nb_max, rb=rb),
        grid_spec=pltpu.PrefetchScalarGridSpec(
            num_scalar_prefetch=5, grid=(ntiles,),
            in_specs=[pl.BlockSpec((tm, D_MODEL), lambda i, *_: (i, 0)),
                      pl.BlockSpec((ROUTE_REC, tm), lambda i, *_: (0, i)),
                      pl.BlockSpec((1, N_EXPERTS, 1), lambda i, *_: (i, 0, 0))],
            out_specs=any_spec,
            scratch_shapes=[pltpu.VMEM((2, j_rows, D_MODEL), BF16), pltpu.VMEM((rb, D_MODEL), BF16),
                            pltpu.SemaphoreType.DMA((4,))]),
        out_shape=jax.ShapeDtypeStruct((rmax, D_MODEL), BF16),
        compiler_params=params,
        name="moe_dispatch",
    )(row_s, npiece_s, tail_s, tailn_s, nused, h2, rt, loc_col)

    blk_row = lambda b, be, nu: (jnp.minimum(b, nu[0] - 1), 0)
    ho = pl.pallas_call(
        _ffn_kernel,
        grid_spec=pltpu.PrefetchScalarGridSpec(
            num_scalar_prefetch=2, grid=(nb_max,),
            in_specs=[pl.BlockSpec((rb, D_MODEL), blk_row),
                      pl.BlockSpec((1, D_MODEL, D_EXPERT), lambda b, be, nu: (be[b], 0, 0)),
                      pl.BlockSpec((1, D_MODEL, D_EXPERT), lambda b, be, nu: (be[b], 0, 0)),
                      pl.BlockSpec((1, D_EXPERT, D_MODEL), lambda b, be, nu: (be[b], 0, 0))],
            out_specs=pl.BlockSpec((rb, D_MODEL), lambda b, be, nu: (b, 0)),
            scratch_shapes=[pltpu.VMEM((D_MODEL, D_EXPERT), BF16), pltpu.VMEM((D_MODEL, D_EXPERT), BF16),
                            pltpu.VMEM((D_EXPERT, D_MODEL), BF16)]),
        out_shape=jax.ShapeDtypeStruct((rmax, D_MODEL), BF16),
        compiler_params=params,
        name="moe_ffn",
    )(blk_expert, nused, hs, w_gate_e, w_up_e, w_down_e)

    return pl.pallas_call(
        _combine_kernel,
        grid_spec=pltpu.PrefetchScalarGridSpec(
            num_scalar_prefetch=2, grid=(ntiles,),
            in_specs=[pl.BlockSpec((tm, D_MODEL), lambda i, *_: (i, 0)),
                      pl.BlockSpec((tm, ROUTE_REC), lambda i, *_: (i, 0)),
                      pl.BlockSpec((1, 1, LANES), lambda i, *_: (i, 0, 0)),
                      any_spec],
            out_specs=pl.BlockSpec((tm, D_MODEL), lambda i, *_: (i, 0)),
            scratch_shapes=[pltpu.VMEM((2, j_rows, D_MODEL), BF16), pltpu.SemaphoreType.DMA((2,))]),
        out_shape=jax.ShapeDtypeStruct((n, D_MODEL), F32),
        compiler_params=params,
        name="moe_combine",
    )(row_s, npiece_s, x1, rt.T, loc_row, ho)


def _row_tile(n, want):
    t = min(want, n)
    while n % t:
        t //= 2
    return t


def kernel(x, norm1_g, w_in, mu_prev, mu_next, w0_f, w_up_f, w0_b, w_up_b, a0_f, a_up_f, a0_b, a_up_b, g_up, k_k, k_a, r_k, lnx_g, lnx_b, q_gain, k_gain, rpb, b_gate, w_o_rwkv, w_o_na, w_out, norm2_g, w_router_group, b_router_group, w_router_expert, b_router_expert, w_gate_e, w_up_e, w_down_e):
    batch, seq, d = x.shape
    assert d == D_MODEL and seq % CHUNK == 0 and seq % GRID_W == 0
    n = batch * seq
    for l in range(norm1_g.shape[0]):
        xf = x.reshape(n, d)
        prw, q, k, v, gate = _in_proj(xf, norm1_g[l], w_in[l].astype(BF16), q_gain[l], k_gain[l],
                                      _row_tile(n, 512))
        yf, bonf, g, yb, bonb = _rwkv(prw, batch, seq, mu_prev[l], mu_next[l], w0_f[l], w_up_f[l],
                                      w0_b[l], w_up_b[l], a0_f[l], a_up_f[l], a0_b[l], a_up_b[l],
                                      g_up[l], k_k[l], k_a[l], r_k[l].reshape(-1))
        yna = _natten(q, k, v, rpb[l], batch, seq)
        tm = _row_tile(n, MOE_TM)
        x1, h2, rt, cnt = _out_proj(xf, yf, yb, bonf, bonb, g, yna, gate, lnx_g[l], lnx_b[l], b_gate[l],
                                    w_o_rwkv[l], w_o_na[l], w_out[l], norm2_g[l],
                                    w_router_group[l], b_router_group[l],
                                    w_router_expert[l], b_router_expert[l], tm)
        out = _moe(h2, x1, rt, cnt, w_gate_e[l], w_up_e[l], w_down_e[l], tm, MOE_RB)
        x = out.reshape(batch, seq, d)
    return x
```
